```python
import math
import jax
import jax.numpy as jnp
from jax import lax
import numpy as np

D_MODEL = 1024
BATCH = 16
SEQ = 2048
DEPTH = 2

GRID_W = 64
CTX_LEN = 256
EPS = 1e-6
N_MIXERS = 4
D_MIX = D_MODEL
W_GRP = D_MIX // N_MIXERS
CHUNK = 128

SSD_HEAD_DIM = 64
SSD_HEADS = W_GRP // SSD_HEAD_DIM
SSD_STATE = 64
SSD_CONV = 3
SSD_XBC = W_GRP + 2 * SSD_STATE
SSD_COLS = W_GRP + SSD_XBC + 2 * SSD_HEADS

HY_ORDER = 2
HY_EMB = 33
HY_BANDS = (HY_EMB - 1) // 2
HY_FILT = 64
HY_CONV = 3
HY_DECAY_SHORT = 0.3
HY_DECAY_LONG = 1.5
HY_DECAY_TARGET = 1e-2
HY_COLS = (HY_ORDER + 1) * W_GRP

RET_HEADS = 4
RET_HEAD_DIM = W_GRP // RET_HEADS
ROPE_BASE = 10000.0
RET_COLS = 4 * W_GRP

S5_CH = 16
S5_GROUPS = W_GRP // S5_CH
S5_STATE = 64
S5_DT_MIN = 1e-3
S5_DT_MAX = 1e-1
S5_COLS = W_GRP

OFF_HY = SSD_COLS
OFF_RET = OFF_HY + HY_COLS
OFF_S5 = OFF_RET + RET_COLS
D_IN = OFF_S5 + S5_COLS

D_FF = -(-8 * D_MODEL // (3 * 256)) * 256

kernel_name = 'hybrid_ssd_hyena_retention_s5_dit'


def rms_norm(x, g):
    xf = x.astype(jnp.float32)
    y = xf * lax.rsqrt(jnp.mean(xf * xf, axis=-1, keepdims=True) + EPS)
    return (y * g.astype(jnp.float32)).astype(x.dtype)


def modulate(x, shift, scale):
    return x * (1.0 + scale) + shift


def dwconv(x, w, b):
    k = w.shape[0]
    y = lax.conv_general_dilated(x, w[:, None, :].astype(x.dtype), window_strides=(1,),
                                 padding=((k // 2, k // 2),),
                                 dimension_numbers=('NWC', 'WIO', 'NWC'),
                                 feature_group_count=x.shape[-1])
    return y + b


def chunked_scan(q, k, v, log_a, h0):
    f32 = jnp.float32
    bsz, L, H, N = q.shape
    P = v.shape[-1]
    nc = L // CHUNK
    qc = q.astype(f32).reshape(bsz, nc, CHUNK, H, N)
    kc = k.astype(f32).reshape(bsz, nc, CHUNK, H, N)
    vc = v.astype(f32).reshape(bsz, nc, CHUNK, H, P)
    acs = jnp.cumsum(log_a.astype(f32).reshape(bsz, nc, CHUNK, H), axis=2)
    acs_h = jnp.swapaxes(acs, 2, 3)
    seg = acs_h[..., :, None] - acs_h[..., None, :]
    lower = jnp.tril(jnp.ones((CHUNK, CHUNK), dtype=bool))
    decay = jnp.exp(jnp.where(lower, seg, -jnp.inf))
    scores = jnp.einsum('bcihn,bcjhn->bchij', qc, kc) * decay
    y = jnp.einsum('bchij,bcjhp->bcihp', scores, vc)
    to_end = jnp.exp(acs[:, :, -1:, :] - acs)
    states = jnp.einsum('bcjhn,bcjh,bcjhp->bchnp', kc, to_end, vc)
    chunk_decay = jnp.exp(acs[:, :, -1, :])

    def step(h, inp):
        s, d = inp
        return h * d[:, :, None, None] + s, h

    h_last, h_in = lax.scan(step, h0.astype(f32),
                            (jnp.moveaxis(states, 1, 0), jnp.moveaxis(chunk_decay, 1, 0)))
    h_in = jnp.moveaxis(h_in, 0, 1)
    y = y + jnp.einsum('bcihn,bchnp->bcihp', qc * jnp.exp(acs)[..., None], h_in)
    return y.reshape(bsz, L, H, P).astype(v.dtype), h_last


def two_stream_scan(q_c, k_c, v_c, la_c, q_l, k_l, v_l, la_l, reverse):
    if reverse:
        q_c, k_c, v_c, la_c = (jnp.flip(q_c, 1), jnp.flip(k_c, 1), jnp.flip(v_c, 1), jnp.flip(la_c, 1))
        q_l, k_l, v_l, la_l = (jnp.flip(q_l, 1), jnp.flip(k_l, 1), jnp.flip(v_l, 1), jnp.flip(la_l, 1))
    bsz, _, H, N = q_c.shape
    h0 = jnp.zeros((bsz, H, N, v_c.shape[-1]), jnp.float32)
    y_c, h_c = chunked_scan(q_c, k_c, v_c, la_c, h0)
    y_l, _ = chunked_scan(q_l, k_l, v_l, la_l, h_c)
    if reverse:
        y_c, y_l = jnp.flip(y_c, 1), jnp.flip(y_l, 1)
    return y_c, y_l


def ssd_prep(u, conv_w, conv_b, dt_bias):
    bsz, L, _ = u.shape
    z = u[..., :W_GRP]
    xbc = jax.nn.silu(dwconv(u[..., W_GRP:W_GRP + SSD_XBC], conv_w, conv_b))
    xs = xbc[..., :W_GRP].reshape(bsz, L, SSD_HEADS, SSD_HEAD_DIM)
    bm = xbc[..., W_GRP:W_GRP + SSD_STATE]
    cm = xbc[..., W_GRP + SSD_STATE:]
    dt = jax.nn.softplus(u[..., W_GRP + SSD_XBC:].reshape(bsz, L, 2, SSD_HEADS) + dt_bias)
    return z, xs, bm, cm, dt


def ssd_direction_args(xs, bm, cm, dt_dir, a_dir):
    bsz, L = bm.shape[:2]
    q = jnp.broadcast_to(cm[:, :, None, :], (bsz, L, SSD_HEADS, SSD_STATE))
    k = bm[:, :, None, :] * dt_dir[..., None]
    return q, k, xs, dt_dir.astype(jnp.float32) * a_dir


def ssd_mixer(u_c, u_l, conv_w, conv_b, a_log, dt_bias, d_skip, norm_g):
    z_c, xs_c, b_c, c_c, dt_c = ssd_prep(u_c, conv_w, conv_b, dt_bias)
    z_l, xs_l, b_l, c_l, dt_l = ssd_prep(u_l, conv_w, conv_b, dt_bias)
    a = -jnp.exp(a_log.astype(jnp.float32))
    y_c = xs_c * d_skip[:, None]
    y_l = xs_l * d_skip[:, None]
    for r in range(2):
        yc, yl = two_stream_scan(*ssd_direction_args(xs_c, b_c, c_c, dt_c[:, :, r], a[r]),
                                 *ssd_direction_args(xs_l, b_l, c_l, dt_l[:, :, r], a[r]),
                                 reverse=(r == 1))
        y_c = y_c + yc
        y_l = y_l + yl
    out_c = rms_norm(y_c.reshape(z_c.shape) * jax.nn.silu(z_c), norm_g)
    out_l = rms_norm(y_l.reshape(z_l.shape) * jax.nn.silu(z_l), norm_g)
    return out_c, out_l


def hyena_filter(L, w1, b1, freq, w2, b2, w3):
    f32 = jnp.float32
    t = jnp.linspace(0.0, 1.0, L, dtype=f32)[:, None]
    w = 2.0 * math.pi * jnp.arange(L, dtype=f32)[:, None] / L
    bands = jnp.linspace(1e-4, HY_BANDS - 1, HY_BANDS, dtype=f32)[None, :]
    feats = jnp.concatenate([t, jnp.cos(bands * w), -jnp.sin(bands * w)], axis=-1)
    h = jnp.sin(freq * (feats @ w1 + b1))
    h = jnp.sin(freq * (h @ w2 + b2))
    h = (h @ w3).astype(f32)
    max_decay = math.log(HY_DECAY_TARGET) / HY_DECAY_SHORT
    min_decay = math.log(HY_DECAY_TARGET) / HY_DECAY_LONG
    deltas = jnp.abs(jnp.linspace(min_decay, max_decay, h.shape[-1], dtype=f32))
    h = h * jnp.exp(-t * deltas)
    return h.reshape(L, 2, HY_ORDER, W_GRP)


def two_sided_filter(h_f, h_b):
    L, C = h_f.shape
    k = jnp.concatenate([h_f, jnp.zeros((1, C), h_f.dtype), jnp.flip(h_b[1:], axis=0)], axis=0)
    return k / (jnp.sum(jnp.abs(k), axis=0, keepdims=True) + EPS)


def fft_conv(u, filt):
    L = u.shape[1]
    U = jnp.fft.rfft(u.astype(jnp.float32), n=2 * L, axis=1)
    F = jnp.fft.rfft(filt.astype(jnp.float32), axis=0)
    return jnp.fft.irfft(U * F, n=2 * L, axis=1)[:, :L].astype(u.dtype)


def hyena_stream(u, conv_w, conv_b, w1, b1, freq, w2, b2, w3, bias):
    L = u.shape[1]
    u = dwconv(u, conv_w, conv_b)
    x1, x2, z = jnp.split(u, 3, axis=-1)
    filt = hyena_filter(L, w1, b1, freq, w2, b2, w3)
    for o, gate in enumerate((x1, x2)):
        f2 = two_sided_filter(filt[:, 0, o], filt[:, 1, o])
        z = gate * (fft_conv(z, f2) + z * bias[o])
    return z


def rope_half(t, pos):
    f = t.shape[-1] // 2
    inv = ROPE_BASE ** (-jnp.arange(f, dtype=jnp.float32) / f)
    ang = pos.astype(jnp.float32)[:, None] * inv
    cos = jnp.cos(ang)[:, None, :]
    sin = jnp.sin(ang)[:, None, :]
    t1, t2 = t[..., :f], t[..., f:]
    return jnp.concatenate([t1 * cos - t2 * sin, t1 * sin + t2 * cos], axis=-1).astype(t.dtype)


def axial_rope(t, row_id, col_id):
    h = t.shape[-1] // 2
    return jnp.concatenate([rope_half(t[..., :h], row_id), rope_half(t[..., h:], col_id)], axis=-1)


def retention_out(y, g):
    yf = y.astype(jnp.float32)
    mu = jnp.mean(yf, axis=-1, keepdims=True)
    var = jnp.mean(jnp.square(yf - mu), axis=-1, keepdims=True)
    yn = ((yf - mu) * lax.rsqrt(var + EPS)).astype(g.dtype)
    return (jax.nn.silu(g) * yn).reshape(g.shape[0], g.shape[1], W_GRP)


def retention_mixer(u_c, u_l, decay_param, row_id, col_id):
    bc, lc = u_c.shape[:2]
    bl, ll = u_l.shape[:2]
    qc, kc, vc, gc = [t.reshape(bc, lc, RET_HEADS, RET_HEAD_DIM) for t in jnp.split(u_c, 4, axis=-1)]
    ql, kl, vl, gl = [t.reshape(bl, ll, RET_HEADS, RET_HEAD_DIM) for t in jnp.split(u_l, 4, axis=-1)]
    ql = axial_rope(ql, row_id, col_id)
    kl = axial_rope(kl, row_id, col_id)
    scale = RET_HEAD_DIM ** -0.5
    log_gamma = -jnp.exp(decay_param.astype(jnp.float32))
    y_c = jnp.zeros(vc.shape, jnp.float32)
    y_l = jnp.zeros(vl.shape, jnp.float32)
    for r in range(2):
        la_c = jnp.broadcast_to(log_gamma[r], (bc, lc, RET_HEADS))
        la_l = jnp.broadcast_to(log_gamma[r], (bl, ll, RET_HEADS))
        yc, yl = two_stream_scan(qc, kc * scale, vc, la_c, ql, kl * scale, vl, la_l, reverse=(r == 1))
        y_c = y_c + yc
        y_l = y_l + yl
    return retention_out(y_c, gc), retention_out(y_l, gl)


def s5_discretize(a_re, a_im, log_dt):
    f32 = jnp.float32
    dt = jnp.exp(log_dt.astype(f32))[:, None]
    a_re = a_re.astype(f32)
    a_im = a_im.astype(f32)
    mag = jnp.exp(a_re * dt)
    ab_re = mag * jnp.cos(a_im * dt)
    ab_im = mag * jnp.sin(a_im * dt)
    den = a_re * a_re + a_im * a_im
    z_re = ((ab_re - 1.0) * a_re + ab_im * a_im) / den
    z_im = (ab_im * a_re - (ab_re - 1.0) * a_im) / den
    return ab_re, ab_im, z_re, z_im


def s5_combine(e1, e2):
    a1r, a1i, b1r, b1i = e1
    a2r, a2i, b2r, b2i = e2
    return (a2r * a1r - a2i * a1i, a2r * a1i + a2i * a1r,
            a2r * b1r - a2i * b1i + b2r, a2r * b1i + a2i * b1r + b2i)


def s5_scan(ab_re, ab_im, bu_re, bu_im, h0_re, h0_im):
    b_re = bu_re.at[:, 0].add(ab_re * h0_re - ab_im * h0_im)
    b_im = bu_im.at[:, 0].add(ab_re * h0_im + ab_im * h0_re)
    a_re = jnp.broadcast_to(ab_re, b_re.shape)
    a_im = jnp.broadcast_to(ab_im, b_im.shape)
    _, _, h_re, h_im = lax.associative_scan(s5_combine, (a_re, a_im, b_re, b_im), axis=1)
    return h_re, h_im


def s5_direction(u, h0_re, h0_im, ab_re, ab_im, bb_re, bb_im, c_re, c_im, reverse):
    if reverse:
        u = jnp.flip(u, 1)
    bu_re = jnp.einsum('gpc,blgc->blgp', bb_re, u)
    bu_im = jnp.einsum('gpc,blgc->blgp', bb_im, u)
    h_re, h_im = s5_scan(ab_re, ab_im, bu_re, bu_im, h0_re, h0_im)
    y = (jnp.einsum('gcp,blgp->blgc', c_re.astype(jnp.float32), h_re)
         - jnp.einsum('gcp,blgp->blgc', c_im.astype(jnp.float32), h_im))
    if reverse:
        y = jnp.flip(y, 1)
    return y, h_re[:, -1], h_im[:, -1]


def s5_glu(y, w, b, dtype):
    y = jax.nn.gelu(y.reshape(y.shape[0], y.shape[1], W_GRP)).astype(dtype)
    return y * jax.nn.sigmoid(y @ w + b)


def s5_mixer(u_c, u_l, a_re, a_im, log_dt, b_re, b_im, c_re, c_im, d_skip, glu_w, glu_b):
    f32 = jnp.float32
    uc = u_c.astype(f32).reshape(u_c.shape[0], u_c.shape[1], S5_GROUPS, S5_CH)
    ul = u_l.astype(f32).reshape(u_l.shape[0], u_l.shape[1], S5_GROUPS, S5_CH)
    dg = d_skip.astype(f32).reshape(S5_GROUPS, S5_CH)
    y_c = uc * dg
    y_l = ul * dg
    b_re = b_re.astype(f32)
    b_im = b_im.astype(f32)
    zero = jnp.zeros((u_c.shape[0], S5_GROUPS, S5_STATE), f32)
    for r in range(2):
        ab_re, ab_im, z_re, z_im = s5_discretize(a_re[r], a_im[r], log_dt[r])
        bb_re = z_re[..., None] * b_re - z_im[..., None] * b_im
        bb_im = z_re[..., None] * b_im + z_im[..., None] * b_re
        yc, hc_re, hc_im = s5_direction(uc, zero, zero, ab_re, ab_im, bb_re, bb_im,
                                        c_re[r], c_im[r], r == 1)
        yl, _, _ = s5_direction(ul, hc_re, hc_im, ab_re, ab_im, bb_re, bb_im,
                                c_re[r], c_im[r], r == 1)
        y_c = y_c + yc
        y_l = y_l + yl
    return s5_glu(y_c, glu_w, glu_b, u_c.dtype), s5_glu(y_l, glu_w, glu_b, u_l.dtype)


def swiglu(x, w_up, w_down):
    g, u = jnp.split(x @ w_up, 2, axis=-1)
    return (jax.nn.silu(g) * u) @ w_down


def split_cols(u):
    return jnp.split(u, [OFF_HY, OFF_RET, OFF_S5], axis=-1)


def setup_inputs(seed: int = 0) -> dict:
    key = jax.random.key(seed)
    ks = iter(jax.random.split(key, 64))
    f32 = jnp.float32
    L = DEPTH

    def nrm(shape, scale):
        return scale * jax.random.normal(next(ks), shape, f32)

    def unif(shape, lo, hi):
        return jax.random.uniform(next(ks), shape, f32, lo, hi)

    x = nrm((BATCH, SEQ, D_MODEL), 1.0)
    c = nrm((BATCH, D_MODEL), 1.0)
    ctx = nrm((BATCH, CTX_LEN, D_MODEL), 1.0)
    c_ctx = nrm((D_MODEL,), 1.0)
    mod_w = nrm((L, D_MODEL, 6 * D_MODEL), 0.5 * D_MODEL ** -0.5)
    mod_b = nrm((L, 6 * D_MODEL), 0.02)
    norm1_g = 1.0 + nrm((L, D_MODEL), 0.02)
    norm2_g = 1.0 + nrm((L, D_MODEL), 0.02)
    w_in = nrm((L, D_MODEL, D_IN), D_MODEL ** -0.5)
    w_out = nrm((L, D_MIX, D_MODEL), D_MIX ** -0.5)
    ssd_conv_w = nrm((L, SSD_CONV, SSD_XBC), SSD_CONV ** -0.5)
    ssd_conv_b = nrm((L, SSD_XBC), 0.02)
    ssd_a_log = jnp.log(unif((L, 2, SSD_HEADS), 1.0, 16.0))
    dt0 = jnp.exp(unif((L, 2, SSD_HEADS), math.log(1e-3), math.log(1e-1)))
    ssd_dt_bias = dt0 + jnp.log(-jnp.expm1(-dt0))
    ssd_d = 1.0 + nrm((L, SSD_HEADS), 0.02)
    ssd_norm_g = 1.0 + nrm((L, W_GRP), 0.02)
    hy_conv_w = nrm((L, HY_CONV, HY_COLS), HY_CONV ** -0.5)
    hy_conv_b = nrm((L, HY_COLS), 0.02)
    hy_w1 = nrm((L, HY_EMB, HY_FILT), HY_EMB ** -0.5)
    hy_b1 = nrm((L, HY_FILT), 0.02)
    hy_freq = 1.0 + nrm((L, HY_FILT), 0.02)
    hy_w2 = nrm((L, HY_FILT, HY_FILT), HY_FILT ** -0.5)
    hy_b2 = nrm((L, HY_FILT), 0.02)
    hy_w3 = nrm((L, HY_FILT, 2 * HY_ORDER * W_GRP), HY_FILT ** -0.5)
    hy_bias = nrm((L, HY_ORDER, W_GRP), 1.0)
    gam = 1.0 - 2.0 ** (-5.0 - jnp.arange(RET_HEADS, dtype=f32))
    ret_decay = jnp.log(-jnp.log(gam)) + nrm((L, 2, RET_HEADS), 0.01)
    n_idx = jnp.arange(S5_STATE, dtype=f32)
    s5_a_re = -0.5 + nrm((L, 2, S5_GROUPS, S5_STATE), 0.01)
    s5_a_im = math.pi * n_idx + nrm((L, 2, S5_GROUPS, S5_STATE), 0.01)
    s5_log_dt = unif((L, 2, S5_GROUPS), math.log(S5_DT_MIN), math.log(S5_DT_MAX))
    s5_b_re = nrm((L, S5_GROUPS, S5_STATE, S5_CH), (2 * S5_CH) ** -0.5)
    s5_b_im = nrm((L, S5_GROUPS, S5_STATE, S5_CH), (2 * S5_CH) ** -0.5)
    s5_c_re = nrm((L, 2, S5_GROUPS, S5_CH, S5_STATE), (2 * S5_STATE) ** -0.5)
    s5_c_im = nrm((L, 2, S5_GROUPS, S5_CH, S5_STATE), (2 * S5_STATE) ** -0.5)
    s5_d = nrm((L, W_GRP), 1.0)
    s5_glu_w = nrm((L, W_GRP, W_GRP), W_GRP ** -0.5)
    s5_glu_b = nrm((L, W_GRP), 0.02)
    ffn_w_up = nrm((L, D_MODEL, 2 * D_FF), D_MODEL ** -0.5)
    ffn_w_down = nrm((L, D_FF, D_MODEL), D_FF ** -0.5)
    final_norm_g = 1.0 + nrm((D_MODEL,), 0.02)
    return {'x': x, 'c': c, 'ctx': ctx, 'c_ctx': c_ctx, 'mod_w': mod_w, 'mod_b': mod_b,
            'norm1_g': norm1_g, 'norm2_g': norm2_g, 'w_in': w_in, 'w_out': w_out,
            'ssd_conv_w': ssd_conv_w, 'ssd_conv_b': ssd_conv_b, 'ssd_a_log': ssd_a_log,
            'ssd_dt_bias': ssd_dt_bias, 'ssd_d': ssd_d, 'ssd_norm_g': ssd_norm_g,
            'hy_conv_w': hy_conv_w, 'hy_conv_b': hy_conv_b, 'hy_w1': hy_w1, 'hy_b1': hy_b1,
            'hy_freq': hy_freq, 'hy_w2': hy_w2, 'hy_b2': hy_b2, 'hy_w3': hy_w3, 'hy_bias': hy_bias,
            'ret_decay': ret_decay, 's5_a_re': s5_a_re, 's5_a_im': s5_a_im, 's5_log_dt': s5_log_dt,
            's5_b_re': s5_b_re, 's5_b_im': s5_b_im, 's5_c_re': s5_c_re, 's5_c_im': s5_c_im,
            's5_d': s5_d, 's5_glu_w': s5_glu_w, 's5_glu_b': s5_glu_b,
            'ffn_w_up': ffn_w_up, 'ffn_w_down': ffn_w_down, 'final_norm_g': final_norm_g}


def reference(x, c, ctx, c_ctx, mod_w, mod_b, norm1_g, norm2_g, w_in, w_out,
              ssd_conv_w, ssd_conv_b, ssd_a_log, ssd_dt_bias, ssd_d, ssd_norm_g,
              hy_conv_w, hy_conv_b, hy_w1, hy_b1, hy_freq, hy_w2, hy_b2, hy_w3, hy_bias,
              ret_decay, s5_a_re, s5_a_im, s5_log_dt, s5_b_re, s5_b_im, s5_c_re, s5_c_im,
              s5_d, s5_glu_w, s5_glu_b, ffn_w_up, ffn_w_down, final_norm_g):
    n_lat = x.shape[1]
    rows = n_lat // GRID_W
    row_id = jnp.repeat(jnp.arange(rows), GRID_W)
    col_id = jnp.tile(jnp.arange(GRID_W), rows)
    silu_c = jax.nn.silu(c)
    silu_cc = jax.nn.silu(c_ctx)
    h_l, h_c = x, ctx
    for i in range(DEPTH):
        last = i == DEPTH - 1
        m_l = jnp.split((silu_c @ mod_w[i] + mod_b[i])[:, None, :], 6, axis=-1)
        m_c = jnp.split((silu_cc @ mod_w[i] + mod_b[i])[None, None, :], 6, axis=-1)
        ul = split_cols(modulate(rms_norm(h_l, norm1_g[i]), m_l[0], m_l[1]) @ w_in[i])
        uc = split_cols(modulate(rms_norm(h_c, norm1_g[i]), m_c[0], m_c[1]) @ w_in[i])
        ssd_c, ssd_l = ssd_mixer(uc[0], ul[0], ssd_conv_w[i], ssd_conv_b[i], ssd_a_log[i],
                                 ssd_dt_bias[i], ssd_d[i], ssd_norm_g[i])
        hy_args = (hy_conv_w[i], hy_conv_b[i], hy_w1[i], hy_b1[i], hy_freq[i],
                   hy_w2[i], hy_b2[i], hy_w3[i], hy_bias[i])
        hy_l = hyena_stream(ul[1], *hy_args)
        ret_c, ret_l = retention_mixer(uc[2], ul[2], ret_decay[i], row_id, col_id)
        s5c, s5l = s5_mixer(uc[3], ul[3], s5_a_re[i], s5_a_im[i], s5_log_dt[i], s5_b_re[i],
                            s5_b_im[i], s5_c_re[i], s5_c_im[i], s5_d[i], s5_glu_w[i], s5_glu_b[i])
        y_l = jnp.concatenate([ssd_l, hy_l, ret_l, s5l], axis=-1) @ w_out[i]
        h_l = h_l + m_l[2] * y_l
        h_l = h_l + m_l[5] * swiglu(modulate(rms_norm(h_l, norm2_g[i]), m_l[3], m_l[4]),
                                    ffn_w_up[i], ffn_w_down[i])
        if not last:
            hy_c = hyena_stream(uc[1], *hy_args)
            y_c = jnp.concatenate([ssd_c, hy_c, ret_c, s5c], axis=-1) @ w_out[i]
            h_c = h_c + m_c[2] * y_c
            h_c = h_c + m_c[5] * swiglu(modulate(rms_norm(h_c, norm2_g[i]), m_c[3], m_c[4]),
                                        ffn_w_up[i], ffn_w_down[i])
    return rms_norm(h_l, final_norm_g)
```

```python
import functools
import math

import numpy as np
import jax
import jax.numpy as jnp
from jax import lax
from jax.experimental import pallas as pl
from jax.experimental.pallas import tpu as pltpu

F32 = jnp.float32
BF16 = jnp.bfloat16
EPS = 1e-6

D_MODEL = 1024
W_GRP = 256
T = 128
N_HEADS = 4
HEAD_DIM = 64
SSD_STATE = 64
SSD_XBC = W_GRP + 2 * SSD_STATE
HY_COLS = 3 * W_GRP
RET_COLS = 4 * W_GRP
GRID_W = 64
ROPE_BASE = 10000.0
HY_EMB = 33
HY_BANDS = 16
HY_FILT = 64
S5_GROUPS = 16
S5_CH = 16
S5_STATE = 64
S5_NS = S5_GROUPS * S5_STATE
D_FF = 2816
S5_TS = 64
MOD_ROWS = 24

VMEM_LIMIT = 56 * 1024 * 1024


def _cparams(n_grid):
    return pltpu.CompilerParams(dimension_semantics=("arbitrary",) * n_grid,
                                vmem_limit_bytes=VMEM_LIMIT)


def _dot(a, b):
    return jnp.dot(a.astype(BF16), b.astype(BF16), preferred_element_type=F32)


def _dot_nt(a, b):
    return lax.dot_general(a.astype(BF16), b.astype(BF16), (((1,), (1,)), ((), ())),
                           preferred_element_type=F32)


def _dot_tn(a, b):
    return lax.dot_general(a.astype(BF16), b.astype(BF16), (((0,), (0,)), ((), ())),
                           preferred_element_type=F32)


def _dot_f32(a, b):
    return jnp.dot(a, b, preferred_element_type=F32, precision=lax.Precision.HIGHEST)


def _silu(x):
    return x * jax.nn.sigmoid(x)


def _const_spec(shape):
    nd = len(shape)
    return pl.BlockSpec(shape, lambda *_: (0,) * nd, pipeline_mode=pl.Buffered(1))


def _mod_kernel(sc_ref, w_ref, b_ref, o_ref):
    s = _silu(sc_ref[...])
    o_ref[0] = _dot_f32(s, w_ref[0]) + b_ref[0]


def _mod_call(sc, mod_w, mod_b):
    depth, _, n = mod_w.shape
    tn = 1536
    return pl.pallas_call(
        _mod_kernel,
        grid=(depth, n // tn),
        in_specs=[pl.BlockSpec((MOD_ROWS, D_MODEL), lambda l, j: (0, 0)),
                  pl.BlockSpec((1, D_MODEL, tn), lambda l, j: (l, 0, j)),
                  pl.BlockSpec((1, 1, tn), lambda l, j: (l, 0, j))],
        out_specs=pl.BlockSpec((1, MOD_ROWS, tn), lambda l, j: (l, 0, j)),
        out_shape=jax.ShapeDtypeStruct((depth, MOD_ROWS, n), F32),
        compiler_params=_cparams(2),
        name="adaln_mod",
    )(sc, mod_w, mod_b.reshape(depth, 1, n))


def _inproj_kernel(x_ref, g_ref, mod_ref, w_ref, wdt_ref,
                   z_ref, xbc_ref, hy_ref, ret_ref, dt_ref, dtt_ref, s5_ref):
    x = x_ref[0]
    xn = x * lax.rsqrt(jnp.mean(x * x, axis=-1, keepdims=True) + EPS) * g_ref[...]
    xm = (xn * (1.0 + mod_ref[0, 1:2, :]) + mod_ref[0, 0:1, :]).astype(BF16)
    o = 0
    for ref, width in ((z_ref, W_GRP), (xbc_ref, SSD_XBC), (hy_ref, HY_COLS),
                       (ret_ref, RET_COLS)):
        ref[0] = jnp.dot(xm, w_ref[:, o:o + width], preferred_element_type=F32)
        o += width
    s5_ref[...] = jnp.dot(xm, w_ref[:, o:o + W_GRP], preferred_element_type=F32)
    o += W_GRP
    dt_ref[0] = jnp.dot(xm, w_ref[:, o:o + 128], preferred_element_type=F32)
    dtt_ref[0] = lax.dot_general(wdt_ref[...], xm, (((1,), (1,)), ((), ())),
                                 preferred_element_type=F32)


def _inproj_call(h, g, mods, ctx_stream, wcat, wdtt):
    bsz, L, _ = h.shape
    tm = min(512, L)
    ncols = wcat.shape[1]
    mod_map = (lambda b, i: (bsz, 0, 0)) if ctx_stream else (lambda b, i: (b, 0, 0))
    tok = lambda w: pl.BlockSpec((1, tm, w), lambda b, i: (b, i, 0))
    out_shape = [jax.ShapeDtypeStruct((bsz, L, W_GRP), F32),
                 jax.ShapeDtypeStruct((bsz, L, SSD_XBC), F32),
                 jax.ShapeDtypeStruct((bsz, L, HY_COLS), F32),
                 jax.ShapeDtypeStruct((bsz, L, RET_COLS), F32),
                 jax.ShapeDtypeStruct((bsz, L, 128), F32),
                 jax.ShapeDtypeStruct((bsz, 8, L), F32),
                 jax.ShapeDtypeStruct((L, bsz * W_GRP), F32)]
    out_specs = [tok(W_GRP), tok(SSD_XBC), tok(HY_COLS), tok(RET_COLS), tok(128),
                 pl.BlockSpec((1, 8, tm), lambda b, i: (b, 0, i)),
                 pl.BlockSpec((tm, W_GRP), lambda b, i: (i, b))]
    return pl.pallas_call(
        _inproj_kernel,
        grid=(bsz, L // tm),
        in_specs=[tok(D_MODEL),
                  pl.BlockSpec((1, D_MODEL), lambda b, i: (0, 0)),
                  pl.BlockSpec((1, 6, D_MODEL), mod_map),
                  _const_spec((D_MODEL, ncols)),
                  pl.BlockSpec((8, D_MODEL), lambda b, i: (0, 0))],
        out_specs=out_specs,
        out_shape=out_shape,
        compiler_params=_cparams(2),
        name="in_proj",
    )(h, g, mods, wcat, wdtt)


def _halo_rows(ref, s, c, nc, L):
    sp = pl.multiple_of(jnp.maximum(s - 8, 0), 8)
    prev = ref[0, pl.ds(sp, 8), :][7:8, :]
    prev = jnp.where(c > 0, prev, 0.0)
    sn = pl.multiple_of(jnp.minimum(s + T, L - 8), 8)
    nxt = ref[0, pl.ds(sn, 8), :][0:1, :]
    nxt = jnp.where(c < nc - 1, nxt, 0.0)
    return prev, nxt


def _dwconv_chunk(x, prev, nxt, w_ref, b_ref):
    n = x.shape[0]
    row = lax.broadcasted_iota(jnp.int32, x.shape, 0)
    up = jnp.where(row == 0, prev, pltpu.roll(x, 1, 0))
    dn = jnp.where(row == n - 1, nxt, pltpu.roll(x, n - 1, 0))
    return up * w_ref[0:1, :] + x * w_ref[1:2, :] + dn * w_ref[2:3, :] + b_ref[...]


def _cumsum(x, axis):
    n = x.shape[axis]
    idx = lax.broadcasted_iota(jnp.int32, x.shape, axis)
    s = 1
    while s < n:
        x = x + jnp.where(idx >= s, pltpu.roll(x, s, axis), 0.0)
        s *= 2
    return x


def _expand_heads(c, exp_ref):
    hi = c.astype(BF16)
    lo = (c - hi.astype(F32)).astype(BF16)
    return jnp.dot(jnp.concatenate([hi, lo], axis=1), exp_ref[...], preferred_element_type=F32)


def _state_recurrence(hf_ref, hb_ref, decf_ref, decb_ref, nc, h0f, h0b):
    hf_ref[0] = h0f
    hb_ref[nc] = h0b

    def fwd(c, carry):
        hf_ref[c + 1] = decf_ref[c] * hf_ref[c] + hf_ref[c + 1]
        return carry

    lax.fori_loop(0, nc, fwd, 0)

    def bwd(k, carry):
        c = nc - 1 - k
        hb_ref[c] = decb_ref[c] * hb_ref[c + 1] + hb_ref[c]
        return carry

    lax.fori_loop(0, nc, bwd, 0)
    return hf_ref[nc], hb_ref[0]


def _ssd_sequence(L, z_ref, xbc_ref, dt_ref, dtt_ref, y_ref, prm, scr, h0f, h0b):
    (cw_ref, cb_ref, alog_row, alog_col, bias_row, bias_col, dskip_ref, ng_ref, exp_ref) = prm
    (xs_s, bc_s, ee_s, acs_s, acst_s, dtt_s, hf_s, hb_s, decf_s, decb_s) = scr
    nc = L // T
    lane = lax.broadcasted_iota(jnp.int32, (T, 128), 1)
    sub8 = lax.broadcasted_iota(jnp.int32, (8, T), 0)
    a_row = -jnp.exp(alog_row[...])
    a_col = -jnp.exp(alog_col[...])

    def phase1(c, carry):
        s = pl.multiple_of(c * T, T)
        rows = pl.ds(s, T)
        prev, nxt = _halo_rows(xbc_ref, s, c, nc, L)
        xact = _silu(_dwconv_chunk(xbc_ref[0, rows, :], prev, nxt, cw_ref, cb_ref))
        xs = xact[:, 0:W_GRP]
        bm = xact[:, W_GRP:W_GRP + SSD_STATE]
        xs_s[rows, :] = xs
        bc_s[rows, :] = xact[:, W_GRP:SSD_XBC]
        dt = jax.nn.softplus(dt_ref[0, rows, :] + bias_row[...])
        la = dt * a_row
        acs_f = _cumsum(la, 0)
        tot = acs_f[T - 1:T, :]
        acs = jnp.where(lane < N_HEADS, acs_f, tot - acs_f + la)
        acs_s[rows, :] = acs
        ee = _expand_heads(jnp.exp(acs), exp_ref)
        ee_s[rows, :] = ee
        wx = _expand_heads(dt * jnp.exp(tot - acs), exp_ref)
        hf_s[c + 1] = _dot_tn(bm, xs * wx[:, 0:W_GRP])
        hb_s[c] = _dot_tn(bm, xs * wx[:, W_GRP:2 * W_GRP])
        decf_s[c] = ee[T - 1:T, 0:W_GRP]
        decb_s[c] = ee[0:1, W_GRP:2 * W_GRP]
        dtt = jax.nn.softplus(dtt_ref[0, :, rows] + bias_col[...])
        lat = dtt * a_col
        acst_f = _cumsum(lat, 1)
        tott = acst_f[:, T - 1:T]
        acst_s[:, rows] = jnp.where(sub8 < N_HEADS, acst_f, tott - acst_f + lat)
        dtt_s[:, rows] = dtt
        return carry

    lax.fori_loop(0, nc, phase1, 0)
    hf_fin, hb_fin = _state_recurrence(hf_s, hb_s, decf_s, decb_s, nc, h0f, h0b)

    ri = lax.broadcasted_iota(jnp.int32, (T, T), 0)
    ci = lax.broadcasted_iota(jnp.int32, (T, T), 1)
    lower = ci <= ri
    upper = ci >= ri

    def phase3(c, carry):
        s = pl.multiple_of(c * T, T)
        rows = pl.ds(s, T)
        xs = xs_s[rows, :]
        bc = bc_s[rows, :]
        bm = bc[:, 0:SSD_STATE]
        cm = bc[:, SSD_STATE:2 * SSD_STATE]
        acs = acs_s[rows, :]
        acst = acst_s[:, rows]
        dtt = dtt_s[:, rows]
        g = _dot_nt(cm, bm)
        ys = []
        for h in range(N_HEADS):
            hb = N_HEADS + h
            df = jnp.exp(jnp.where(lower, acs[:, h:h + 1] - acst[h:h + 1, :], -jnp.inf))
            db = jnp.exp(jnp.where(upper, acs[:, hb:hb + 1] - acst[hb:hb + 1, :], -jnp.inf))
            wh = g * (df * dtt[h:h + 1, :] + db * dtt[hb:hb + 1, :])
            ys.append(_dot(wh, xs[:, HEAD_DIM * h:HEAD_DIM * (h + 1)]))
        y = jnp.concatenate(ys, axis=1)
        ee = ee_s[rows, :]
        y = y + ee[:, 0:W_GRP] * _dot(cm, hf_s[c]) + ee[:, W_GRP:2 * W_GRP] * _dot(cm, hb_s[c + 1])
        y = y + xs * dskip_ref[...]
        y = y * _silu(z_ref[0, rows, :])
        y = y * lax.rsqrt(jnp.mean(y * y, axis=-1, keepdims=True) + EPS) * ng_ref[...]
        y_ref[0, rows, :] = y
        return carry

    lax.fori_loop(0, nc, phase3, 0)
    return hf_fin, hb_fin


def _ssd_kernel(Lc, L, zc_ref, xbcc_ref, dtc_ref, dttc_ref, zl_ref, xbcl_ref, dtl_ref, dttl_ref,
                cw_ref, cb_ref, alog_row, alog_col, bias_row, bias_col, dskip_ref, ng_ref, exp_ref,
                yc_ref, yl_ref, *scr):
    prm = (cw_ref, cb_ref, alog_row, alog_col, bias_row, bias_col, dskip_ref, ng_ref, exp_ref)
    zero = jnp.zeros((SSD_STATE, W_GRP), F32)
    hf, hb = _ssd_sequence(Lc, zc_ref, xbcc_ref, dtc_ref, dttc_ref, yc_ref, prm, scr, zero, zero)
    _ssd_sequence(L, zl_ref, xbcl_ref, dtl_ref, dttl_ref, yl_ref, prm, scr, hf, hb)


def _head_expand_matrix():
    m = np.zeros((256, 512), np.float32)
    for r in range(2):
        for h in range(N_HEADS):
            for half in range(2):
                m[128 * half + r * N_HEADS + h, r * 256 + 64 * h:r * 256 + 64 * (h + 1)] = 1.0
    return jnp.asarray(m, BF16)


def _pad_row(v, n=128):
    v = v.reshape(1, -1)
    return jnp.pad(v, ((0, 0), (0, n - v.shape[1])))


def _ssd_call(uc, ul, conv_w, conv_b, a_log, dt_bias, d_skip, norm_g):
    zc, xbcc, dtc, dttc = uc
    zl, xbcl, dtl, dttl = ul
    bsz, Lc, _ = zc.shape
    L = zl.shape[1]
    nc = L // T
    alog_row = _pad_row(a_log)
    bias_row = _pad_row(dt_bias)
    alog_col = jnp.broadcast_to(a_log.reshape(8, 1), (8, 128))
    bias_col = jnp.broadcast_to(dt_bias.reshape(8, 1), (8, 128))
    dskip = jnp.repeat(d_skip, HEAD_DIM).reshape(1, W_GRP)

    def seq(Lx, w):
        return pl.BlockSpec((1, Lx, w), lambda b: (b, 0, 0))

    def small(shape):
        return pl.BlockSpec(shape, lambda b: (0,) * len(shape))

    in_specs = [seq(Lc, W_GRP), seq(Lc, SSD_XBC), seq(Lc, 128), pl.BlockSpec((1, 8, Lc), lambda b: (b, 0, 0)),
                seq(L, W_GRP), seq(L, SSD_XBC), seq(L, 128), pl.BlockSpec((1, 8, L), lambda b: (b, 0, 0)),
                small((3, SSD_XBC)), small((1, SSD_XBC)), small((1, 128)), small((8, 128)),
                small((1, 128)), small((8, 128)), small((1, W_GRP)), small((1, W_GRP)),
                small((256, 512))]
    scratch = [pltpu.VMEM((L, W_GRP), F32), pltpu.VMEM((L, 128), F32), pltpu.VMEM((L, 512), F32),
               pltpu.VMEM((L, 128), F32), pltpu.VMEM((8, L), F32), pltpu.VMEM((8, L), F32),
               pltpu.VMEM((nc + 1, SSD_STATE, W_GRP), F32), pltpu.VMEM((nc + 1, SSD_STATE, W_GRP), F32),
               pltpu.VMEM((nc, 1, W_GRP), F32), pltpu.VMEM((nc, 1, W_GRP), F32)]
    return pl.pallas_call(
        functools.partial(_ssd_kernel, Lc, L),
        grid=(bsz,),
        in_specs=in_specs,
        out_specs=[seq(Lc, W_GRP), seq(L, W_GRP)],
        out_shape=[jax.ShapeDtypeStruct((bsz, Lc, W_GRP), F32), jax.ShapeDtypeStruct((bsz, L, W_GRP), F32)],
        scratch_shapes=scratch,
        compiler_params=_cparams(1),
        name="ssd_mixer",
    )(zc, xbcc, dtc, dttc, zl, xbcl, dtl, dttl, conv_w, conv_b.reshape(1, -1),
      alog_row, alog_col, bias_row, bias_col, dskip, norm_g.reshape(1, -1), _head_expand_matrix())


def _ret_sequence(L, rope, q_ref, k_ref, v_ref, g_ref, y_ref, cos_ref, sin_ref, tabs, scr, h0f, h0b):
    (ef, eb, wf, wb, decf, decb, bdmask) = tabs
    (qr_s, kr_s, dm_s, hf_s, hb_s) = scr
    nc = L // T
    scale = HEAD_DIM ** -0.5
    lane = lax.broadcasted_iota(jnp.int32, (T, W_GRP), 1)
    first_half = (lane % 32) < 16

    def rot(x, rows):
        partner = jnp.where(first_half, pltpu.roll(x, W_GRP - 16, 1), pltpu.roll(x, 16, 1))
        return x * cos_ref[rows, :] + partner * sin_ref[rows, :]

    def phase1(c, carry):
        rows = pl.ds(pl.multiple_of(c * T, T), T)
        q = q_ref[0, rows, :]
        k = k_ref[0, rows, :]
        if rope:
            q = rot(q, rows)
            k = rot(k, rows)
        k = k * scale
        qr_s[rows, :] = q
        kr_s[rows, :] = k
        v = v_ref[0, rows, :]
        hf_s[c + 1] = _dot_tn(k * wf, v) * bdmask
        hb_s[c] = _dot_tn(k * wb, v) * bdmask
        return carry

    lax.fori_loop(0, nc, phase1, 0)

    hf_s[0] = h0f
    hb_s[nc] = h0b

    def fwd(c, carry):
        hf_s[c + 1] = decf * hf_s[c] + hf_s[c + 1]
        return carry

    lax.fori_loop(0, nc, fwd, 0)

    def bwd(kk, carry):
        c = nc - 1 - kk
        hb_s[c] = decb * hb_s[c + 1] + hb_s[c]
        return carry

    lax.fori_loop(0, nc, bwd, 0)

    def phase3(c, carry):
        rows = pl.ds(pl.multiple_of(c * T, T), T)
        q = qr_s[rows, :]
        k = kr_s[rows, :]
        v = v_ref[0, rows, :]
        gate = _silu(g_ref[0, rows, :])
        inter = _dot(q * ef, hf_s[c]) + _dot(q * eb, hb_s[c + 1])
        outs = []
        for h in range(N_HEADS):
            sl = slice(HEAD_DIM * h, HEAD_DIM * (h + 1))
            sc = _dot_nt(q[:, sl], k[:, sl]) * dm_s[h]
            yh = _dot(sc, v[:, sl]) + inter[:, sl]
            mu = jnp.mean(yh, axis=-1, keepdims=True)
            yc = yh - mu
            var = jnp.mean(yc * yc, axis=-1, keepdims=True)
            outs.append(yc * lax.rsqrt(var + EPS))
        y_ref[0, rows, :] = gate * jnp.concatenate(outs, axis=1)
        return carry

    lax.fori_loop(0, nc, phase3, 0)
    return hf_s[nc], hb_s[0]


def _ret_kernel(Lc, L, qc_ref, kc_ref, vc_ref, gc_ref, ql_ref, kl_ref, vl_ref, gl_ref,
                cos_ref, sin_ref, dec_ref, yc_ref, yl_ref, qr_s, kr_s, dm_s, hf_s, hb_s):
    lg = -jnp.exp(dec_ref[...])
    lgf = lg[0:1, :]
    lgb = lg[1:2, :]
    i = lax.broadcasted_iota(jnp.int32, (T, W_GRP), 0).astype(F32)
    ef = jnp.exp(lgf * (i + 1.0))
    eb = jnp.exp(lgb * (T - i))
    wf = jnp.exp(lgf * (T - 1.0 - i))
    wb = jnp.exp(lgb * i)
    decf = jnp.exp(lgf * float(T))
    decb = jnp.exp(lgb * float(T))
    r2 = lax.broadcasted_iota(jnp.int32, (W_GRP, W_GRP), 0) // HEAD_DIM
    c2 = lax.broadcasted_iota(jnp.int32, (W_GRP, W_GRP), 1) // HEAD_DIM
    bdmask = (r2 == c2).astype(F32)
    ri = lax.broadcasted_iota(jnp.int32, (T, T), 0)
    ci = lax.broadcasted_iota(jnp.int32, (T, T), 1)
    d = (ri - ci).astype(F32)
    for h in range(N_HEADS):
        lf = lgf[:, HEAD_DIM * h:HEAD_DIM * h + 1]
        lb = lgb[:, HEAD_DIM * h:HEAD_DIM * h + 1]
        dm_s[h] = (jnp.exp(jnp.where(ci <= ri, lf * d, -jnp.inf))
                   + jnp.exp(jnp.where(ci >= ri, -lb * d, -jnp.inf)))
    tabs = (ef, eb, wf, wb, decf, decb, bdmask)
    scr = (qr_s, kr_s, dm_s, hf_s, hb_s)
    zero = jnp.zeros((W_GRP, W_GRP), F32)
    hf, hb = _ret_sequence(Lc, False, qc_ref, kc_ref, vc_ref, gc_ref, yc_ref, cos_ref, sin_ref,
                           tabs, scr, zero, zero)
    _ret_sequence(L, True, ql_ref, kl_ref, vl_ref, gl_ref, yl_ref, cos_ref, sin_ref,
                  tabs, scr, hf, hb)


@functools.lru_cache(maxsize=None)
def _rope_tables(L):
    t = np.arange(L)
    f = 16
    inv = (ROPE_BASE ** (-np.arange(f, dtype=np.float32) / f)).astype(np.float32)
    cos = np.zeros((L, HEAD_DIM), np.float32)
    sin = np.zeros((L, HEAD_DIM), np.float32)
    for base, pos in ((0, t // GRID_W), (32, t % GRID_W)):
        ang = pos.astype(np.float32)[:, None] * inv[None, :]
        ang = ang.astype(np.float32).astype(np.float64)
        cos[:, base:base + f] = np.cos(ang)
        cos[:, base + f:base + 2 * f] = np.cos(ang)
        sin[:, base:base + f] = -np.sin(ang)
        sin[:, base + f:base + 2 * f] = np.sin(ang)
    return np.tile(cos, (1, N_HEADS)), np.tile(sin, (1, N_HEADS))


def _ret_call(uc, ul, decay_param):
    bsz, Lc, _ = uc.shape
    L = ul.shape[1]
    nc = L // T
    cos, sin = _rope_tables(L)
    dec = jnp.repeat(decay_param, HEAD_DIM, axis=1)

    def col(Lx, j):
        return pl.BlockSpec((1, Lx, W_GRP), lambda b, j=j: (b, 0, j))

    def seq(Lx):
        return pl.BlockSpec((1, Lx, W_GRP), lambda b: (b, 0, 0))

    in_specs = ([col(Lc, j) for j in range(4)] + [col(L, j) for j in range(4)]
                + [pl.BlockSpec((L, W_GRP), lambda b: (0, 0)), pl.BlockSpec((L, W_GRP), lambda b: (0, 0)),
                   pl.BlockSpec((2, W_GRP), lambda b: (0, 0))])
    scratch = [pltpu.VMEM((L, W_GRP), F32), pltpu.VMEM((L, W_GRP), F32),
               pltpu.VMEM((N_HEADS, T, T), F32),
               pltpu.VMEM((nc + 1, W_GRP, W_GRP), F32), pltpu.VMEM((nc + 1, W_GRP, W_GRP), F32)]
    return pl.pallas_call(
        functools.partial(_ret_kernel, Lc, L),
        grid=(bsz,),
        in_specs=in_specs,
        out_specs=[seq(Lc), seq(L)],
        out_shape=[jax.ShapeDtypeStruct((bsz, Lc, W_GRP), F32), jax.ShapeDtypeStruct((bsz, L, W_GRP), F32)],
        scratch_shapes=scratch,
        compiler_params=_cparams(1),
        name="retention_mixer",
    )(uc, uc, uc, uc, ul, ul, ul, ul, jnp.asarray(cos), jnp.asarray(sin), dec)


def _s5_prep_kernel(are_ref, aim_ref, ldt_ref, bre_ref, bim_ref, cre_ref, cim_ref,
                    bmat_ref, cmat_ref, ab_ref):
    a_re = are_ref[0]
    a_im = aim_ref[0]
    dt = jnp.exp(ldt_ref[0])
    mag = jnp.exp(a_re * dt)
    ab_re = mag * jnp.cos(a_im * dt)
    ab_im = mag * jnp.sin(a_im * dt)
    den = a_re * a_re + a_im * a_im
    z_re = ((ab_re - 1.0) * a_re + ab_im * a_im) / den
    z_im = (ab_im * a_re - (ab_re - 1.0) * a_im) / den
    b_re = bre_ref[...]
    b_im = bim_ref[...]
    bmat_ref[0, :, 0:S5_NS] = (b_re * z_re - b_im * z_im).astype(BF16)
    bmat_ref[0, :, S5_NS:2 * S5_NS] = (b_re * z_im + b_im * z_re).astype(BF16)
    cmat_ref[0, 0:S5_NS, :] = cre_ref[0].astype(BF16)
    cmat_ref[0, S5_NS:2 * S5_NS, :] = (-cim_ref[0]).astype(BF16)
    ab_ref[0, :, 0:S5_NS] = ab_re
    ab_ref[0, :, S5_NS:2 * S5_NS] = ab_im


def _s5_prep_call(a_re, a_im, log_dt, b_re, b_im, c_re, c_im):
    eye = jnp.eye(S5_GROUPS, dtype=F32)
    b_re_bd = jnp.einsum('gpc,gh->gchp', b_re, eye).reshape(W_GRP, S5_NS)
    b_im_bd = jnp.einsum('gpc,gh->gchp', b_im, eye).reshape(W_GRP, S5_NS)
    c_re_bd = jnp.einsum('rgcp,gh->rgphc', c_re, eye).reshape(2, S5_NS, W_GRP)
    c_im_bd = jnp.einsum('rgcp,gh->rgphc', c_im, eye).reshape(2, S5_NS, W_GRP)
    ldt = jnp.repeat(log_dt, S5_STATE, axis=1).reshape(2, 1, S5_NS)
    row = lambda: pl.BlockSpec((1, 1, S5_NS), lambda r: (r, 0, 0))
    return pl.pallas_call(
        _s5_prep_kernel,
        grid=(2,),
        in_specs=[row(), row(), row(),
                  pl.BlockSpec((W_GRP, S5_NS), lambda r: (0, 0)), pl.BlockSpec((W_GRP, S5_NS), lambda r: (0, 0)),
                  pl.BlockSpec((1, S5_NS, W_GRP), lambda r: (r, 0, 0)),
                  pl.BlockSpec((1, S5_NS, W_GRP), lambda r: (r, 0, 0))],
        out_specs=[pl.BlockSpec((1, W_GRP, 2 * S5_NS), lambda r: (r, 0, 0)),
                   pl.BlockSpec((1, 2 * S5_NS, W_GRP), lambda r: (r, 0, 0)),
                   pl.BlockSpec((1, 1, 2 * S5_NS), lambda r: (r, 0, 0))],
        out_shape=[jax.ShapeDtypeStruct((2, W_GRP, 2 * S5_NS), BF16),
                   jax.ShapeDtypeStruct((2, 2 * S5_NS, W_GRP), BF16),
                   jax.ShapeDtypeStruct((2, 1, 2 * S5_NS), F32)],
        compiler_params=_cparams(1),
        name="s5_prep",
    )(a_re.reshape(2, 1, S5_NS), a_im.reshape(2, 1, S5_NS), ldt, b_re_bd, b_im_bd, c_re_bd, c_im_bd)


def _s5_scan_kernel(bsz, uf_ref, ub_ref, h0_ref, bmat_ref, cmat_ref, ab_ref,
                    yf_ref, yb_ref, hs_ref, x_s):
    @pl.when(pl.program_id(0) == 0)
    def _():
        hs_ref[...] = h0_ref[...]

    nq = S5_NS // W_GRP
    for r, (u_ref, y_ref) in enumerate(((uf_ref, yf_ref), (ub_ref, yb_ref))):
        u = u_ref[...].reshape(S5_TS * bsz, W_GRP)
        x_s[...] = _dot(u, bmat_ref[r])
        for q in range(nq):
            cr = slice(W_GRP * q, W_GRP * (q + 1))
            cim = slice(S5_NS + W_GRP * q, S5_NS + W_GRP * (q + 1))
            a_re = jnp.broadcast_to(ab_ref[r, :, cr], (bsz, W_GRP))
            a_im = jnp.broadcast_to(ab_ref[r, :, cim], (bsz, W_GRP))

            def body(kk, carry, cr=cr, cim=cim, a_re=a_re, a_im=a_im, r=r):
                h_re, h_im = carry
                t = (S5_TS - 1 - kk) if r == 1 else kk
                rows = pl.ds(pl.multiple_of(t * bsz, bsz), bsz)
                n_re = a_re * h_re - a_im * h_im + x_s[rows, cr]
                n_im = a_re * h_im + a_im * h_re + x_s[rows, cim]
                x_s[rows, cr] = n_re
                x_s[rows, cim] = n_im
                return n_re, n_im

            h_re, h_im = lax.fori_loop(0, S5_TS, body, (hs_ref[r, :, cr], hs_ref[r, :, cim]), unroll=4)
            hs_ref[r, :, cr] = h_re
            hs_ref[r, :, cim] = h_im
        y_ref[...] = _dot(x_s[...], cmat_ref[r]).reshape(S5_TS, bsz, W_GRP)


def _s5_scan_call(u3, h0, bmat, cmat, ab):
    L, bsz, _ = u3.shape
    n = L // S5_TS
    blk = (S5_TS, bsz, W_GRP)
    return pl.pallas_call(
        functools.partial(_s5_scan_kernel, bsz),
        grid=(n,),
        in_specs=[pl.BlockSpec(blk, lambda i: (i, 0, 0)),
                  pl.BlockSpec(blk, lambda i: (n - 1 - i, 0, 0)),
                  pl.BlockSpec((2, bsz, 2 * S5_NS), lambda i: (0, 0, 0)),
                  pl.BlockSpec((2, W_GRP, 2 * S5_NS), lambda i: (0, 0, 0)),
                  pl.BlockSpec((2, 2 * S5_NS, W_GRP), lambda i: (0, 0, 0)),
                  pl.BlockSpec((2, 1, 2 * S5_NS), lambda i: (0, 0, 0))],
        out_specs=[pl.BlockSpec(blk, lambda i: (i, 0, 0)),
                   pl.BlockSpec(blk, lambda i: (n - 1 - i, 0, 0)),
                   pl.BlockSpec((2, bsz, 2 * S5_NS), lambda i: (0, 0, 0))],
        out_shape=[jax.ShapeDtypeStruct((L, bsz, W_GRP), F32),
                   jax.ShapeDtypeStruct((L, bsz, W_GRP), F32),
                   jax.ShapeDtypeStruct((2, bsz, 2 * S5_NS), F32)],
        scratch_shapes=[pltpu.VMEM((S5_TS * bsz, 2 * S5_NS), F32)],
        compiler_params=_cparams(1),
        name="s5_scan",
    )(u3, u3, h0, bmat, cmat, ab)


def _s5_post_kernel(u_ref, yf_ref, yb_ref, d_ref, w_ref, b_ref, o_ref):
    y = u_ref[...] * d_ref[...] + yf_ref[...] + yb_ref[...]
    y = jax.nn.gelu(y)
    o_ref[0] = y * jax.nn.sigmoid(jnp.dot(y.astype(BF16), w_ref[...], preferred_element_type=F32)
                                  + b_ref[...])


def _s5_post_call(u2, yf2, yb2, d, glu_w, glu_b, bsz):
    L = u2.shape[0]
    tm = min(512, L)
    tok = lambda: pl.BlockSpec((tm, W_GRP), lambda b, i: (i, b))
    return pl.pallas_call(
        _s5_post_kernel,
        grid=(bsz, L // tm),
        in_specs=[tok(), tok(), tok(),
                  pl.BlockSpec((1, W_GRP), lambda b, i: (0, 0)),
                  pl.BlockSpec((W_GRP, W_GRP), lambda b, i: (0, 0)),
                  pl.BlockSpec((1, W_GRP), lambda b, i: (0, 0))],
        out_specs=pl.BlockSpec((1, tm, W_GRP), lambda b, i: (b, i, 0)),
        out_shape=jax.ShapeDtypeStruct((bsz, L, W_GRP), F32),
        compiler_params=_cparams(2),
        name="s5_post",
    )(u2, yf2, yb2, d.reshape(1, -1), glu_w.astype(BF16), glu_b.reshape(1, -1))


def _s5_mixer(u5c, u5l, bsz, p):
    bmat, cmat, ab = _s5_prep_call(p['s5_a_re'], p['s5_a_im'], p['s5_log_dt'], p['s5_b_re'],
                                   p['s5_b_im'], p['s5_c_re'], p['s5_c_im'])
    Lc, L = u5c.shape[0], u5l.shape[0]
    h0 = jnp.zeros((2, bsz, 2 * S5_NS), F32)
    yfc, ybc, hc = _s5_scan_call(u5c.reshape(Lc, bsz, W_GRP), h0, bmat, cmat, ab)
    yfl, ybl, _ = _s5_scan_call(u5l.reshape(L, bsz, W_GRP), hc, bmat, cmat, ab)
    post = functools.partial(_s5_post_call, d=p['s5_d'], glu_w=p['s5_glu_w'], glu_b=p['s5_glu_b'], bsz=bsz)
    s5c = post(u5c, yfc.reshape(Lc, -1), ybc.reshape(Lc, -1))
    s5l = post(u5l, yfl.reshape(L, -1), ybl.reshape(L, -1))
    return s5c, s5l


@functools.lru_cache(maxsize=None)
def _dft_tables(L):
    k = np.arange(L, dtype=np.int64)
    ft = (k[:, None] * k[None, :]) % (2 * L)
    ang = ft.astype(np.float64) * (math.pi / L)
    return np.cos(ang).astype(np.float32), np.sin(ang).astype(np.float32)


@functools.lru_cache(maxsize=None)
def _hyena_consts(L):
    t = np.linspace(0.0, 1.0, L, dtype=np.float32)[:, None]
    w = (2.0 * math.pi * np.arange(L, dtype=np.float32)[:, None] / L).astype(np.float32)
    bands = np.linspace(1e-4, HY_BANDS - 1, HY_BANDS, dtype=np.float32)[None, :]
    bw = (bands * w).astype(np.float32).astype(np.float64)
    feats = np.zeros((L, 128), np.float32)
    feats[:, 0:1] = t
    feats[:, 1:1 + HY_BANDS] = np.cos(bw)
    feats[:, 1 + HY_BANDS:HY_EMB] = -np.sin(bw)
    max_decay = math.log(1e-2) / 0.3
    min_decay = math.log(1e-2) / 1.5
    deltas = np.abs(np.linspace(min_decay, max_decay, 4 * W_GRP, dtype=np.float32))[None, :]
    return feats, deltas.astype(np.float32)


HY_RB = 256


def _hy_filter_kernel(feats_ref, w1_ref, b1_ref, fr_ref, w2_ref, b2_ref, w3_ref, del_ref,
                      p_ref, q_ref, nrm_ref, an_ref):
    i = pl.program_id(0)
    feats = feats_ref[...]
    fr = fr_ref[...]
    h = jnp.sin(fr * (_dot_f32(feats, w1_ref[...]) + b1_ref[...]))
    h = jnp.sin(fr * (_dot_f32(h, w2_ref[...]) + b2_ref[...]))
    h = _dot_f32(h, w3_ref[...])
    h = h * jnp.exp(-feats[:, 0:1] * del_ref[...])
    half = 2 * W_GRP
    hf = h[:, 0:half]
    row = lax.broadcasted_iota(jnp.int32, (HY_RB, half), 0) + i * HY_RB
    hb = jnp.where(row == 0, 0.0, h[:, half:2 * half])
    p = hf + hb
    sign = (1 - 2 * (row % 2)).astype(F32)
    p_ref[...] = p.astype(BF16)
    q_ref[...] = (hb - hf).astype(BF16)

    @pl.when(i == 0)
    def _():
        nrm_ref[...] = jnp.full_like(nrm_ref, EPS)
        an_ref[...] = jnp.zeros_like(an_ref)

    nrm_ref[...] += (jnp.sum(jnp.abs(hf), axis=0, keepdims=True)
                     + jnp.sum(jnp.abs(hb), axis=0, keepdims=True))
    an_ref[...] += jnp.sum(p * sign, axis=0, keepdims=True)


def _hy_spectrum_kernel(L, c_ref, s_ref, p_ref, q_ref, nrm_ref, ans_ref, a_ref, bc_ref, an_ref):
    i = pl.program_id(0)
    n = 2.0 * L
    inv = 1.0 / nrm_ref[...]
    row = lax.broadcasted_iota(jnp.int32, a_ref.shape, 0) + i * HY_RB
    wv = jnp.where(row == 0, 1.0 / n, 2.0 / n) * inv
    a_ref[...] = wv * jnp.dot(c_ref[...], p_ref[...], preferred_element_type=F32)
    bc_ref[...] = wv * jnp.dot(s_ref[...], q_ref[...], preferred_element_type=F32)
    an_ref[...] = ans_ref[...] * inv / n


def _hy_filter_call(L, w1, b1, freq, w2, b2, w3, cmat, smat):
    feats, deltas = _hyena_consts(L)
    w1p = jnp.pad(w1, ((0, 128 - HY_EMB), (0, 0)))
    half = 2 * W_GRP
    nb = L // HY_RB
    full = lambda a: pl.BlockSpec(a.shape, lambda i, nd=a.ndim: (0,) * nd)
    small = (w1p, b1.reshape(1, -1), freq.reshape(1, -1), w2, b2.reshape(1, -1), w3, jnp.asarray(deltas))
    rowblk = lambda w: pl.BlockSpec((HY_RB, w), lambda i: (i, 0))
    vec = pl.BlockSpec((1, half), lambda i: (0, 0))
    p, q, nrm, ans = pl.pallas_call(
        _hy_filter_kernel,
        grid=(nb,),
        in_specs=[rowblk(128)] + [full(a) for a in small],
        out_specs=[rowblk(half), rowblk(half), vec, vec],
        out_shape=[jax.ShapeDtypeStruct((L, half), BF16), jax.ShapeDtypeStruct((L, half), BF16),
                   jax.ShapeDtypeStruct((1, half), F32), jax.ShapeDtypeStruct((1, half), F32)],
        compiler_params=_cparams(1),
        name="hyena_filter",
    )(jnp.asarray(feats), *small)
    return pl.pallas_call(
        functools.partial(_hy_spectrum_kernel, L),
        grid=(nb,),
        in_specs=[rowblk(L), rowblk(L), full(p), full(q), vec, vec],
        out_specs=[rowblk(half), rowblk(half), vec],
        out_shape=[jax.ShapeDtypeStruct((L, half), F32), jax.ShapeDtypeStruct((L, half), F32),
                   jax.ShapeDtypeStruct((1, half), F32)],
        compiler_params=_cparams(1),
        name="hyena_spectrum",
    )(cmat, smat, p, q, nrm, ans)


def _hy_conv_kernel(L, rb, u_ref, cw_ref, cb_ref, c_ref, s_ref, a_ref, bc_ref, an_ref, bias_ref,
                    o_ref, x1_s, x2_s, z_s, zb_s, xy_s, qn_s):
    nc = L // T
    nb = L // rb

    def conv(c, carry):
        s = pl.multiple_of(c * T, T)
        rows = pl.ds(s, T)
        prev, nxt = _halo_rows(u_ref, s, c, nc, L)
        y = _dwconv_chunk(u_ref[0, rows, :], prev, nxt, cw_ref, cb_ref)
        x1_s[rows, :] = y[:, 0:W_GRP]
        x2_s[rows, :] = y[:, W_GRP:2 * W_GRP]
        z_s[rows, :] = y[:, 2 * W_GRP:3 * W_GRP]
        return carry

    lax.fori_loop(0, nc, conv, 0)

    for o, gate_s in enumerate((x1_s, x2_s)):
        cols = slice(W_GRP * o, W_GRP * (o + 1))
        qn_s[...] = jnp.zeros_like(qn_s)

        def cast(j, carry):
            rows = pl.ds(pl.multiple_of(j * rb, rb), rb)
            z = z_s[rows, :]
            sign = (1 - 2 * (lax.broadcasted_iota(jnp.int32, z.shape, 0) % 2)).astype(F32)
            qn_s[...] += jnp.sum(z * sign, axis=0, keepdims=True)
            zb_s[rows, :] = z.astype(BF16)
            return carry

        lax.fori_loop(0, nb, cast, 0)

        def fwd(j, carry, cols=cols):
            rows = pl.ds(pl.multiple_of(j * rb, rb), rb)
            zb = zb_s[...]
            p = jnp.dot(c_ref[rows, :], zb, preferred_element_type=F32)
            q = jnp.dot(s_ref[rows, :], zb, preferred_element_type=F32)
            a = a_ref[rows, cols]
            bc = bc_ref[rows, cols]
            xy_s[rows, :] = (p * a + q * bc).astype(BF16)
            xy_s[pl.ds(pl.multiple_of(L + j * rb, rb), rb), :] = (q * a - p * bc).astype(BF16)
            return carry

        lax.fori_loop(0, nb, fwd, 0)
        nyq = qn_s[...] * an_ref[:, cols]

        def inv(j, carry, cols=cols, gate_s=gate_s, nyq=nyq, o=o):
            rows = pl.ds(pl.multiple_of(j * rb, rb), rb)
            y = (jnp.dot(c_ref[rows, :], xy_s[0:L, :], preferred_element_type=F32)
                 + jnp.dot(s_ref[rows, :], xy_s[L:2 * L, :], preferred_element_type=F32))
            sign = (1 - 2 * (lax.broadcasted_iota(jnp.int32, y.shape, 0) % 2)).astype(F32)
            z = z_s[rows, :]
            res = gate_s[rows, :] * (y + sign * nyq + z * bias_ref[:, cols])
            if o == 0:
                z_s[rows, :] = res
            else:
                o_ref[0, rows, :] = res
            return carry

        lax.fori_loop(0, nb, inv, 0)


def _hy_conv_call(u, conv_w, conv_b, cmat, smat, a, bc, an, bias):
    bsz, L, _ = u.shape
    rb = min(512, L)
    half = 2 * W_GRP
    return pl.pallas_call(
        functools.partial(_hy_conv_kernel, L, rb),
        grid=(bsz,),
        in_specs=[pl.BlockSpec((1, L, HY_COLS), lambda b: (b, 0, 0)),
                  pl.BlockSpec((3, HY_COLS), lambda b: (0, 0)),
                  pl.BlockSpec((1, HY_COLS), lambda b: (0, 0)),
                  _const_spec((L, L)), _const_spec((L, L)),
                  _const_spec((L, half)), _const_spec((L, half)),
                  pl.BlockSpec((1, half), lambda b: (0, 0)),
                  pl.BlockSpec((1, half), lambda b: (0, 0))],
        out_specs=pl.BlockSpec((1, L, W_GRP), lambda b: (b, 0, 0)),
        out_shape=jax.ShapeDtypeStruct((bsz, L, W_GRP), F32),
        scratch_shapes=[pltpu.VMEM((L, W_GRP), F32), pltpu.VMEM((L, W_GRP), F32),
                        pltpu.VMEM((L, W_GRP), F32), pltpu.VMEM((L, W_GRP), BF16),
                        pltpu.VMEM((2 * L, W_GRP), BF16), pltpu.VMEM((1, W_GRP), F32)],
        compiler_params=_cparams(1),
        name="hyena_conv",
    )(u, conv_w, conv_b.reshape(1, -1), cmat, smat, a, bc, an, bias.reshape(1, -1))


def _hyena_mixer(u, p):
    L = u.shape[1]
    cnp, snp = _dft_tables(L)
    cmat = jnp.asarray(cnp, BF16)
    smat = jnp.asarray(snp, BF16)
    a, bc, an = _hy_filter_call(L, p['hy_w1'], p['hy_b1'], p['hy_freq'], p['hy_w2'], p['hy_b2'],
                                p['hy_w3'], cmat, smat)
    return _hy_conv_call(u, p['hy_conv_w'], p['hy_conv_b'], cmat, smat, a, bc, an, p['hy_bias'])


def _out_ffn_kernel(final, h_ref, ssd_ref, hy_ref, ret_ref, s5_ref, mod_ref, g2_ref, wo_ref,
                    wup_ref, wdn_ref, fg_ref, o_ref, acc_s):
    y = jnp.zeros(h_ref.shape[1:], F32)
    for j, ref in enumerate((ssd_ref, hy_ref, ret_ref, s5_ref)):
        y = y + jnp.dot(ref[0].astype(BF16), wo_ref[W_GRP * j:W_GRP * (j + 1), :],
                        preferred_element_type=F32)
    h1 = h_ref[0] + mod_ref[0, 2:3, :] * y
    xn = h1 * lax.rsqrt(jnp.mean(h1 * h1, axis=-1, keepdims=True) + EPS) * g2_ref[...]
    xm = (xn * (1.0 + mod_ref[0, 4:5, :]) + mod_ref[0, 3:4, :]).astype(BF16)
    fc = 256
    for j in range(D_FF // fc):
        gg = jnp.dot(xm, wup_ref[:, fc * j:fc * (j + 1)], preferred_element_type=F32)
        uu = jnp.dot(xm, wup_ref[:, D_FF + fc * j:D_FF + fc * (j + 1)], preferred_element_type=F32)
        part = jnp.dot((_silu(gg) * uu).astype(BF16), wdn_ref[fc * j:fc * (j + 1), :],
                       preferred_element_type=F32)
        if j == 0:
            acc_s[...] = part
        else:
            acc_s[...] += part
    h2 = h1 + mod_ref[0, 5:6, :] * acc_s[...]
    if final:
        h2 = h2 * lax.rsqrt(jnp.mean(h2 * h2, axis=-1, keepdims=True) + EPS) * fg_ref[...]
    o_ref[0] = h2


def _out_ffn_call(h, mix, mods, ctx_stream, g2, wo, wup, wdn, final_g, final):
    bsz, L, _ = h.shape
    tm = min(256, L)
    mod_map = (lambda b, i: (bsz, 0, 0)) if ctx_stream else (lambda b, i: (b, 0, 0))
    tok = lambda w: pl.BlockSpec((1, tm, w), lambda b, i: (b, i, 0))
    return pl.pallas_call(
        functools.partial(_out_ffn_kernel, final),
        grid=(bsz, L // tm),
        in_specs=[tok(D_MODEL), tok(W_GRP), tok(W_GRP), tok(W_GRP), tok(W_GRP),
                  pl.BlockSpec((1, 6, D_MODEL), mod_map),
                  pl.BlockSpec((1, D_MODEL), lambda b, i: (0, 0)),
                  _const_spec((D_MODEL, D_MODEL)),
                  _const_spec((D_MODEL, 2 * D_FF)),
                  _const_spec((D_FF, D_MODEL)),
                  pl.BlockSpec((1, D_MODEL), lambda b, i: (0, 0))],
        out_specs=tok(D_MODEL),
        out_shape=jax.ShapeDtypeStruct((bsz, L, D_MODEL), F32),
        scratch_shapes=[pltpu.VMEM((tm, D_MODEL), F32)],
        compiler_params=_cparams(2),
        name="out_ffn",
    )(h, *mix, mods, g2, wo, wup, wdn, final_g)


def kernel(x, c, ctx, c_ctx, mod_w, mod_b, norm1_g, norm2_g, w_in, w_out, ssd_conv_w, ssd_conv_b, ssd_a_log, ssd_dt_bias, ssd_d, ssd_norm_g, hy_conv_w, hy_conv_b, hy_w1, hy_b1, hy_freq, hy_w2, hy_b2, hy_w3, hy_bias, ret_decay, s5_a_re, s5_a_im, s5_log_dt, s5_b_re, s5_b_im, s5_c_re, s5_c_im, s5_d, s5_glu_w, s5_glu_b, ffn_w_up, ffn_w_down, final_norm_g):
    bsz = x.shape[0]
    depth = mod_w.shape[0]
    sc = jnp.concatenate([c, c_ctx[None, :], jnp.zeros((MOD_ROWS - bsz - 1, D_MODEL), F32)], axis=0)
    mods_all = _mod_call(sc, mod_w, mod_b).reshape(depth, MOD_ROWS, 6, D_MODEL)
    fg = final_norm_g.reshape(1, -1)
    h_l, h_c = x, ctx
    o_xbc, o_dt, o_hy = W_GRP, W_GRP + SSD_XBC, W_GRP + SSD_XBC + 2 * N_HEADS
    o_ret = o_hy + HY_COLS
    o_s5 = o_ret + RET_COLS
    for i in range(depth):
        last = i == depth - 1
        wi = w_in[i]
        wdt = wi[:, o_dt:o_hy]
        wcat = jnp.concatenate([wi[:, 0:o_dt], wi[:, o_hy:o_s5 + W_GRP], wdt,
                                jnp.zeros((D_MODEL, 128 - 2 * N_HEADS), F32)], axis=1).astype(BF16)
        wdtt = wdt.T.astype(BF16)
        mods = mods_all[i]
        g1 = norm1_g[i].reshape(1, -1)
        zl, xbcl, hyl, retl, dtl, dttl, s5l = _inproj_call(h_l, g1, mods, False, wcat, wdtt)
        zc, xbcc, hyc, retc, dtc, dttc, s5c = _inproj_call(h_c, g1, mods, True, wcat, wdtt)
        ssd_c, ssd_l = _ssd_call((zc, xbcc, dtc, dttc), (zl, xbcl, dtl, dttl), ssd_conv_w[i],
                                 ssd_conv_b[i], ssd_a_log[i], ssd_dt_bias[i], ssd_d[i], ssd_norm_g[i])
        ret_c, ret_l = _ret_call(retc, retl, ret_decay[i])
        p = dict(s5_a_re=s5_a_re[i], s5_a_im=s5_a_im[i], s5_log_dt=s5_log_dt[i], s5_b_re=s5_b_re[i],
                 s5_b_im=s5_b_im[i], s5_c_re=s5_c_re[i], s5_c_im=s5_c_im[i], s5_d=s5_d[i],
                 s5_glu_w=s5_glu_w[i], s5_glu_b=s5_glu_b[i],
                 hy_conv_w=hy_conv_w[i], hy_conv_b=hy_conv_b[i], hy_w1=hy_w1[i], hy_b1=hy_b1[i],
                 hy_freq=hy_freq[i], hy_w2=hy_w2[i], hy_b2=hy_b2[i], hy_w3=hy_w3[i], hy_bias=hy_bias[i])
        s5_c, s5_l = _s5_mixer(s5c, s5l, bsz, p)
        hy_l = _hyena_mixer(hyl, p)
        g2 = norm2_g[i].reshape(1, -1)
        wo = w_out[i].astype(BF16)
        wup = ffn_w_up[i].astype(BF16)
        wdn = ffn_w_down[i].astype(BF16)
        h_l = _out_ffn_call(h_l, (ssd_l, hy_l, ret_l, s5_l), mods, False, g2, wo, wup, wdn, fg, last)
        if not last:
            hy_c = _hyena_mixer(hyc, p)
            h_c = _out_ffn_call(h_c, (ssd_c, hy_c, ret_c, s5_c), mods, True, g2, wo, wup, wdn, fg, False)
    return h_l
```

```python
import functools
import math

import numpy as np
import jax
import jax.numpy as jnp
from jax import lax
from jax.experimental import pallas as pl
from jax.experimental.pallas import tpu as pltpu

F32 = jnp.float32
BF16 = jnp.bfloat16
EPS = 1e-6

D_MODEL = 1024
W_GRP = 256
T = 128
N_HEADS = 4
HEAD_DIM = 64
SSD_STATE = 64
SSD_XBC = W_GRP + 2 * SSD_STATE
HY_COLS = 3 * W_GRP
RET_COLS = 4 * W_GRP
GRID_W = 64
ROPE_BASE = 10000.0
HY_EMB = 33
HY_BANDS = 16
HY_FILT = 64
S5_GROUPS = 16
S5_CH = 16
S5_STATE = 64
S5_NS = S5_GROUPS * S5_STATE
D_FF = 2816
S5_TS = 64
MOD_ROWS = 24

VMEM_LIMIT = 56 * 1024 * 1024


def _cparams(n_grid):
    return pltpu.CompilerParams(dimension_semantics=("arbitrary",) * n_grid,
                                vmem_limit_bytes=VMEM_LIMIT)


def _dot(a, b):
    return jnp.dot(a.astype(BF16), b.astype(BF16), preferred_element_type=F32)


def _dot_nt(a, b):
    return lax.dot_general(a.astype(BF16), b.astype(BF16), (((1,), (1,)), ((), ())),
                           preferred_element_type=F32)


def _dot_tn(a, b):
    return lax.dot_general(a.astype(BF16), b.astype(BF16), (((0,), (0,)), ((), ())),
                           preferred_element_type=F32)


def _dot_f32(a, b):
    return jnp.dot(a, b, preferred_element_type=F32, precision=lax.Precision.HIGHEST)


def _silu(x):
    return x * jax.nn.sigmoid(x)


def _const_spec(shape):
    nd = len(shape)
    return pl.BlockSpec(shape, lambda *_: (0,) * nd, pipeline_mode=pl.Buffered(1))


def _mod_kernel(sc_ref, w_ref, b_ref, o_ref):
    s = _silu(sc_ref[...])
    o_ref[0] = _dot_f32(s, w_ref[0]) + b_ref[0]


def _mod_call(sc, mod_w, mod_b):
    depth, _, n = mod_w.shape
    tn = 1536
    return pl.pallas_call(
        _mod_kernel,
        grid=(depth, n // tn),
        in_specs=[pl.BlockSpec((MOD_ROWS, D_MODEL), lambda l, j: (0, 0)),
                  pl.BlockSpec((1, D_MODEL, tn), lambda l, j: (l, 0, j)),
                  pl.BlockSpec((1, 1, tn), lambda l, j: (l, 0, j))],
        out_specs=pl.BlockSpec((1, MOD_ROWS, tn), lambda l, j: (l, 0, j)),
        out_shape=jax.ShapeDtypeStruct((depth, MOD_ROWS, n), F32),
        compiler_params=_cparams(2),
        name="adaln_mod",
    )(sc, mod_w, mod_b.reshape(depth, 1, n))


def _inproj_kernel(x_ref, g_ref, mod_ref, w_ref, z_ref, xbc_ref, hy_ref, ret_ref, dt_ref, s5_ref):
    x = x_ref[0]
    xn = x * lax.rsqrt(jnp.mean(x * x, axis=-1, keepdims=True) + EPS) * g_ref[...]
    xm = (xn * (1.0 + mod_ref[0, 1:2, :]) + mod_ref[0, 0:1, :]).astype(BF16)
    o = 0
    for ref, width in ((z_ref, W_GRP), (xbc_ref, SSD_XBC), (hy_ref, HY_COLS),
                       (ret_ref, RET_COLS)):
        ref[0] = jnp.dot(xm, w_ref[:, o:o + width], preferred_element_type=F32)
        o += width
    s5_ref[...] = jnp.dot(xm, w_ref[:, o:o + W_GRP], preferred_element_type=F32)
    o += W_GRP
    dt_ref[0] = jnp.dot(xm, w_ref[:, o:o + 128], preferred_element_type=F32)


def _inproj_call(h, g, mods, ctx_stream, wcat):
    bsz, L, _ = h.shape
    tm = min(512, L)
    ncols = wcat.shape[1]
    mod_map = (lambda b, i: (bsz, 0, 0)) if ctx_stream else (lambda b, i: (b, 0, 0))
    tok = lambda w: pl.BlockSpec((1, tm, w), lambda b, i: (b, i, 0))
    out_shape = [jax.ShapeDtypeStruct((bsz, L, W_GRP), F32),
                 jax.ShapeDtypeStruct((bsz, L, SSD_XBC), F32),
                 jax.ShapeDtypeStruct((bsz, L, HY_COLS), F32),
                 jax.ShapeDtypeStruct((bsz, L, RET_COLS), F32),
                 jax.ShapeDtypeStruct((bsz, L, 128), F32),
                 jax.ShapeDtypeStruct((L, bsz * W_GRP), F32)]
    out_specs = [tok(W_GRP), tok(SSD_XBC), tok(HY_COLS), tok(RET_COLS), tok(128),
                 pl.BlockSpec((tm, W_GRP), lambda b, i: (i, b))]
    return pl.pallas_call(
        _inproj_kernel,
        grid=(bsz, L // tm),
        in_specs=[tok(D_MODEL),
                  pl.BlockSpec((1, D_MODEL), lambda b, i: (0, 0)),
                  pl.BlockSpec((1, 6, D_MODEL), mod_map),
                  _const_spec((D_MODEL, ncols))],
        out_specs=out_specs,
        out_shape=out_shape,
        compiler_params=_cparams(2),
        name="in_proj",
    )(h, g, mods, wcat)


def _halo_rows(ref, s, c, nc, L):
    sp = pl.multiple_of(jnp.maximum(s - 8, 0), 8)
    prev = ref[0, pl.ds(sp, 8), :][7:8, :]
    prev = jnp.where(c > 0, prev, 0.0)
    sn = pl.multiple_of(jnp.minimum(s + T, L - 8), 8)
    nxt = ref[0, pl.ds(sn, 8), :][0:1, :]
    nxt = jnp.where(c < nc - 1, nxt, 0.0)
    return prev, nxt


def _dwconv_chunk(x, prev, nxt, w_ref, b_ref):
    n = x.shape[0]
    row = lax.broadcasted_iota(jnp.int32, x.shape, 0)
    up = jnp.where(row == 0, prev, pltpu.roll(x, 1, 0))
    dn = jnp.where(row == n - 1, nxt, pltpu.roll(x, n - 1, 0))
    return up * w_ref[0:1, :] + x * w_ref[1:2, :] + dn * w_ref[2:3, :] + b_ref[...]


def _cumsum(x, axis):
    n = x.shape[axis]
    idx = lax.broadcasted_iota(jnp.int32, x.shape, axis)
    s = 1
    while s < n:
        x = x + jnp.where(idx >= s, pltpu.roll(x, s, axis), 0.0)
        s *= 2
    return x


def _expand_heads(c, exp_ref):
    hi = c.astype(BF16)
    lo = (c - hi.astype(F32)).astype(BF16)
    return jnp.dot(jnp.concatenate([hi, lo], axis=1), exp_ref[...], preferred_element_type=F32)


def _state_recurrence(hf_ref, hb_ref, decf_ref, decb_ref, nc, h0f, h0b):
    hf_ref[0] = h0f
    hb_ref[nc] = h0b

    def fwd(c, carry):
        hf_ref[c + 1] = decf_ref[c] * hf_ref[c] + hf_ref[c + 1]
        return carry

    lax.fori_loop(0, nc, fwd, 0)

    def bwd(k, carry):
        c = nc - 1 - k
        hb_ref[c] = decb_ref[c] * hb_ref[c + 1] + hb_ref[c]
        return carry

    lax.fori_loop(0, nc, bwd, 0)
    return hf_ref[nc], hb_ref[0]


def _split3(x):
    hi = x.astype(BF16)
    r = x - hi.astype(F32)
    mid = r.astype(BF16)
    lo = (r - mid.astype(F32)).astype(BF16)
    return jnp.concatenate([hi, mid, lo], axis=1)


def _tile_heads_bd(x, bdmask):
    xb = x.astype(BF16)
    return jnp.where(bdmask, jnp.concatenate([xb] * N_HEADS, axis=0), 0)


def _rows_to_lanes(a, lo):
    return jnp.concatenate([a[lo + h:lo + h + 1, :] for h in range(N_HEADS)], axis=1)


def _ssd_sequence(L, z_ref, xbc_ref, dt_ref, y_ref, prm, scr, h0f, h0b):
    (cw_ref, cb_ref, alog_row, bias_row, dskip_ref, ng_ref, exp_ref, exp128_ref, sel_ref) = prm
    (xs_s, bc_s, ee_s, acs_s, dt_s, hf_s, hb_s, decf_s, decb_s) = scr
    nc = L // T
    lane = lax.broadcasted_iota(jnp.int32, (T, 128), 1)
    a_row = -jnp.exp(alog_row[...])

    def phase1(c, carry):
        s = pl.multiple_of(c * T, T)
        rows = pl.ds(s, T)
        prev, nxt = _halo_rows(xbc_ref, s, c, nc, L)
        xact = _silu(_dwconv_chunk(xbc_ref[0, rows, :], prev, nxt, cw_ref, cb_ref))
        xs = xact[:, 0:W_GRP]
        bm = xact[:, W_GRP:W_GRP + SSD_STATE]
        xs_s[rows, :] = xs
        bc_s[rows, :] = xact[:, W_GRP:SSD_XBC]
        dt = jax.nn.softplus(dt_ref[0, rows, :] + bias_row[...])
        dt_s[rows, :] = dt
        la = dt * a_row
        acs_f = _cumsum(la, 0)
        tot = acs_f[T - 1:T, :]
        acs = jnp.where(lane < N_HEADS, acs_f, tot - acs_f + la)
        acs_s[rows, :] = acs
        ee = _expand_heads(jnp.exp(acs), exp_ref)
        ee_s[rows, :] = ee
        wx = _expand_heads(dt * jnp.exp(tot - acs), exp_ref)
        hf_s[c + 1] = _dot_tn(bm, xs * wx[:, 0:W_GRP])
        hb_s[c] = _dot_tn(bm, xs * wx[:, W_GRP:2 * W_GRP])
        decf_s[c] = ee[T - 1:T, 0:W_GRP]
        decb_s[c] = ee[0:1, W_GRP:2 * W_GRP]
        return carry

    lax.fori_loop(0, nc, phase1, 0, unroll=2)
    hf_fin, hb_fin = _state_recurrence(hf_s, hb_s, decf_s, decb_s, nc, h0f, h0b)

    ri = lax.broadcasted_iota(jnp.int32, (T, N_HEADS * T), 0)
    ci = lax.broadcasted_iota(jnp.int32, (T, N_HEADS * T), 1) % T
    lower = ci <= ri
    upper = ci >= ri
    r4 = lax.broadcasted_iota(jnp.int32, (N_HEADS * T, W_GRP), 0) // T
    c4 = lax.broadcasted_iota(jnp.int32, (N_HEADS * T, W_GRP), 1) // HEAD_DIM
    bdmask = r4 == c4
    nl = N_HEADS * T

    def phase3(c, carry):
        s = pl.multiple_of(c * T, T)
        rows = pl.ds(s, T)
        xs = xs_s[rows, :]
        bc = bc_s[rows, :]
        bm = bc[:, 0:SSD_STATE]
        cm = bc[:, SSD_STATE:2 * SSD_STATE]
        a3 = _split3(acs_s[rows, :])
        col = jnp.dot(a3, exp128_ref[...], preferred_element_type=F32)
        acst = lax.dot_general(sel_ref[...], a3, (((1,), (1,)), ((), ())), preferred_element_type=F32)
        dtt = lax.dot_general(sel_ref[...], _split3(dt_s[rows, :]), (((1,), (1,)), ((), ())),
                              preferred_element_type=F32)
        g = _dot_nt(cm, bm)
        g4 = jnp.concatenate([g] * N_HEADS, axis=1)
        df = jnp.exp(jnp.where(lower, col[:, 0:nl] - _rows_to_lanes(acst, 0), -jnp.inf))
        db = jnp.exp(jnp.where(upper, col[:, nl:2 * nl] - _rows_to_lanes(acst, N_HEADS), -jnp.inf))
        w = g4 * (df * _rows_to_lanes(dtt, 0) + db * _rows_to_lanes(dtt, N_HEADS))
        y = jnp.dot(w.astype(BF16), _tile_heads_bd(xs, bdmask), preferred_element_type=F32)
        ee = ee_s[rows, :]
        y = y + ee[:, 0:W_GRP] * _dot(cm, hf_s[c]) + ee[:, W_GRP:2 * W_GRP] * _dot(cm, hb_s[c + 1])
        y = y + xs * dskip_ref[...]
        y = y * _silu(z_ref[0, rows, :])
        y = y * lax.rsqrt(jnp.mean(y * y, axis=-1, keepdims=True) + EPS) * ng_ref[...]
        y_ref[0, rows, :] = y
        return carry

    lax.fori_loop(0, nc, phase3, 0, unroll=2)
    return hf_fin, hb_fin


def _ssd_kernel(Lc, L, zc_ref, xbcc_ref, dtc_ref, zl_ref, xbcl_ref, dtl_ref,
                cw_ref, cb_ref, alog_row, bias_row, dskip_ref, ng_ref, exp_ref, exp128_ref, sel_ref,
                yc_ref, yl_ref, *scr):
    prm = (cw_ref, cb_ref, alog_row, bias_row, dskip_ref, ng_ref, exp_ref, exp128_ref, sel_ref)
    zero = jnp.zeros((SSD_STATE, W_GRP), F32)
    hf, hb = _ssd_sequence(Lc, zc_ref, xbcc_ref, dtc_ref, yc_ref, prm, scr, zero, zero)
    _ssd_sequence(L, zl_ref, xbcl_ref, dtl_ref, yl_ref, prm, scr, hf, hb)


@functools.lru_cache(maxsize=None)
def _ssd_tables():
    exp64 = np.zeros((256, 512), np.float32)
    exp128 = np.zeros((384, 2 * N_HEADS * T), np.float32)
    sel = np.zeros((8, 384), np.float32)
    for r in range(2):
        for h in range(N_HEADS):
            m = r * N_HEADS + h
            for part in range(2):
                exp64[128 * part + m, r * 256 + 64 * h:r * 256 + 64 * (h + 1)] = 1.0
            for part in range(3):
                exp128[128 * part + m, (r * N_HEADS + h) * T:(r * N_HEADS + h + 1) * T] = 1.0
                sel[m, 128 * part + m] = 1.0
    return exp64, exp128, sel


def _pad_row(v, n=128):
    v = v.reshape(1, -1)
    return jnp.pad(v, ((0, 0), (0, n - v.shape[1])))


def _ssd_call(uc, ul, conv_w, conv_b, a_log, dt_bias, d_skip, norm_g):
    zc, xbcc, dtc = uc
    zl, xbcl, dtl = ul
    bsz, Lc, _ = zc.shape
    L = zl.shape[1]
    nc = L // T
    alog_row = _pad_row(a_log)
    bias_row = _pad_row(dt_bias)
    dskip = jnp.repeat(d_skip, HEAD_DIM).reshape(1, W_GRP)
    exp64, exp128, sel = (jnp.asarray(t).astype(BF16) for t in _ssd_tables())

    def seq(Lx, w):
        return pl.BlockSpec((1, Lx, w), lambda b: (b, 0, 0))

    def small(shape):
        return pl.BlockSpec(shape, lambda b: (0,) * len(shape))

    in_specs = [seq(Lc, W_GRP), seq(Lc, SSD_XBC), seq(Lc, 128),
                seq(L, W_GRP), seq(L, SSD_XBC), seq(L, 128),
                small((3, SSD_XBC)), small((1, SSD_XBC)), small((1, 128)),
                small((1, 128)), small((1, W_GRP)), small((1, W_GRP)),
                small(exp64.shape), small(exp128.shape), small(sel.shape)]
    scratch = [pltpu.VMEM((L, W_GRP), F32), pltpu.VMEM((L, 128), F32), pltpu.VMEM((L, 512), F32),
               pltpu.VMEM((L, 128), F32), pltpu.VMEM((L, 128), F32),
               pltpu.VMEM((nc + 1, SSD_STATE, W_GRP), F32), pltpu.VMEM((nc + 1, SSD_STATE, W_GRP), F32),
               pltpu.VMEM((nc, 1, W_GRP), F32), pltpu.VMEM((nc, 1, W_GRP), F32)]
    return pl.pallas_call(
        functools.partial(_ssd_kernel, Lc, L),
        grid=(bsz,),
        in_specs=in_specs,
        out_specs=[seq(Lc, W_GRP), seq(L, W_GRP)],
        out_shape=[jax.ShapeDtypeStruct((bsz, Lc, W_GRP), F32), jax.ShapeDtypeStruct((bsz, L, W_GRP), F32)],
        scratch_shapes=scratch,
        compiler_params=_cparams(1),
        name="ssd_mixer",
    )(zc, xbcc, dtc, zl, xbcl, dtl, conv_w, conv_b.reshape(1, -1),
      alog_row, bias_row, dskip, norm_g.reshape(1, -1), exp64, exp128, sel)


def _group_norm_heads(y, avg_ref):
    def gmean(x):
        hi = x.astype(BF16)
        lo = (x - hi.astype(F32)).astype(BF16)
        return jnp.dot(jnp.concatenate([hi, lo], axis=1), avg_ref[...], preferred_element_type=F32)

    yc = y - gmean(y)
    return yc * lax.rsqrt(gmean(yc * yc) + EPS)


def _ret_sequence(L, rope, q_ref, k_ref, v_ref, g_ref, y_ref, cos_ref, sin_ref, perm_ref, avg_ref,
                  tabs, scr, h0f, h0b):
    (ef, eb, wf, wb, decf, decb, bdmask) = tabs
    (qr_s, kr_s, dm_s, hf_s, hb_s) = scr
    nc = L // T
    scale = HEAD_DIM ** -0.5
    r4 = lax.broadcasted_iota(jnp.int32, (N_HEADS * T, W_GRP), 0) // T
    c4 = lax.broadcasted_iota(jnp.int32, (N_HEADS * T, W_GRP), 1) // HEAD_DIM
    stackmask = r4 == c4

    def rot(x, rows):
        hi = x.astype(BF16)
        lo = (x - hi.astype(F32)).astype(BF16)
        partner = jnp.dot(jnp.concatenate([hi, lo], axis=1), perm_ref[...], preferred_element_type=F32)
        return x * cos_ref[rows, :] + partner * sin_ref[rows, :]

    def phase1(c, carry):
        rows = pl.ds(pl.multiple_of(c * T, T), T)
        q = q_ref[0, rows, :]
        k = k_ref[0, rows, :]
        if rope:
            q = rot(q, rows)
            k = rot(k, rows)
        k = k * scale
        qr_s[rows, :] = q
        kr_s[rows, :] = k
        v = v_ref[0, rows, :]
        hf_s[c + 1] = _dot_tn(k * wf, v) * bdmask
        hb_s[c] = _dot_tn(k * wb, v) * bdmask
        return carry

    lax.fori_loop(0, nc, phase1, 0, unroll=2)

    hf_s[0] = h0f
    hb_s[nc] = h0b

    def fwd(c, carry):
        hf_s[c + 1] = decf * hf_s[c] + hf_s[c + 1]
        return carry

    lax.fori_loop(0, nc, fwd, 0)

    def bwd(kk, carry):
        c = nc - 1 - kk
        hb_s[c] = decb * hb_s[c + 1] + hb_s[c]
        return carry

    lax.fori_loop(0, nc, bwd, 0)

    def phase3(c, carry):
        rows = pl.ds(pl.multiple_of(c * T, T), T)
        q = qr_s[rows, :]
        v = v_ref[0, rows, :]
        inter = _dot(q * ef, hf_s[c]) + _dot(q * eb, hb_s[c + 1])
        sc = lax.dot_general(q.astype(BF16), _tile_heads_bd(kr_s[rows, :], stackmask),
                             (((1,), (1,)), ((), ())), preferred_element_type=F32)
        y = jnp.dot((sc * dm_s[...]).astype(BF16), _tile_heads_bd(v, stackmask),
                    preferred_element_type=F32) + inter
        y_ref[0, rows, :] = _silu(g_ref[0, rows, :]) * _group_norm_heads(y, avg_ref)
        return carry

    lax.fori_loop(0, nc, phase3, 0, unroll=2)
    return hf_s[nc], hb_s[0]


def _ret_kernel(Lc, L, qc_ref, kc_ref, vc_ref, gc_ref, ql_ref, kl_ref, vl_ref, gl_ref,
                cos_ref, sin_ref, dec_ref, perm_ref, avg_ref, yc_ref, yl_ref,
                qr_s, kr_s, dm_s, hf_s, hb_s):
    lg = -jnp.exp(dec_ref[...])
    lgf = lg[0:1, :]
    lgb = lg[1:2, :]
    i = lax.broadcasted_iota(jnp.int32, (T, W_GRP), 0).astype(F32)
    ef = jnp.exp(lgf * (i + 1.0))
    eb = jnp.exp(lgb * (T - i))
    wf = jnp.exp(lgf * (T - 1.0 - i))
    wb = jnp.exp(lgb * i)
    decf = jnp.exp(lgf * float(T))
    decb = jnp.exp(lgb * float(T))
    r2 = lax.broadcasted_iota(jnp.int32, (W_GRP, W_GRP), 0) // HEAD_DIM
    c2 = lax.broadcasted_iota(jnp.int32, (W_GRP, W_GRP), 1) // HEAD_DIM
    bdmask = (r2 == c2).astype(F32)
    ri = lax.broadcasted_iota(jnp.int32, (T, T), 0)
    ci = lax.broadcasted_iota(jnp.int32, (T, T), 1)
    d = (ri - ci).astype(F32)
    for h in range(N_HEADS):
        lf = lgf[:, HEAD_DIM * h:HEAD_DIM * h + 1]
        lb = lgb[:, HEAD_DIM * h:HEAD_DIM * h + 1]
        dm_s[:, T * h:T * (h + 1)] = (jnp.exp(jnp.where(ci <= ri, lf * d, -jnp.inf))
                                      + jnp.exp(jnp.where(ci >= ri, -lb * d, -jnp.inf)))
    tabs = (ef, eb, wf, wb, decf, decb, bdmask)
    scr = (qr_s, kr_s, dm_s, hf_s, hb_s)
    zero = jnp.zeros((W_GRP, W_GRP), F32)
    hf, hb = _ret_sequence(Lc, False, qc_ref, kc_ref, vc_ref, gc_ref, yc_ref, cos_ref, sin_ref,
                           perm_ref, avg_ref, tabs, scr, zero, zero)
    _ret_sequence(L, True, ql_ref, kl_ref, vl_ref, gl_ref, yl_ref, cos_ref, sin_ref,
                  perm_ref, avg_ref, tabs, scr, hf, hb)


@functools.lru_cache(maxsize=None)
def _rope_tables(L):
    t = np.arange(L)
    f = 16
    inv = (ROPE_BASE ** (-np.arange(f, dtype=np.float32) / f)).astype(np.float32)
    cos = np.zeros((L, HEAD_DIM), np.float32)
    sin = np.zeros((L, HEAD_DIM), np.float32)
    for base, pos in ((0, t // GRID_W), (32, t % GRID_W)):
        ang = pos.astype(np.float32)[:, None] * inv[None, :]
        ang = ang.astype(np.float32).astype(np.float64)
        cos[:, base:base + f] = np.cos(ang)
        cos[:, base + f:base + 2 * f] = np.cos(ang)
        sin[:, base:base + f] = -np.sin(ang)
        sin[:, base + f:base + 2 * f] = np.sin(ang)
    return np.tile(cos, (1, N_HEADS)), np.tile(sin, (1, N_HEADS))


@functools.lru_cache(maxsize=None)
def _ret_tables():
    perm = np.zeros((2 * W_GRP, W_GRP), np.float32)
    avg = np.zeros((2 * W_GRP, W_GRP), np.float32)
    for part in range(2):
        for l in range(W_GRP):
            src = l + 16 if (l % 32) < 16 else l - 16
            perm[part * W_GRP + src, l] = 1.0
            g = l // HEAD_DIM
            avg[part * W_GRP + g * HEAD_DIM:part * W_GRP + (g + 1) * HEAD_DIM, l] = 1.0 / HEAD_DIM
    return perm, avg


def _ret_call(uc, ul, decay_param):
    bsz, Lc, _ = uc.shape
    L = ul.shape[1]
    nc = L // T
    cos, sin = _rope_tables(L)
    perm, avg = (jnp.asarray(t).astype(BF16) for t in _ret_tables())
    dec = jnp.repeat(decay_param, HEAD_DIM, axis=1)

    def col(Lx, j):
        return pl.BlockSpec((1, Lx, W_GRP), lambda b, j=j: (b, 0, j))

    def seq(Lx):
        return pl.BlockSpec((1, Lx, W_GRP), lambda b: (b, 0, 0))

    in_specs = ([col(Lc, j) for j in range(4)] + [col(L, j) for j in range(4)]
                + [pl.BlockSpec((L, W_GRP), lambda b: (0, 0)), pl.BlockSpec((L, W_GRP), lambda b: (0, 0)),
                   pl.BlockSpec((2, W_GRP), lambda b: (0, 0)),
                   pl.BlockSpec((2 * W_GRP, W_GRP), lambda b: (0, 0)),
                   pl.BlockSpec((2 * W_GRP, W_GRP), lambda b: (0, 0))])
    scratch = [pltpu.VMEM((L, W_GRP), F32), pltpu.VMEM((L, W_GRP), F32),
               pltpu.VMEM((T, N_HEADS * T), F32),
               pltpu.VMEM((nc + 1, W_GRP, W_GRP), F32), pltpu.VMEM((nc + 1, W_GRP, W_GRP), F32)]
    return pl.pallas_call(
        functools.partial(_ret_kernel, Lc, L),
        grid=(bsz,),
        in_specs=in_specs,
        out_specs=[seq(Lc), seq(L)],
        out_shape=[jax.ShapeDtypeStruct((bsz, Lc, W_GRP), F32), jax.ShapeDtypeStruct((bsz, L, W_GRP), F32)],
        scratch_shapes=scratch,
        compiler_params=_cparams(1),
        name="retention_mixer",
    )(uc, uc, uc, uc, ul, ul, ul, ul, jnp.asarray(cos), jnp.asarray(sin), dec, perm, avg)


def _s5_prep_kernel(are_ref, aim_ref, ldt_ref, bre_ref, bim_ref, cre_ref, cim_ref,
                    bmat_ref, cmat_ref, ab_ref):
    a_re = are_ref[0]
    a_im = aim_ref[0]
    dt = jnp.exp(ldt_ref[0])
    mag = jnp.exp(a_re * dt)
    ab_re = mag * jnp.cos(a_im * dt)
    ab_im = mag * jnp.sin(a_im * dt)
    den = a_re * a_re + a_im * a_im
    z_re = ((ab_re - 1.0) * a_re + ab_im * a_im) / den
    z_im = (ab_im * a_re - (ab_re - 1.0) * a_im) / den
    b_re = bre_ref[...]
    b_im = bim_ref[...]
    bmat_ref[0, :, 0:S5_NS] = (b_re * z_re - b_im * z_im).astype(BF16)
    bmat_ref[0, :, S5_NS:2 * S5_NS] = (b_re * z_im + b_im * z_re).astype(BF16)
    cmat_ref[0, 0:S5_NS, :] = cre_ref[0].astype(BF16)
    cmat_ref[0, S5_NS:2 * S5_NS, :] = (-cim_ref[0]).astype(BF16)
    ab_ref[0, :, 0:S5_NS] = ab_re
    ab_ref[0, :, S5_NS:2 * S5_NS] = ab_im


def _s5_prep_call(a_re, a_im, log_dt, b_re, b_im, c_re, c_im):
    eye = jnp.eye(S5_GROUPS, dtype=F32)
    b_re_bd = jnp.einsum('gpc,gh->gchp', b_re, eye).reshape(W_GRP, S5_NS)
    b_im_bd = jnp.einsum('gpc,gh->gchp', b_im, eye).reshape(W_GRP, S5_NS)
    c_re_bd = jnp.einsum('rgcp,gh->rgphc', c_re, eye).reshape(2, S5_NS, W_GRP)
    c_im_bd = jnp.einsum('rgcp,gh->rgphc', c_im, eye).reshape(2, S5_NS, W_GRP)
    ldt = jnp.repeat(log_dt, S5_STATE, axis=1).reshape(2, 1, S5_NS)
    row = lambda: pl.BlockSpec((1, 1, S5_NS), lambda r: (r, 0, 0))
    return pl.pallas_call(
        _s5_prep_kernel,
        grid=(2,),
        in_specs=[row(), row(), row(),
                  pl.BlockSpec((W_GRP, S5_NS), lambda r: (0, 0)), pl.BlockSpec((W_GRP, S5_NS), lambda r: (0, 0)),
                  pl.BlockSpec((1, S5_NS, W_GRP), lambda r: (r, 0, 0)),
                  pl.BlockSpec((1, S5_NS, W_GRP), lambda r: (r, 0, 0))],
        out_specs=[pl.BlockSpec((1, W_GRP, 2 * S5_NS), lambda r: (r, 0, 0)),
                   pl.BlockSpec((1, 2 * S5_NS, W_GRP), lambda r: (r, 0, 0)),
                   pl.BlockSpec((1, 1, 2 * S5_NS), lambda r: (r, 0, 0))],
        out_shape=[jax.ShapeDtypeStruct((2, W_GRP, 2 * S5_NS), BF16),
                   jax.ShapeDtypeStruct((2, 2 * S5_NS, W_GRP), BF16),
                   jax.ShapeDtypeStruct((2, 1, 2 * S5_NS), F32)],
        compiler_params=_cparams(1),
        name="s5_prep",
    )(a_re.reshape(2, 1, S5_NS), a_im.reshape(2, 1, S5_NS), ldt, b_re_bd, b_im_bd, c_re_bd, c_im_bd)


def _s5_scan_kernel(bsz, uf_ref, ub_ref, h0_ref, bmat_ref, cmat_ref, ab_ref,
                    yf_ref, yb_ref, hs_ref, x_s):
    @pl.when(pl.program_id(0) == 0)
    def _():
        hs_ref[...] = h0_ref[...]

    nq = S5_NS // W_GRP
    for r, (u_ref, y_ref) in enumerate(((uf_ref, yf_ref), (ub_ref, yb_ref))):
        u = u_ref[...].reshape(S5_TS * bsz, W_GRP)
        x_s[...] = _dot(u, bmat_ref[r])
        for q in range(nq):
            cr = slice(W_GRP * q, W_GRP * (q + 1))
            cim = slice(S5_NS + W_GRP * q, S5_NS + W_GRP * (q + 1))
            a_re = jnp.broadcast_to(ab_ref[r, :, cr], (bsz, W_GRP))
            a_im = jnp.broadcast_to(ab_ref[r, :, cim], (bsz, W_GRP))

            def body(kk, carry, cr=cr, cim=cim, a_re=a_re, a_im=a_im, r=r):
                h_re, h_im = carry
                t = (S5_TS - 1 - kk) if r == 1 else kk
                rows = pl.ds(pl.multiple_of(t * bsz, bsz), bsz)
                n_re = a_re * h_re - a_im * h_im + x_s[rows, cr]
                n_im = a_re * h_im + a_im * h_re + x_s[rows, cim]
                x_s[rows, cr] = n_re
                x_s[rows, cim] = n_im
                return n_re, n_im

            h_re, h_im = lax.fori_loop(0, S5_TS, body, (hs_ref[r, :, cr], hs_ref[r, :, cim]), unroll=4)
            hs_ref[r, :, cr] = h_re
            hs_ref[r, :, cim] = h_im
        y_ref[...] = _dot(x_s[...], cmat_ref[r]).reshape(S5_TS, bsz, W_GRP)


def _s5_scan_call(u3, h0, bmat, cmat, ab):
    L, bsz, _ = u3.shape
    n = L // S5_TS
    blk = (S5_TS, bsz, W_GRP)
    return pl.pallas_call(
        functools.partial(_s5_scan_kernel, bsz),
        grid=(n,),
        in_specs=[pl.BlockSpec(blk, lambda i: (i, 0, 0)),
                  pl.BlockSpec(blk, lambda i: (n - 1 - i, 0, 0)),
                  pl.BlockSpec((2, bsz, 2 * S5_NS), lambda i: (0, 0, 0)),
                  pl.BlockSpec((2, W_GRP, 2 * S5_NS), lambda i: (0, 0, 0)),
                  pl.BlockSpec((2, 2 * S5_NS, W_GRP), lambda i: (0, 0, 0)),
                  pl.BlockSpec((2, 1, 2 * S5_NS), lambda i: (0, 0, 0))],
        out_specs=[pl.BlockSpec(blk, lambda i: (i, 0, 0)),
                   pl.BlockSpec(blk, lambda i: (n - 1 - i, 0, 0)),
                   pl.BlockSpec((2, bsz, 2 * S5_NS), lambda i: (0, 0, 0))],
        out_shape=[jax.ShapeDtypeStruct((L, bsz, W_GRP), F32),
                   jax.ShapeDtypeStruct((L, bsz, W_GRP), F32),
                   jax.ShapeDtypeStruct((2, bsz, 2 * S5_NS), F32)],
        scratch_shapes=[pltpu.VMEM((S5_TS * bsz, 2 * S5_NS), F32)],
        compiler_params=_cparams(1),
        name="s5_scan",
    )(u3, u3, h0, bmat, cmat, ab)


def _s5_post_kernel(u_ref, yf_ref, yb_ref, d_ref, w_ref, b_ref, o_ref):
    y = u_ref[...] * d_ref[...] + yf_ref[...] + yb_ref[...]
    y = jax.nn.gelu(y)
    o_ref[0] = y * jax.nn.sigmoid(jnp.dot(y.astype(BF16), w_ref[...], preferred_element_type=F32)
                                  + b_ref[...])


def _s5_post_call(u2, yf2, yb2, d, glu_w, glu_b, bsz):
    L = u2.shape[0]
    tm = min(512, L)
    tok = lambda: pl.BlockSpec((tm, W_GRP), lambda b, i: (i, b))
    return pl.pallas_call(
        _s5_post_kernel,
        grid=(bsz, L // tm),
        in_specs=[tok(), tok(), tok(),
                  pl.BlockSpec((1, W_GRP), lambda b, i: (0, 0)),
                  pl.BlockSpec((W_GRP, W_GRP), lambda b, i: (0, 0)),
                  pl.BlockSpec((1, W_GRP), lambda b, i: (0, 0))],
        out_specs=pl.BlockSpec((1, tm, W_GRP), lambda b, i: (b, i, 0)),
        out_shape=jax.ShapeDtypeStruct((bsz, L, W_GRP), F32),
        compiler_params=_cparams(2),
        name="s5_post",
    )(u2, yf2, yb2, d.reshape(1, -1), glu_w.astype(BF16), glu_b.reshape(1, -1))


def _s5_mixer(u5c, u5l, bsz, p):
    bmat, cmat, ab = _s5_prep_call(p['s5_a_re'], p['s5_a_im'], p['s5_log_dt'], p['s5_b_re'],
                                   p['s5_b_im'], p['s5_c_re'], p['s5_c_im'])
    Lc, L = u5c.shape[0], u5l.shape[0]
    h0 = jnp.zeros((2, bsz, 2 * S5_NS), F32)
    yfc, ybc, hc = _s5_scan_call(u5c.reshape(Lc, bsz, W_GRP), h0, bmat, cmat, ab)
    yfl, ybl, _ = _s5_scan_call(u5l.reshape(L, bsz, W_GRP), hc, bmat, cmat, ab)
    post = functools.partial(_s5_post_call, d=p['s5_d'], glu_w=p['s5_glu_w'], glu_b=p['s5_glu_b'], bsz=bsz)
    s5c = post(u5c, yfc.reshape(Lc, -1), ybc.reshape(Lc, -1))
    s5l = post(u5l, yfl.reshape(L, -1), ybl.reshape(L, -1))
    return s5c, s5l


@functools.lru_cache(maxsize=None)
def _dft_tables(L):
    k = np.arange(L, dtype=np.int64)
    ft = (k[:, None] * k[None, :]) % (2 * L)
    ang = ft.astype(np.float64) * (math.pi / L)
    return np.cos(ang).astype(np.float32), np.sin(ang).astype(np.float32)


@functools.lru_cache(maxsize=None)
def _hyena_consts(L):
    t = np.linspace(0.0, 1.0, L, dtype=np.float32)[:, None]
    w = (2.0 * math.pi * np.arange(L, dtype=np.float32)[:, None] / L).astype(np.float32)
    bands = np.linspace(1e-4, HY_BANDS - 1, HY_BANDS, dtype=np.float32)[None, :]
    bw = (bands * w).astype(np.float32).astype(np.float64)
    feats = np.zeros((L, 128), np.float32)
    feats[:, 0:1] = t
    feats[:, 1:1 + HY_BANDS] = np.cos(bw)
    feats[:, 1 + HY_BANDS:HY_EMB] = -np.sin(bw)
    max_decay = math.log(1e-2) / 0.3
    min_decay = math.log(1e-2) / 1.5
    deltas = np.abs(np.linspace(min_decay, max_decay, 4 * W_GRP, dtype=np.float32))[None, :]
    return feats, deltas.astype(np.float32)


HY_RB = 256


def _hy_filter_kernel(feats_ref, w1_ref, b1_ref, fr_ref, w2_ref, b2_ref, w3_ref, del_ref,
                      p_ref, q_ref, nrm_ref, an_ref):
    i = pl.program_id(0)
    feats = feats_ref[...]
    fr = fr_ref[...]
    h = jnp.sin(fr * (_dot_f32(feats, w1_ref[...]) + b1_ref[...]))
    h = jnp.sin(fr * (_dot_f32(h, w2_ref[...]) + b2_ref[...]))
    h = _dot_f32(h, w3_ref[...])
    h = h * jnp.exp(-feats[:, 0:1] * del_ref[...])
    half = 2 * W_GRP
    hf = h[:, 0:half]
    row = lax.broadcasted_iota(jnp.int32, (HY_RB, half), 0) + i * HY_RB
    hb = jnp.where(row == 0, 0.0, h[:, half:2 * half])
    p = hf + hb
    sign = (1 - 2 * (row % 2)).astype(F32)
    p_ref[...] = p.astype(BF16)
    q_ref[...] = (hb - hf).astype(BF16)

    @pl.when(i == 0)
    def _():
        nrm_ref[...] = jnp.full_like(nrm_ref, EPS)
        an_ref[...] = jnp.zeros_like(an_ref)

    nrm_ref[...] += (jnp.sum(jnp.abs(hf), axis=0, keepdims=True)
                     + jnp.sum(jnp.abs(hb), axis=0, keepdims=True))
    an_ref[...] += jnp.sum(p * sign, axis=0, keepdims=True)


def _hy_spectrum_kernel(L, c_ref, s_ref, p_ref, q_ref, nrm_ref, ans_ref, a_ref, bc_ref, an_ref):
    i = pl.program_id(0)
    n = 2.0 * L
    inv = 1.0 / nrm_ref[...]
    row = lax.broadcasted_iota(jnp.int32, a_ref.shape, 0) + i * HY_RB
    wv = jnp.where(row == 0, 1.0 / n, 2.0 / n) * inv
    a_ref[...] = wv * jnp.dot(c_ref[...], p_ref[...], preferred_element_type=F32)
    bc_ref[...] = wv * jnp.dot(s_ref[...], q_ref[...], preferred_element_type=F32)
    an_ref[...] = ans_ref[...] * inv / n


def _hy_filter_call(L, w1, b1, freq, w2, b2, w3, cmat, smat):
    feats, deltas = _hyena_consts(L)
    w1p = jnp.pad(w1, ((0, 128 - HY_EMB), (0, 0)))
    half = 2 * W_GRP
    nb = L // HY_RB
    full = lambda a: pl.BlockSpec(a.shape, lambda i, nd=a.ndim: (0,) * nd)
    small = (w1p, b1.reshape(1, -1), freq.reshape(1, -1), w2, b2.reshape(1, -1), w3, jnp.asarray(deltas))
    rowblk = lambda w: pl.BlockSpec((HY_RB, w), lambda i: (i, 0))
    vec = pl.BlockSpec((1, half), lambda i: (0, 0))
    p, q, nrm, ans = pl.pallas_call(
        _hy_filter_kernel,
        grid=(nb,),
        in_specs=[rowblk(128)] + [full(a) for a in small],
        out_specs=[rowblk(half), rowblk(half), vec, vec],
        out_shape=[jax.ShapeDtypeStruct((L, half), BF16), jax.ShapeDtypeStruct((L, half), BF16),
                   jax.ShapeDtypeStruct((1, half), F32), jax.ShapeDtypeStruct((1, half), F32)],
        compiler_params=_cparams(1),
        name="hyena_filter",
    )(jnp.asarray(feats), *small)
    return pl.pallas_call(
        functools.partial(_hy_spectrum_kernel, L),
        grid=(nb,),
        in_specs=[rowblk(L), rowblk(L), full(p), full(q), vec, vec],
        out_specs=[rowblk(half), rowblk(half), vec],
        out_shape=[jax.ShapeDtypeStruct((L, half), F32), jax.ShapeDtypeStruct((L, half), F32),
                   jax.ShapeDtypeStruct((1, half), F32)],
        compiler_params=_cparams(1),
        name="hyena_spectrum",
    )(cmat, smat, p, q, nrm, ans)


def _hy_conv_kernel(L, rb, u_ref, cw_ref, cb_ref, c_ref, s_ref, a_ref, bc_ref, an_ref, bias_ref,
                    o_ref, x1_s, x2_s, z_s, zb_s, xy_s, qn_s):
    nc = L // T
    nb = L // rb

    def conv(c, carry):
        s = pl.multiple_of(c * T, T)
        rows = pl.ds(s, T)
        prev, nxt = _halo_rows(u_ref, s, c, nc, L)
        y = _dwconv_chunk(u_ref[0, rows, :], prev, nxt, cw_ref, cb_ref)
        x1_s[rows, :] = y[:, 0:W_GRP]
        x2_s[rows, :] = y[:, W_GRP:2 * W_GRP]
        z_s[rows, :] = y[:, 2 * W_GRP:3 * W_GRP]
        return carry

    lax.fori_loop(0, nc, conv, 0)

    for o, gate_s in enumerate((x1_s, x2_s)):
        cols = slice(W_GRP * o, W_GRP * (o + 1))
        qn_s[...] = jnp.zeros_like(qn_s)

        def cast(j, carry):
            rows = pl.ds(pl.multiple_of(j * rb, rb), rb)
            z = z_s[rows, :]
            sign = (1 - 2 * (lax.broadcasted_iota(jnp.int32, z.shape, 0) % 2)).astype(F32)
            qn_s[...] += jnp.sum(z * sign, axis=0, keepdims=True)
            zb_s[rows, :] = z.astype(BF16)
            return carry

        lax.fori_loop(0, nb, cast, 0)

        def fwd(j, carry, cols=cols):
            rows = pl.ds(pl.multiple_of(j * rb, rb), rb)
            zb = zb_s[...]
            p = jnp.dot(c_ref[rows, :], zb, preferred_element_type=F32)
            q = jnp.dot(s_ref[rows, :], zb, preferred_element_type=F32)
            a = a_ref[rows, cols]
            bc = bc_ref[rows, cols]
            xy_s[rows, :] = (p * a + q * bc).astype(BF16)
            xy_s[pl.ds(pl.multiple_of(L + j * rb, rb), rb), :] = (q * a - p * bc).astype(BF16)
            return carry

        lax.fori_loop(0, nb, fwd, 0)
        nyq = qn_s[...] * an_ref[:, cols]

        def inv(j, carry, cols=cols, gate_s=gate_s, nyq=nyq, o=o):
            rows = pl.ds(pl.multiple_of(j * rb, rb), rb)
            y = (jnp.dot(c_ref[rows, :], xy_s[0:L, :], preferred_element_type=F32)
                 + jnp.dot(s_ref[rows, :], xy_s[L:2 * L, :], preferred_element_type=F32))
            sign = (1 - 2 * (lax.broadcasted_iota(jnp.int32, y.shape, 0) % 2)).astype(F32)
            z = z_s[rows, :]
            res = gate_s[rows, :] * (y + sign * nyq + z * bias_ref[:, cols])
            if o == 0:
                z_s[rows, :] = res
            else:
                o_ref[0, rows, :] = res
            return carry

        lax.fori_loop(0, nb, inv, 0)


def _hy_conv_call(u, conv_w, conv_b, cmat, smat, a, bc, an, bias):
    bsz, L, _ = u.shape
    rb = min(512, L)
    half = 2 * W_GRP
    return pl.pallas_call(
        functools.partial(_hy_conv_kernel, L, rb),
        grid=(bsz,),
        in_specs=[pl.BlockSpec((1, L, HY_COLS), lambda b: (b, 0, 0)),
                  pl.BlockSpec((3, HY_COLS), lambda b: (0, 0)),
                  pl.BlockSpec((1, HY_COLS), lambda b: (0, 0)),
                  _const_spec((L, L)), _const_spec((L, L)),
                  _const_spec((L, half)), _const_spec((L, half)),
                  pl.BlockSpec((1, half), lambda b: (0, 0)),
                  pl.BlockSpec((1, half), lambda b: (0, 0))],
        out_specs=pl.BlockSpec((1, L, W_GRP), lambda b: (b, 0, 0)),
        out_shape=jax.ShapeDtypeStruct((bsz, L, W_GRP), F32),
        scratch_shapes=[pltpu.VMEM((L, W_GRP), F32), pltpu.VMEM((L, W_GRP), F32),
                        pltpu.VMEM((L, W_GRP), F32), pltpu.VMEM((L, W_GRP), BF16),
                        pltpu.VMEM((2 * L, W_GRP), BF16), pltpu.VMEM((1, W_GRP), F32)],
        compiler_params=_cparams(1),
        name="hyena_conv",
    )(u, conv_w, conv_b.reshape(1, -1), cmat, smat, a, bc, an, bias.reshape(1, -1))


def _hyena_mixer(u, p):
    L = u.shape[1]
    cnp, snp = _dft_tables(L)
    cmat = jnp.asarray(cnp).astype(BF16)
    smat = jnp.asarray(snp).astype(BF16)
    a, bc, an = _hy_filter_call(L, p['hy_w1'], p['hy_b1'], p['hy_freq'], p['hy_w2'], p['hy_b2'],
                                p['hy_w3'], cmat, smat)
    return _hy_conv_call(u, p['hy_conv_w'], p['hy_conv_b'], cmat, smat, a, bc, an, p['hy_bias'])


def _out_ffn_kernel(final, h_ref, ssd_ref, hy_ref, ret_ref, s5_ref, mod_ref, g2_ref, wo_ref,
                    wup_ref, wdn_ref, fg_ref, o_ref, acc_s):
    y = jnp.zeros(h_ref.shape[1:], F32)
    for j, ref in enumerate((ssd_ref, hy_ref, ret_ref, s5_ref)):
        y = y + jnp.dot(ref[0].astype(BF16), wo_ref[W_GRP * j:W_GRP * (j + 1), :],
                        preferred_element_type=F32)
    h1 = h_ref[0] + mod_ref[0, 2:3, :] * y
    xn = h1 * lax.rsqrt(jnp.mean(h1 * h1, axis=-1, keepdims=True) + EPS) * g2_ref[...]
    xm = (xn * (1.0 + mod_ref[0, 4:5, :]) + mod_ref[0, 3:4, :]).astype(BF16)
    fc = 256
    for j in range(D_FF // fc):
        gg = jnp.dot(xm, wup_ref[:, fc * j:fc * (j + 1)], preferred_element_type=F32)
        uu = jnp.dot(xm, wup_ref[:, D_FF + fc * j:D_FF + fc * (j + 1)], preferred_element_type=F32)
        part = jnp.dot((_silu(gg) * uu).astype(BF16), wdn_ref[fc * j:fc * (j + 1), :],
                       preferred_element_type=F32)
        if j == 0:
            acc_s[...] = part
        else:
            acc_s[...] += part
    h2 = h1 + mod_ref[0, 5:6, :] * acc_s[...]
    if final:
        h2 = h2 * lax.rsqrt(jnp.mean(h2 * h2, axis=-1, keepdims=True) + EPS) * fg_ref[...]
    o_ref[0] = h2


def _out_ffn_call(h, mix, mods, ctx_stream, g2, wo, wup, wdn, final_g, final):
    bsz, L, _ = h.shape
    tm = min(512, L)
    mod_map = (lambda b, i: (bsz, 0, 0)) if ctx_stream else (lambda b, i: (b, 0, 0))
    tok = lambda w: pl.BlockSpec((1, tm, w), lambda b, i: (b, i, 0))
    return pl.pallas_call(
        functools.partial(_out_ffn_kernel, final),
        grid=(bsz, L // tm),
        in_specs=[tok(D_MODEL), tok(W_GRP), tok(W_GRP), tok(W_GRP), tok(W_GRP),
                  pl.BlockSpec((1, 6, D_MODEL), mod_map),
                  pl.BlockSpec((1, D_MODEL), lambda b, i: (0, 0)),
                  _const_spec((D_MODEL, D_MODEL)),
                  _const_spec((D_MODEL, 2 * D_FF)),
                  _const_spec((D_FF, D_MODEL)),
                  pl.BlockSpec((1, D_MODEL), lambda b, i: (0, 0))],
        out_specs=tok(D_MODEL),
        out_shape=jax.ShapeDtypeStruct((bsz, L, D_MODEL), F32),
        scratch_shapes=[pltpu.VMEM((tm, D_MODEL), F32)],
        compiler_params=_cparams(2),
        name="out_ffn",
    )(h, *mix, mods, g2, wo, wup, wdn, final_g)


def kernel(x, c, ctx, c_ctx, mod_w, mod_b, norm1_g, norm2_g, w_in, w_out, ssd_conv_w, ssd_conv_b, ssd_a_log, ssd_dt_bias, ssd_d, ssd_norm_g, hy_conv_w, hy_conv_b, hy_w1, hy_b1, hy_freq, hy_w2, hy_b2, hy_w3, hy_bias, ret_decay, s5_a_re, s5_a_im, s5_log_dt, s5_b_re, s5_b_im, s5_c_re, s5_c_im, s5_d, s5_glu_w, s5_glu_b, ffn_w_up, ffn_w_down, final_norm_g):
    bsz = x.shape[0]
    depth = mod_w.shape[0]
    sc = jnp.concatenate([c, c_ctx[None, :], jnp.zeros((MOD_ROWS - bsz - 1, D_MODEL), F32)], axis=0)
    mods_all = _mod_call(sc, mod_w, mod_b).reshape(depth, MOD_ROWS, 6, D_MODEL)
    fg = final_norm_g.reshape(1, -1)
    h_l, h_c = x, ctx
    o_xbc, o_dt, o_hy = W_GRP, W_GRP + SSD_XBC, W_GRP + SSD_XBC + 2 * N_HEADS
    o_ret = o_hy + HY_COLS
    o_s5 = o_ret + RET_COLS
    for i in range(depth):
        last = i == depth - 1
        wi = w_in[i]
        wdt = wi[:, o_dt:o_hy]
        wcat = jnp.concatenate([wi[:, 0:o_dt], wi[:, o_hy:o_s5 + W_GRP], wdt,
                                jnp.zeros((D_MODEL, 128 - 2 * N_HEADS), F32)], axis=1).astype(BF16)
        mods = mods_all[i]
        g1 = norm1_g[i].reshape(1, -1)
        zl, xbcl, hyl, retl, dtl, s5l = _inproj_call(h_l, g1, mods, False, wcat)
        zc, xbcc, hyc, retc, dtc, s5c = _inproj_call(h_c, g1, mods, True, wcat)
        ssd_c, ssd_l = _ssd_call((zc, xbcc, dtc), (zl, xbcl, dtl), ssd_conv_w[i],
                                 ssd_conv_b[i], ssd_a_log[i], ssd_dt_bias[i], ssd_d[i], ssd_norm_g[i])
        ret_c, ret_l = _ret_call(retc, retl, ret_decay[i])
        p = dict(s5_a_re=s5_a_re[i], s5_a_im=s5_a_im[i], s5_log_dt=s5_log_dt[i], s5_b_re=s5_b_re[i],
                 s5_b_im=s5_b_im[i], s5_c_re=s5_c_re[i], s5_c_im=s5_c_im[i], s5_d=s5_d[i],
                 s5_glu_w=s5_glu_w[i], s5_glu_b=s5_glu_b[i],
                 hy_conv_w=hy_conv_w[i], hy_conv_b=hy_conv_b[i], hy_w1=hy_w1[i], hy_b1=hy_b1[i],
                 hy_freq=hy_freq[i], hy_w2=hy_w2[i], hy_b2=hy_b2[i], hy_w3=hy_w3[i], hy_bias=hy_bias[i])
        s5_c, s5_l = _s5_mixer(s5c, s5l, bsz, p)
        hy_l = _hyena_mixer(hyl, p)
        g2 = norm2_g[i].reshape(1, -1)
        wo = w_out[i].astype(BF16)
        wup = ffn_w_up[i].astype(BF16)
        wdn = ffn_w_down[i].astype(BF16)
        h_l = _out_ffn_call(h_l, (ssd_l, hy_l, ret_l, s5_l), mods, False, g2, wo, wup, wdn, fg, last)
        if not last:
            hy_c = _hyena_mixer(hyc, p)
            h_c = _out_ffn_call(h_c, (ssd_c, hy_c, ret_c, s5_c), mods, True, g2, wo, wup, wdn, fg, False)
    return h_l
```

```python
import functools
import math

import numpy as np
import jax
import jax.numpy as jnp
from jax import lax
from jax.experimental import pallas as pl
from jax.experimental.pallas import tpu as pltpu

F32 = jnp.float32
BF16 = jnp.bfloat16
EPS = 1e-6

D_MODEL = 1024
W_GRP = 256
T = 128
N_HEADS = 4
HEAD_DIM = 64
SSD_STATE = 64
SSD_XBC = W_GRP + 2 * SSD_STATE
HY_COLS = 3 * W_GRP
RET_COLS = 4 * W_GRP
GRID_W = 64
ROPE_BASE = 10000.0
HY_EMB = 33
HY_BANDS = 16
HY_FILT = 64
S5_GROUPS = 16
S5_CH = 16
S5_STATE = 64
S5_NS = S5_GROUPS * S5_STATE
D_FF = 2816
S5_TS = 64
MOD_ROWS = 24

VMEM_LIMIT = 56 * 1024 * 1024


def _cparams(n_grid):
    return pltpu.CompilerParams(dimension_semantics=("arbitrary",) * n_grid,
                                vmem_limit_bytes=VMEM_LIMIT)


def _dot(a, b):
    return jnp.dot(a.astype(BF16), b.astype(BF16), preferred_element_type=F32)


def _dot_nt(a, b):
    return lax.dot_general(a.astype(BF16), b.astype(BF16), (((1,), (1,)), ((), ())),
                           preferred_element_type=F32)


def _dot_tn(a, b):
    return lax.dot_general(a.astype(BF16), b.astype(BF16), (((0,), (0,)), ((), ())),
                           preferred_element_type=F32)


def _dot_f32(a, b):
    return jnp.dot(a, b, preferred_element_type=F32, precision=lax.Precision.HIGHEST)


def _silu(x):
    return x * jax.nn.sigmoid(x)


def _const_spec(shape):
    nd = len(shape)
    return pl.BlockSpec(shape, lambda *_: (0,) * nd, pipeline_mode=pl.Buffered(1))


def _mod_kernel(sc_ref, w_ref, b_ref, o_ref):
    s = _silu(sc_ref[...])
    o_ref[0] = _dot_f32(s, w_ref[0]) + b_ref[0]


def _mod_call(sc, mod_w, mod_b):
    depth, _, n = mod_w.shape
    tn = 1536
    return pl.pallas_call(
        _mod_kernel,
        grid=(depth, n // tn),
        in_specs=[pl.BlockSpec((MOD_ROWS, D_MODEL), lambda l, j: (0, 0)),
                  pl.BlockSpec((1, D_MODEL, tn), lambda l, j: (l, 0, j)),
                  pl.BlockSpec((1, 1, tn), lambda l, j: (l, 0, j))],
        out_specs=pl.BlockSpec((1, MOD_ROWS, tn), lambda l, j: (l, 0, j)),
        out_shape=jax.ShapeDtypeStruct((depth, MOD_ROWS, n), F32),
        compiler_params=_cparams(2),
        name="adaln_mod",
    )(sc, mod_w, mod_b.reshape(depth, 1, n))


def _inproj_kernel(x_ref, g_ref, mod_ref, w_ref, z_ref, xbc_ref, hy_ref, ret_ref, dt_ref, s5_ref):
    x = x_ref[0]
    xn = x * lax.rsqrt(jnp.mean(x * x, axis=-1, keepdims=True) + EPS) * g_ref[...]
    xm = (xn * (1.0 + mod_ref[0, 1:2, :]) + mod_ref[0, 0:1, :]).astype(BF16)
    o = 0
    for ref, width in ((z_ref, W_GRP), (xbc_ref, SSD_XBC), (hy_ref, HY_COLS),
                       (ret_ref, RET_COLS)):
        ref[0] = jnp.dot(xm, w_ref[:, o:o + width], preferred_element_type=F32).astype(BF16)
        o += width
    s5_ref[...] = jnp.dot(xm, w_ref[:, o:o + W_GRP], preferred_element_type=F32).astype(BF16)
    o += W_GRP
    dt_ref[0] = jnp.dot(xm, w_ref[:, o:o + 128], preferred_element_type=F32)


def _inproj_call(h, g, mods, ctx_stream, wcat):
    bsz, L, _ = h.shape
    tm = min(512, L)
    ncols = wcat.shape[1]
    mod_map = (lambda b, i: (bsz, 0, 0)) if ctx_stream else (lambda b, i: (b, 0, 0))
    tok = lambda w: pl.BlockSpec((1, tm, w), lambda b, i: (b, i, 0))
    out_shape = [jax.ShapeDtypeStruct((bsz, L, W_GRP), BF16),
                 jax.ShapeDtypeStruct((bsz, L, SSD_XBC), BF16),
                 jax.ShapeDtypeStruct((bsz, L, HY_COLS), BF16),
                 jax.ShapeDtypeStruct((bsz, L, RET_COLS), BF16),
                 jax.ShapeDtypeStruct((bsz, L, 128), F32),
                 jax.ShapeDtypeStruct((L, bsz * W_GRP), BF16)]
    out_specs = [tok(W_GRP), tok(SSD_XBC), tok(HY_COLS), tok(RET_COLS), tok(128),
                 pl.BlockSpec((tm, W_GRP), lambda b, i: (i, b))]
    return pl.pallas_call(
        _inproj_kernel,
        grid=(bsz, L // tm),
        in_specs=[tok(D_MODEL),
                  pl.BlockSpec((1, D_MODEL), lambda b, i: (0, 0)),
                  pl.BlockSpec((1, 6, D_MODEL), mod_map),
                  _const_spec((D_MODEL, ncols))],
        out_specs=out_specs,
        out_shape=out_shape,
        compiler_params=_cparams(2),
        name="in_proj",
    )(h, g, mods, wcat)


def _halo_rows(ref, s, c, nc, L):
    sp = pl.multiple_of(jnp.maximum(s - 16, 0), 16)
    prev = ref[0, pl.ds(sp, 16), :][15:16, :].astype(F32)
    prev = jnp.where(c > 0, prev, 0.0)
    sn = pl.multiple_of(jnp.minimum(s + T, L - 16), 16)
    nxt = ref[0, pl.ds(sn, 16), :][0:1, :].astype(F32)
    nxt = jnp.where(c < nc - 1, nxt, 0.0)
    return prev, nxt


def _dwconv_chunk(x, prev, nxt, w_ref, b_ref):
    n = x.shape[0]
    row = lax.broadcasted_iota(jnp.int32, x.shape, 0)
    up = jnp.where(row == 0, prev, pltpu.roll(x, 1, 0))
    dn = jnp.where(row == n - 1, nxt, pltpu.roll(x, n - 1, 0))
    return up * w_ref[0:1, :] + x * w_ref[1:2, :] + dn * w_ref[2:3, :] + b_ref[...]


def _cumsum(x, axis):
    n = x.shape[axis]
    idx = lax.broadcasted_iota(jnp.int32, x.shape, axis)
    s = 1
    while s < n:
        x = x + jnp.where(idx >= s, pltpu.roll(x, s, axis), 0.0)
        s *= 2
    return x


def _expand_heads(c, exp_ref):
    hi = c.astype(BF16)
    lo = (c - hi.astype(F32)).astype(BF16)
    return jnp.dot(jnp.concatenate([hi, lo], axis=1), exp_ref[...], preferred_element_type=F32)


def _state_recurrence(hf_ref, hb_ref, decf_ref, decb_ref, nc, h0f, h0b):
    hf_ref[0] = h0f
    hb_ref[nc] = h0b

    def fwd(c, carry):
        hf_ref[c + 1] = decf_ref[c] * hf_ref[c] + hf_ref[c + 1]
        return carry

    lax.fori_loop(0, nc, fwd, 0)

    def bwd(k, carry):
        c = nc - 1 - k
        hb_ref[c] = decb_ref[c] * hb_ref[c + 1] + hb_ref[c]
        return carry

    lax.fori_loop(0, nc, bwd, 0)
    return hf_ref[nc], hb_ref[0]


def _split3(x):
    hi = x.astype(BF16)
    r = x - hi.astype(F32)
    mid = r.astype(BF16)
    lo = (r - mid.astype(F32)).astype(BF16)
    return jnp.concatenate([hi, mid, lo], axis=1)


def _tile_heads_bd(x, bdmask):
    xb = x.astype(BF16)
    return jnp.where(bdmask, jnp.concatenate([xb] * N_HEADS, axis=0), 0)


def _rows_to_lanes(a, lo):
    return jnp.concatenate([a[lo + h:lo + h + 1, :] for h in range(N_HEADS)], axis=1)


def _ssd_sequence(L, z_ref, xbc_ref, dt_ref, y_ref, prm, scr, h0f, h0b):
    (cw_ref, cb_ref, alog_row, bias_row, dskip_ref, ng_ref, exp_ref, exp128_ref, sel_ref) = prm
    (xs_s, bc_s, ee_s, acs_s, dt_s, hf_s, hb_s, decf_s, decb_s) = scr
    nc = L // T
    lane = lax.broadcasted_iota(jnp.int32, (T, 128), 1)
    a_row = -jnp.exp(alog_row[...])

    def phase1(c, carry):
        s = pl.multiple_of(c * T, T)
        rows = pl.ds(s, T)
        prev, nxt = _halo_rows(xbc_ref, s, c, nc, L)
        xact = _silu(_dwconv_chunk(xbc_ref[0, rows, :].astype(F32), prev, nxt, cw_ref, cb_ref))
        xs = xact[:, 0:W_GRP]
        bm = xact[:, W_GRP:W_GRP + SSD_STATE]
        xs_s[rows, :] = xs
        bc_s[rows, :] = xact[:, W_GRP:SSD_XBC]
        dt = jax.nn.softplus(dt_ref[0, rows, :] + bias_row[...])
        dt_s[rows, :] = dt
        la = dt * a_row
        acs_f = _cumsum(la, 0)
        tot = acs_f[T - 1:T, :]
        acs = jnp.where(lane < N_HEADS, acs_f, tot - acs_f + la)
        acs_s[rows, :] = acs
        ee = _expand_heads(jnp.exp(acs), exp_ref)
        ee_s[rows, :] = ee
        wx = _expand_heads(dt * jnp.exp(tot - acs), exp_ref)
        hf_s[c + 1] = _dot_tn(bm, xs * wx[:, 0:W_GRP])
        hb_s[c] = _dot_tn(bm, xs * wx[:, W_GRP:2 * W_GRP])
        decf_s[c] = ee[T - 1:T, 0:W_GRP]
        decb_s[c] = ee[0:1, W_GRP:2 * W_GRP]
        return carry

    lax.fori_loop(0, nc, phase1, 0, unroll=2)
    hf_fin, hb_fin = _state_recurrence(hf_s, hb_s, decf_s, decb_s, nc, h0f, h0b)

    ri = lax.broadcasted_iota(jnp.int32, (T, N_HEADS * T), 0)
    ci = lax.broadcasted_iota(jnp.int32, (T, N_HEADS * T), 1) % T
    lower = ci <= ri
    upper = ci >= ri
    r4 = lax.broadcasted_iota(jnp.int32, (N_HEADS * T, W_GRP), 0) // T
    c4 = lax.broadcasted_iota(jnp.int32, (N_HEADS * T, W_GRP), 1) // HEAD_DIM
    bdmask = r4 == c4
    nl = N_HEADS * T

    def phase3(c, carry):
        s = pl.multiple_of(c * T, T)
        rows = pl.ds(s, T)
        xs = xs_s[rows, :]
        bc = bc_s[rows, :]
        bm = bc[:, 0:SSD_STATE]
        cm = bc[:, SSD_STATE:2 * SSD_STATE]
        a3 = _split3(acs_s[rows, :])
        col = jnp.dot(a3, exp128_ref[...], preferred_element_type=F32)
        acst = lax.dot_general(sel_ref[...], a3, (((1,), (1,)), ((), ())), preferred_element_type=F32)
        dtt = lax.dot_general(sel_ref[...], _split3(dt_s[rows, :]), (((1,), (1,)), ((), ())),
                              preferred_element_type=F32)
        g = _dot_nt(cm, bm)
        g4 = jnp.concatenate([g] * N_HEADS, axis=1)
        df = jnp.exp(jnp.where(lower, col[:, 0:nl] - _rows_to_lanes(acst, 0), -jnp.inf))
        db = jnp.exp(jnp.where(upper, col[:, nl:2 * nl] - _rows_to_lanes(acst, N_HEADS), -jnp.inf))
        w = g4 * (df * _rows_to_lanes(dtt, 0) + db * _rows_to_lanes(dtt, N_HEADS))
        y = jnp.dot(w.astype(BF16), _tile_heads_bd(xs, bdmask), preferred_element_type=F32)
        ee = ee_s[rows, :]
        y = y + ee[:, 0:W_GRP] * _dot(cm, hf_s[c]) + ee[:, W_GRP:2 * W_GRP] * _dot(cm, hb_s[c + 1])
        y = y + xs * dskip_ref[...]
        y = y * _silu(z_ref[0, rows, :].astype(F32))
        y = y * lax.rsqrt(jnp.mean(y * y, axis=-1, keepdims=True) + EPS) * ng_ref[...]
        y_ref[0, rows, :] = y
        return carry

    lax.fori_loop(0, nc, phase3, 0, unroll=2)
    return hf_fin, hb_fin


def _ssd_kernel(Lc, L, zc_ref, xbcc_ref, dtc_ref, zl_ref, xbcl_ref, dtl_ref,
                cw_ref, cb_ref, alog_row, bias_row, dskip_ref, ng_ref, exp_ref, exp128_ref, sel_ref,
                yc_ref, yl_ref, *scr):
    prm = (cw_ref, cb_ref, alog_row, bias_row, dskip_ref, ng_ref, exp_ref, exp128_ref, sel_ref)
    zero = jnp.zeros((SSD_STATE, W_GRP), F32)
    hf, hb = _ssd_sequence(Lc, zc_ref, xbcc_ref, dtc_ref, yc_ref, prm, scr, zero, zero)
    _ssd_sequence(L, zl_ref, xbcl_ref, dtl_ref, yl_ref, prm, scr, hf, hb)


@functools.lru_cache(maxsize=None)
def _ssd_tables():
    exp64 = np.zeros((256, 512), np.float32)
    exp128 = np.zeros((384, 2 * N_HEADS * T), np.float32)
    sel = np.zeros((8, 384), np.float32)
    for r in range(2):
        for h in range(N_HEADS):
            m = r * N_HEADS + h
            for part in range(2):
                exp64[128 * part + m, r * 256 + 64 * h:r * 256 + 64 * (h + 1)] = 1.0
            for part in range(3):
                exp128[128 * part + m, (r * N_HEADS + h) * T:(r * N_HEADS + h + 1) * T] = 1.0
                sel[m, 128 * part + m] = 1.0
    return exp64, exp128, sel


def _pad_row(v, n=128):
    v = v.reshape(1, -1)
    return jnp.pad(v, ((0, 0), (0, n - v.shape[1])))


def _ssd_call(uc, ul, conv_w, conv_b, a_log, dt_bias, d_skip, norm_g):
    zc, xbcc, dtc = uc
    zl, xbcl, dtl = ul
    bsz, Lc, _ = zc.shape
    L = zl.shape[1]
    nc = L // T
    alog_row = _pad_row(a_log)
    bias_row = _pad_row(dt_bias)
    dskip = jnp.repeat(d_skip, HEAD_DIM).reshape(1, W_GRP)
    exp64, exp128, sel = (jnp.asarray(t).astype(BF16) for t in _ssd_tables())

    def seq(Lx, w):
        return pl.BlockSpec((1, Lx, w), lambda b: (b, 0, 0))

    def small(shape):
        return pl.BlockSpec(shape, lambda b: (0,) * len(shape))

    in_specs = [seq(Lc, W_GRP), seq(Lc, SSD_XBC), seq(Lc, 128),
                seq(L, W_GRP), seq(L, SSD_XBC), seq(L, 128),
                small((3, SSD_XBC)), small((1, SSD_XBC)), small((1, 128)),
                small((1, 128)), small((1, W_GRP)), small((1, W_GRP)),
                small(exp64.shape), small(exp128.shape), small(sel.shape)]
    scratch = [pltpu.VMEM((L, W_GRP), F32), pltpu.VMEM((L, 128), F32), pltpu.VMEM((L, 512), F32),
               pltpu.VMEM((L, 128), F32), pltpu.VMEM((L, 128), F32),
               pltpu.VMEM((nc + 1, SSD_STATE, W_GRP), F32), pltpu.VMEM((nc + 1, SSD_STATE, W_GRP), F32),
               pltpu.VMEM((nc, 1, W_GRP), F32), pltpu.VMEM((nc, 1, W_GRP), F32)]
    return pl.pallas_call(
        functools.partial(_ssd_kernel, Lc, L),
        grid=(bsz,),
        in_specs=in_specs,
        out_specs=[seq(Lc, W_GRP), seq(L, W_GRP)],
        out_shape=[jax.ShapeDtypeStruct((bsz, Lc, W_GRP), F32), jax.ShapeDtypeStruct((bsz, L, W_GRP), F32)],
        scratch_shapes=scratch,
        compiler_params=_cparams(1),
        name="ssd_mixer",
    )(zc, xbcc, dtc, zl, xbcl, dtl, conv_w, conv_b.reshape(1, -1),
      alog_row, bias_row, dskip, norm_g.reshape(1, -1), exp64, exp128, sel)


def _group_norm_heads(y, avg_ref):
    def gmean(x):
        hi = x.astype(BF16)
        lo = (x - hi.astype(F32)).astype(BF16)
        return jnp.dot(jnp.concatenate([hi, lo], axis=1), avg_ref[...], preferred_element_type=F32)

    yc = y - gmean(y)
    return yc * lax.rsqrt(gmean(yc * yc) + EPS)


def _ret_sequence(L, rope, q_ref, k_ref, v_ref, g_ref, y_ref, cos_ref, sin_ref, perm_ref, avg_ref,
                  tabs, scr, h0f, h0b):
    (ef, eb, wf, wb, decf, decb, bdmask) = tabs
    (qr_s, kr_s, dm_s, hf_s, hb_s) = scr
    nc = L // T
    scale = HEAD_DIM ** -0.5
    r4 = lax.broadcasted_iota(jnp.int32, (N_HEADS * T, W_GRP), 0) // T
    c4 = lax.broadcasted_iota(jnp.int32, (N_HEADS * T, W_GRP), 1) // HEAD_DIM
    stackmask = r4 == c4

    def rot(x, rows):
        partner = jnp.dot(x, perm_ref[...], preferred_element_type=F32)
        return x.astype(F32) * cos_ref[rows, :] + partner * sin_ref[rows, :]

    def phase1(c, carry):
        rows = pl.ds(pl.multiple_of(c * T, T), T)
        q = q_ref[0, rows, :]
        k = k_ref[0, rows, :]
        if rope:
            q = rot(q, rows)
            k = rot(k, rows)
        k = k.astype(F32) * scale
        qr_s[rows, :] = q.astype(F32)
        kr_s[rows, :] = k
        v = v_ref[0, rows, :]
        hf_s[c + 1] = _dot_tn(k * wf, v) * bdmask
        hb_s[c] = _dot_tn(k * wb, v) * bdmask
        return carry

    lax.fori_loop(0, nc, phase1, 0, unroll=2)

    hf_s[0] = h0f
    hb_s[nc] = h0b

    def fwd(c, carry):
        hf_s[c + 1] = decf * hf_s[c] + hf_s[c + 1]
        return carry

    lax.fori_loop(0, nc, fwd, 0)

    def bwd(kk, carry):
        c = nc - 1 - kk
        hb_s[c] = decb * hb_s[c + 1] + hb_s[c]
        return carry

    lax.fori_loop(0, nc, bwd, 0)

    def phase3(c, carry):
        rows = pl.ds(pl.multiple_of(c * T, T), T)
        q = qr_s[rows, :]
        v = v_ref[0, rows, :]
        inter = _dot(q * ef, hf_s[c]) + _dot(q * eb, hb_s[c + 1])
        sc = lax.dot_general(q.astype(BF16), _tile_heads_bd(kr_s[rows, :], stackmask),
                             (((1,), (1,)), ((), ())), preferred_element_type=F32)
        y = jnp.dot((sc * dm_s[...]).astype(BF16), _tile_heads_bd(v, stackmask),
                    preferred_element_type=F32) + inter
        y_ref[0, rows, :] = _silu(g_ref[0, rows, :].astype(F32)) * _group_norm_heads(y, avg_ref)
        return carry

    lax.fori_loop(0, nc, phase3, 0, unroll=2)
    return hf_s[nc], hb_s[0]


def _ret_kernel(Lc, L, qc_ref, kc_ref, vc_ref, gc_ref, ql_ref, kl_ref, vl_ref, gl_ref,
                cos_ref, sin_ref, dec_ref, perm_ref, avg_ref, yc_ref, yl_ref,
                qr_s, kr_s, dm_s, hf_s, hb_s):
    lg = -jnp.exp(dec_ref[...])
    lgf = lg[0:1, :]
    lgb = lg[1:2, :]
    i = lax.broadcasted_iota(jnp.int32, (T, W_GRP), 0).astype(F32)
    ef = jnp.exp(lgf * (i + 1.0))
    eb = jnp.exp(lgb * (T - i))
    wf = jnp.exp(lgf * (T - 1.0 - i))
    wb = jnp.exp(lgb * i)
    decf = jnp.exp(lgf * float(T))
    decb = jnp.exp(lgb * float(T))
    r2 = lax.broadcasted_iota(jnp.int32, (W_GRP, W_GRP), 0) // HEAD_DIM
    c2 = lax.broadcasted_iota(jnp.int32, (W_GRP, W_GRP), 1) // HEAD_DIM
    bdmask = (r2 == c2).astype(F32)
    ri = lax.broadcasted_iota(jnp.int32, (T, T), 0)
    ci = lax.broadcasted_iota(jnp.int32, (T, T), 1)
    d = (ri - ci).astype(F32)
    for h in range(N_HEADS):
        lf = lgf[:, HEAD_DIM * h:HEAD_DIM * h + 1]
        lb = lgb[:, HEAD_DIM * h:HEAD_DIM * h + 1]
        dm_s[:, T * h:T * (h + 1)] = (jnp.exp(jnp.where(ci <= ri, lf * d, -jnp.inf))
                                      + jnp.exp(jnp.where(ci >= ri, -lb * d, -jnp.inf)))
    tabs = (ef, eb, wf, wb, decf, decb, bdmask)
    scr = (qr_s, kr_s, dm_s, hf_s, hb_s)
    zero = jnp.zeros((W_GRP, W_GRP), F32)
    hf, hb = _ret_sequence(Lc, False, qc_ref, kc_ref, vc_ref, gc_ref, yc_ref, cos_ref, sin_ref,
                           perm_ref, avg_ref, tabs, scr, zero, zero)
    _ret_sequence(L, True, ql_ref, kl_ref, vl_ref, gl_ref, yl_ref, cos_ref, sin_ref,
                  perm_ref, avg_ref, tabs, scr, hf, hb)


@functools.lru_cache(maxsize=None)
def _rope_tables(L):
    t = np.arange(L)
    f = 16
    inv = (ROPE_BASE ** (-np.arange(f, dtype=np.float32) / f)).astype(np.float32)
    cos = np.zeros((L, HEAD_DIM), np.float32)
    sin = np.zeros((L, HEAD_DIM), np.float32)
    for base, pos in ((0, t // GRID_W), (32, t % GRID_W)):
        ang = pos.astype(np.float32)[:, None] * inv[None, :]
        ang = ang.astype(np.float32).astype(np.float64)
        cos[:, base:base + f] = np.cos(ang)
        cos[:, base + f:base + 2 * f] = np.cos(ang)
        sin[:, base:base + f] = -np.sin(ang)
        sin[:, base + f:base + 2 * f] = np.sin(ang)
    return np.tile(cos, (1, N_HEADS)), np.tile(sin, (1, N_HEADS))


@functools.lru_cache(maxsize=None)
def _ret_tables():
    perm = np.zeros((W_GRP, W_GRP), np.float32)
    avg = np.zeros((2 * W_GRP, W_GRP), np.float32)
    for l in range(W_GRP):
        src = l + 16 if (l % 32) < 16 else l - 16
        perm[src, l] = 1.0
        g = l // HEAD_DIM
        for part in range(2):
            avg[part * W_GRP + g * HEAD_DIM:part * W_GRP + (g + 1) * HEAD_DIM, l] = 1.0 / HEAD_DIM
    return perm, avg


def _ret_call(uc, ul, decay_param):
    bsz, Lc, _ = uc.shape
    L = ul.shape[1]
    nc = L // T
    cos, sin = _rope_tables(L)
    perm, avg = (jnp.asarray(t).astype(BF16) for t in _ret_tables())
    dec = jnp.repeat(decay_param, HEAD_DIM, axis=1)

    def col(Lx, j):
        return pl.BlockSpec((1, Lx, W_GRP), lambda b, j=j: (b, 0, j))

    def seq(Lx):
        return pl.BlockSpec((1, Lx, W_GRP), lambda b: (b, 0, 0))

    in_specs = ([col(Lc, j) for j in range(4)] + [col(L, j) for j in range(4)]
                + [pl.BlockSpec((L, W_GRP), lambda b: (0, 0)), pl.BlockSpec((L, W_GRP), lambda b: (0, 0)),
                   pl.BlockSpec((2, W_GRP), lambda b: (0, 0)),
                   pl.BlockSpec((W_GRP, W_GRP), lambda b: (0, 0)),
                   pl.BlockSpec((2 * W_GRP, W_GRP), lambda b: (0, 0))])
    scratch = [pltpu.VMEM((L, W_GRP), F32), pltpu.VMEM((L, W_GRP), F32),
               pltpu.VMEM((T, N_HEADS * T), F32),
               pltpu.VMEM((nc + 1, W_GRP, W_GRP), F32), pltpu.VMEM((nc + 1, W_GRP, W_GRP), F32)]
    return pl.pallas_call(
        functools.partial(_ret_kernel, Lc, L),
        grid=(bsz,),
        in_specs=in_specs,
        out_specs=[seq(Lc), seq(L)],
        out_shape=[jax.ShapeDtypeStruct((bsz, Lc, W_GRP), F32), jax.ShapeDtypeStruct((bsz, L, W_GRP), F32)],
        scratch_shapes=scratch,
        compiler_params=_cparams(1),
        name="retention_mixer",
    )(uc, uc, uc, uc, ul, ul, ul, ul, jnp.asarray(cos), jnp.asarray(sin), dec, perm, avg)


def _s5_prep_kernel(are_ref, aim_ref, ldt_ref, bre_ref, bim_ref, cre_ref, cim_ref,
                    bmat_ref, cmat_ref, ab_ref):
    a_re = are_ref[0]
    a_im = aim_ref[0]
    dt = jnp.exp(ldt_ref[0])
    mag = jnp.exp(a_re * dt)
    ab_re = mag * jnp.cos(a_im * dt)
    ab_im = mag * jnp.sin(a_im * dt)
    den = a_re * a_re + a_im * a_im
    z_re = ((ab_re - 1.0) * a_re + ab_im * a_im) / den
    z_im = (ab_im * a_re - (ab_re - 1.0) * a_im) / den
    b_re = bre_ref[...]
    b_im = bim_ref[...]
    bmat_ref[0, :, 0:S5_NS] = (b_re * z_re - b_im * z_im).astype(BF16)
    bmat_ref[0, :, S5_NS:2 * S5_NS] = (b_re * z_im + b_im * z_re).astype(BF16)
    cmat_ref[0, 0:S5_NS, :] = cre_ref[0].astype(BF16)
    cmat_ref[0, S5_NS:2 * S5_NS, :] = (-cim_ref[0]).astype(BF16)
    ab_ref[0, :, 0:S5_NS] = ab_re
    ab_ref[0, :, S5_NS:2 * S5_NS] = ab_im


def _s5_prep_call(a_re, a_im, log_dt, b_re, b_im, c_re, c_im):
    eye = jnp.eye(S5_GROUPS, dtype=F32)
    b_re_bd = jnp.einsum('gpc,gh->gchp', b_re, eye).reshape(W_GRP, S5_NS)
    b_im_bd = jnp.einsum('gpc,gh->gchp', b_im, eye).reshape(W_GRP, S5_NS)
    c_re_bd = jnp.einsum('rgcp,gh->rgphc', c_re, eye).reshape(2, S5_NS, W_GRP)
    c_im_bd = jnp.einsum('rgcp,gh->rgphc', c_im, eye).reshape(2, S5_NS, W_GRP)
    ldt = jnp.repeat(log_dt, S5_STATE, axis=1).reshape(2, 1, S5_NS)
    row = lambda: pl.BlockSpec((1, 1, S5_NS), lambda r: (r, 0, 0))
    return pl.pallas_call(
        _s5_prep_kernel,
        grid=(2,),
        in_specs=[row(), row(), row(),
                  pl.BlockSpec((W_GRP, S5_NS), lambda r: (0, 0)), pl.BlockSpec((W_GRP, S5_NS), lambda r: (0, 0)),
                  pl.BlockSpec((1, S5_NS, W_GRP), lambda r: (r, 0, 0)),
                  pl.BlockSpec((1, S5_NS, W_GRP), lambda r: (r, 0, 0))],
        out_specs=[pl.BlockSpec((1, W_GRP, 2 * S5_NS), lambda r: (r, 0, 0)),
                   pl.BlockSpec((1, 2 * S5_NS, W_GRP), lambda r: (r, 0, 0)),
                   pl.BlockSpec((1, 1, 2 * S5_NS), lambda r: (r, 0, 0))],
        out_shape=[jax.ShapeDtypeStruct((2, W_GRP, 2 * S5_NS), BF16),
                   jax.ShapeDtypeStruct((2, 2 * S5_NS, W_GRP), BF16),
                   jax.ShapeDtypeStruct((2, 1, 2 * S5_NS), F32)],
        compiler_params=_cparams(1),
        name="s5_prep",
    )(a_re.reshape(2, 1, S5_NS), a_im.reshape(2, 1, S5_NS), ldt, b_re_bd, b_im_bd, c_re_bd, c_im_bd)


def _s5_dir_kernel(bsz, n, reverse, u_ref, h0_ref, bmat_ref, cmat_ref, ab_ref, *rest):
    if reverse:
        y_ref, hs_ref, x_s, hb_s = rest
    else:
        ul_ref, yb_ref, d_ref, gw_ref, gb_ref, y_ref, hs_ref, x_s, hb_s = rest
    step = pl.program_id(0)

    @pl.when(step == 0)
    def _():
        hs_ref[...] = h0_ref[...]
        hb_s[...] = jnp.zeros_like(hb_s)

    u = u_ref[...].reshape(S5_TS * bsz, W_GRP)
    x_s[...] = jnp.dot(u, bmat_ref[0], preferred_element_type=F32)
    y = jnp.dot(hb_s[...], cmat_ref[0], preferred_element_type=F32)
    if not reverse:
        y = (y + yb_ref[...].reshape(S5_TS * bsz, W_GRP)
             + ul_ref[...].reshape(S5_TS * bsz, W_GRP).astype(F32) * d_ref[...])
        y = jax.nn.gelu(y)
        y = y * jax.nn.sigmoid(jnp.dot(y.astype(BF16), gw_ref[...], preferred_element_type=F32)
                               + gb_ref[...])
    y_ref[...] = y.reshape(S5_TS, bsz, W_GRP)

    @pl.when(step < n)
    def _():
        _s5_scan_block(bsz, reverse, ab_ref, hs_ref, x_s, hb_s)


def _s5_scan_block(bsz, reverse, ab_ref, hs_ref, x_s, hb_s):
    for q in range(S5_NS // W_GRP):
        cr = slice(W_GRP * q, W_GRP * (q + 1))
        cim = slice(S5_NS + W_GRP * q, S5_NS + W_GRP * (q + 1))
        a_re = jnp.broadcast_to(ab_ref[0, :, cr], (bsz, W_GRP))
        a_im = jnp.broadcast_to(ab_ref[0, :, cim], (bsz, W_GRP))

        def body(kk, carry, cr=cr, cim=cim, a_re=a_re, a_im=a_im):
            h_re, h_im = carry
            t = (S5_TS - 1 - kk) if reverse else kk
            rows = pl.ds(pl.multiple_of(t * bsz, bsz), bsz)
            n_re = a_re * h_re - a_im * h_im + x_s[rows, cr]
            n_im = a_re * h_im + a_im * h_re + x_s[rows, cim]
            hb_s[rows, cr] = n_re.astype(BF16)
            hb_s[rows, cim] = n_im.astype(BF16)
            return n_re, n_im

        h_re, h_im = lax.fori_loop(0, S5_TS, body, (hs_ref[:, cr], hs_ref[:, cim]), unroll=4)
        hs_ref[:, cr] = h_re
        hs_ref[:, cim] = h_im


def _s5_dir_call(r, u3, h0, bmat, cmat, ab, post=None):
    L, bsz, _ = u3.shape
    n = L // S5_TS
    blk = (S5_TS, bsz, W_GRP)
    reverse = r == 1

    def block_of(c):
        return n - 1 - c if reverse else c

    cur = lambda i: (block_of(jnp.minimum(i, n - 1)), 0, 0)
    lag = lambda i: (block_of(jnp.maximum(i - 1, 0)), 0, 0)
    state = pl.BlockSpec((bsz, 2 * S5_NS), lambda i: (0, 0))
    in_specs = [pl.BlockSpec(blk, cur), state,
                pl.BlockSpec((1, W_GRP, 2 * S5_NS), lambda i: (r, 0, 0)),
                pl.BlockSpec((1, 2 * S5_NS, W_GRP), lambda i: (r, 0, 0)),
                pl.BlockSpec((1, 1, 2 * S5_NS), lambda i: (r, 0, 0))]
    args = [u3, h0, bmat, cmat, ab]
    if not reverse:
        yb, d, glu_w, glu_b = post
        in_specs += [pl.BlockSpec(blk, lag), pl.BlockSpec(blk, lag),
                     pl.BlockSpec((1, W_GRP), lambda i: (0, 0)),
                     pl.BlockSpec((W_GRP, W_GRP), lambda i: (0, 0)),
                     pl.BlockSpec((1, W_GRP), lambda i: (0, 0))]
        args += [u3, yb, d.reshape(1, -1), glu_w.astype(BF16), glu_b.reshape(1, -1)]
    return pl.pallas_call(
        functools.partial(_s5_dir_kernel, bsz, n, reverse),
        grid=(n + 1,),
        in_specs=in_specs,
        out_specs=[pl.BlockSpec(blk, lag), state],
        out_shape=[jax.ShapeDtypeStruct((L, bsz, W_GRP), F32),
                   jax.ShapeDtypeStruct((bsz, 2 * S5_NS), F32)],
        scratch_shapes=[pltpu.VMEM((S5_TS * bsz, 2 * S5_NS), F32),
                        pltpu.VMEM((S5_TS * bsz, 2 * S5_NS), BF16)],
        compiler_params=_cparams(1),
        name="s5_bwd" if reverse else "s5_fwd",
    )(*args)


def _s5_mixer(u5c, u5l, bsz, p):
    bmat, cmat, ab = _s5_prep_call(p['s5_a_re'], p['s5_a_im'], p['s5_log_dt'], p['s5_b_re'],
                                   p['s5_b_im'], p['s5_c_re'], p['s5_c_im'])
    Lc, L = u5c.shape[0], u5l.shape[0]
    u3c = u5c.reshape(Lc, bsz, W_GRP)
    u3l = u5l.reshape(L, bsz, W_GRP)
    h0 = jnp.zeros((bsz, 2 * S5_NS), F32)
    head = (p['s5_d'], p['s5_glu_w'], p['s5_glu_b'])
    ybc, hbc = _s5_dir_call(1, u3c, h0, bmat, cmat, ab)
    ybl, _ = _s5_dir_call(1, u3l, hbc, bmat, cmat, ab)
    s5c, hfc = _s5_dir_call(0, u3c, h0, bmat, cmat, ab, post=(ybc,) + head)
    s5l, _ = _s5_dir_call(0, u3l, hfc, bmat, cmat, ab, post=(ybl,) + head)
    return s5c.reshape(Lc, bsz * W_GRP), s5l.reshape(L, bsz * W_GRP)


@functools.lru_cache(maxsize=None)
def _dft_tables(L):
    k = np.arange(L, dtype=np.int64)
    ft = (k[:, None] * k[None, :]) % (2 * L)
    ang = ft.astype(np.float64) * (math.pi / L)
    return np.cos(ang).astype(np.float32), np.sin(ang).astype(np.float32)


@functools.lru_cache(maxsize=None)
def _hyena_consts(L):
    t = np.linspace(0.0, 1.0, L, dtype=np.float32)[:, None]
    w = (2.0 * math.pi * np.arange(L, dtype=np.float32)[:, None] / L).astype(np.float32)
    bands = np.linspace(1e-4, HY_BANDS - 1, HY_BANDS, dtype=np.float32)[None, :]
    bw = (bands * w).astype(np.float32).astype(np.float64)
    feats = np.zeros((L, 128), np.float32)
    feats[:, 0:1] = t
    feats[:, 1:1 + HY_BANDS] = np.cos(bw)
    feats[:, 1 + HY_BANDS:HY_EMB] = -np.sin(bw)
    max_decay = math.log(1e-2) / 0.3
    min_decay = math.log(1e-2) / 1.5
    deltas = np.abs(np.linspace(min_decay, max_decay, 4 * W_GRP, dtype=np.float32))[None, :]
    return feats, deltas.astype(np.float32)


HY_RB = 256


def _hy_filter_kernel(feats_ref, w1_ref, b1_ref, fr_ref, w2_ref, b2_ref, w3_ref, del_ref,
                      p_ref, q_ref, nrm_ref, an_ref):
    i = pl.program_id(0)
    feats = feats_ref[...]
    fr = fr_ref[...]
    h = jnp.sin(fr * (_dot_f32(feats, w1_ref[...]) + b1_ref[...]))
    h = jnp.sin(fr * (_dot_f32(h, w2_ref[...]) + b2_ref[...]))
    h = _dot_f32(h, w3_ref[...])
    h = h * jnp.exp(-feats[:, 0:1] * del_ref[...])
    half = 2 * W_GRP
    hf = h[:, 0:half]
    row = lax.broadcasted_iota(jnp.int32, (HY_RB, half), 0) + i * HY_RB
    hb = jnp.where(row == 0, 0.0, h[:, half:2 * half])
    p = hf + hb
    sign = (1 - 2 * (row % 2)).astype(F32)
    p_ref[...] = p.astype(BF16)
    q_ref[...] = (hb - hf).astype(BF16)

    @pl.when(i == 0)
    def _():
        nrm_ref[...] = jnp.full_like(nrm_ref, EPS)
        an_ref[...] = jnp.zeros_like(an_ref)

    nrm_ref[...] += (jnp.sum(jnp.abs(hf), axis=0, keepdims=True)
                     + jnp.sum(jnp.abs(hb), axis=0, keepdims=True))
    an_ref[...] += jnp.sum(p * sign, axis=0, keepdims=True)


def _hy_spectrum_kernel(L, c_ref, s_ref, p_ref, q_ref, nrm_ref, ans_ref, a_ref, bc_ref, an_ref):
    i = pl.program_id(0)
    n = 2.0 * L
    inv = 1.0 / nrm_ref[...]
    row = lax.broadcasted_iota(jnp.int32, a_ref.shape, 0) + i * HY_RB
    wv = jnp.where(row == 0, 1.0 / n, 2.0 / n) * inv
    a_ref[...] = wv * jnp.dot(c_ref[...], p_ref[...], preferred_element_type=F32)
    bc_ref[...] = wv * jnp.dot(s_ref[...], q_ref[...], preferred_element_type=F32)
    an_ref[...] = ans_ref[...] * inv / n


def _hy_filter_call(L, w1, b1, freq, w2, b2, w3, cmat, smat):
    feats, deltas = _hyena_consts(L)
    w1p = jnp.pad(w1, ((0, 128 - HY_EMB), (0, 0)))
    half = 2 * W_GRP
    nb = L // HY_RB
    full = lambda a: pl.BlockSpec(a.shape, lambda i, nd=a.ndim: (0,) * nd)
    small = (w1p, b1.reshape(1, -1), freq.reshape(1, -1), w2, b2.reshape(1, -1), w3, jnp.asarray(deltas))
    rowblk = lambda w: pl.BlockSpec((HY_RB, w), lambda i: (i, 0))
    vec = pl.BlockSpec((1, half), lambda i: (0, 0))
    p, q, nrm, ans = pl.pallas_call(
        _hy_filter_kernel,
        grid=(nb,),
        in_specs=[rowblk(128)] + [full(a) for a in small],
        out_specs=[rowblk(half), rowblk(half), vec, vec],
        out_shape=[jax.ShapeDtypeStruct((L, half), BF16), jax.ShapeDtypeStruct((L, half), BF16),
                   jax.ShapeDtypeStruct((1, half), F32), jax.ShapeDtypeStruct((1, half), F32)],
        compiler_params=_cparams(1),
        name="hyena_filter",
    )(jnp.asarray(feats), *small)
    return pl.pallas_call(
        functools.partial(_hy_spectrum_kernel, L),
        grid=(nb,),
        in_specs=[rowblk(L), rowblk(L), full(p), full(q), vec, vec],
        out_specs=[rowblk(half), rowblk(half), vec],
        out_shape=[jax.ShapeDtypeStruct((L, half), F32), jax.ShapeDtypeStruct((L, half), F32),
                   jax.ShapeDtypeStruct((1, half), F32)],
        compiler_params=_cparams(1),
        name="hyena_spectrum",
    )(cmat, smat, p, q, nrm, ans)


def _alt_sign(shape):
    return (1 - 2 * (lax.broadcasted_iota(jnp.int32, shape, 0) % 2)).astype(F32)


def _reverse_shift(x, j_ref):
    hi = x.astype(BF16)
    lo = (x - hi.astype(F32)).astype(BF16)
    j = j_ref[...]
    return (jnp.dot(j, hi, preferred_element_type=F32) + jnp.dot(j, lo, preferred_element_type=F32))


def _hy_conv_kernel(L, u_ref, cw_ref, cb_ref, ce_ref, se_ref, co_ref, so_ref, cot_ref, sot_ref, j_ref,
                    a_ref, bc_ref, an_ref, bias_ref, o_ref,
                    x1_s, x2_s, z_s, zs_s, zd_s, xe_s, ye_s, xo_s, yo_s, d_s, acc_s):
    nc = L // T
    H = L // 2
    fb = min(256, H)
    nfb = H // fb
    rb = min(512, H)
    nrb = H // rb

    def conv(c, carry):
        s = pl.multiple_of(c * T, T)
        rows = pl.ds(s, T)
        prev, nxt = _halo_rows(u_ref, s, c, nc, L)
        y = _dwconv_chunk(u_ref[0, rows, :].astype(F32), prev, nxt, cw_ref, cb_ref)
        x1_s[rows, :] = y[:, 0:W_GRP]
        x2_s[rows, :] = y[:, W_GRP:2 * W_GRP]
        z_s[rows, :] = y[:, 2 * W_GRP:3 * W_GRP]
        return carry

    lax.fori_loop(0, nc, conv, 0)
    row0 = lax.broadcasted_iota(jnp.int32, (fb, W_GRP), 0) == 0

    for o, gate_s in enumerate((x1_s, x2_s)):
        cols = slice(W_GRP * o, W_GRP * (o + 1))
        acc_s[...] = jnp.zeros_like(acc_s)

        def fold(b, carry):
            top_rows = pl.ds(pl.multiple_of(b * fb, fb), fb)
            top = z_s[top_rows, :]
            w = z_s[pl.ds(pl.multiple_of(H + (nfb - 1 - b) * fb, fb), fb), :]
            edge = z_s[pl.ds(pl.multiple_of(jnp.minimum(H + (nfb - b) * fb, L - 8), 8), 8), :][0:1, :]
            zr = jnp.where(row0, jnp.where(b > 0, edge, 0.0), _reverse_shift(w, j_ref))
            zs_s[top_rows, :] = (top + zr).astype(BF16)
            zd_s[top_rows, :] = (top - zr).astype(BF16)
            acc_s[0:1, :] += jnp.sum((top + w) * _alt_sign(top.shape), axis=0, keepdims=True)
            return carry

        lax.fori_loop(0, nfb, fold, 0)
        z_mid = z_s[H:H + 1, :]

        def fwd(j, carry, cols=cols, z_mid=z_mid):
            rows = pl.ds(pl.multiple_of(j * rb, rb), rb)
            odd_rows = pl.ds(pl.multiple_of(H + j * rb, rb), rb)
            zs = zs_s[...]
            zd = zd_s[...]
            mid = _alt_sign((rb, W_GRP)) * z_mid
            pe = jnp.dot(ce_ref[rows, :], zs, preferred_element_type=F32) + mid
            qo = jnp.dot(so_ref[rows, :], zs, preferred_element_type=F32) + mid
            po = jnp.dot(co_ref[rows, :], zd, preferred_element_type=F32)
            qe = jnp.dot(se_ref[rows, :], zd, preferred_element_type=F32)
            ae = a_ref[rows, cols]
            bce = bc_ref[rows, cols]
            ao = a_ref[odd_rows, cols]
            bco = bc_ref[odd_rows, cols]
            xe = pe * ae + qe * bce
            yo = qo * ao - po * bco
            xe_s[rows, :] = xe.astype(BF16)
            ye_s[rows, :] = (qe * ae - pe * bce).astype(BF16)
            xo_s[rows, :] = (po * ao + qo * bco).astype(BF16)
            yo_s[rows, :] = yo.astype(BF16)
            acc_s[1:2, :] += jnp.sum((xe + yo) * _alt_sign(xe.shape), axis=0, keepdims=True)
            return carry

        lax.fori_loop(0, nrb, fwd, 0)
        nyq = acc_s[0:1, :] * an_ref[:, cols]

        def finish(rows, y, cols=cols, gate_s=gate_s, nyq=nyq, o=o):
            res = gate_s[rows, :] * (y + _alt_sign(y.shape) * nyq + z_s[rows, :] * bias_ref[:, cols])
            if o == 0:
                z_s[rows, :] = res
            else:
                o_ref[0, rows, :] = res

        def inv(j, carry, finish=finish):
            rows = pl.ds(pl.multiple_of(j * rb, rb), rb)
            u1 = (jnp.dot(ce_ref[rows, :], xe_s[...], preferred_element_type=F32)
                  + jnp.dot(sot_ref[rows, :], yo_s[...], preferred_element_type=F32))
            u2 = (jnp.dot(cot_ref[rows, :], xo_s[...], preferred_element_type=F32)
                  + jnp.dot(se_ref[rows, :], ye_s[...], preferred_element_type=F32))
            d_s[rows, :] = u1 - u2
            finish(rows, u1 + u2)
            return carry

        lax.fori_loop(0, nrb, inv, 0)
        y_mid = acc_s[1:2, :]

        def second(b, carry, finish=finish, y_mid=y_mid):
            rows = pl.ds(pl.multiple_of(H + b * fb, fb), fb)
            dblk = d_s[pl.ds(pl.multiple_of((nfb - 1 - b) * fb, fb), fb), :]
            edge = d_s[pl.ds(pl.multiple_of(jnp.minimum((nfb - b) * fb, H - 8), 8), 8), :][0:1, :]
            y = jnp.where(row0, jnp.where(b > 0, edge, y_mid), _reverse_shift(dblk, j_ref))
            finish(rows, y)
            return carry

        lax.fori_loop(0, nfb, second, 0)


@functools.lru_cache(maxsize=None)
def _dft_half_tables(L):
    c, s = _dft_tables(L)
    H = L // 2
    fb = min(256, H)
    ce, co = c[0::2, :H], c[1::2, :H]
    se, so = s[0::2, :H], s[1::2, :H]
    j = np.zeros((fb, fb), np.float32)
    for r in range(1, fb):
        j[r, fb - r] = 1.0
    return tuple(np.ascontiguousarray(m) for m in (ce, se, co, so, co.T, so.T, j))


def _hy_conv_call(u, conv_w, conv_b, tables, a, bc, an, bias):
    bsz, L, _ = u.shape
    H = L // 2
    half = 2 * W_GRP
    mats = [jnp.asarray(m).astype(BF16) for m in tables]
    return pl.pallas_call(
        functools.partial(_hy_conv_kernel, L),
        grid=(bsz,),
        in_specs=[pl.BlockSpec((1, L, HY_COLS), lambda b: (b, 0, 0)),
                  pl.BlockSpec((3, HY_COLS), lambda b: (0, 0)),
                  pl.BlockSpec((1, HY_COLS), lambda b: (0, 0))]
                 + [_const_spec(m.shape) for m in mats]
                 + [_const_spec((L, half)), _const_spec((L, half)),
                    pl.BlockSpec((1, half), lambda b: (0, 0)),
                    pl.BlockSpec((1, half), lambda b: (0, 0))],
        out_specs=pl.BlockSpec((1, L, W_GRP), lambda b: (b, 0, 0)),
        out_shape=jax.ShapeDtypeStruct((bsz, L, W_GRP), F32),
        scratch_shapes=[pltpu.VMEM((L, W_GRP), F32), pltpu.VMEM((L, W_GRP), F32),
                        pltpu.VMEM((L, W_GRP), F32),
                        pltpu.VMEM((H, W_GRP), BF16), pltpu.VMEM((H, W_GRP), BF16),
                        pltpu.VMEM((H, W_GRP), BF16), pltpu.VMEM((H, W_GRP), BF16),
                        pltpu.VMEM((H, W_GRP), BF16), pltpu.VMEM((H, W_GRP), BF16),
                        pltpu.VMEM((H, W_GRP), F32), pltpu.VMEM((8, W_GRP), F32)],
        compiler_params=_cparams(1),
        name="hyena_conv",
    )(u, conv_w, conv_b.reshape(1, -1), *mats, a, bc, an, bias.reshape(1, -1))


def _hyena_mixer(u, p):
    L = u.shape[1]
    cnp, snp = _dft_tables(L)
    cmat = jnp.asarray(cnp).astype(BF16)
    smat = jnp.asarray(snp).astype(BF16)
    a, bc, an = _hy_filter_call(L, p['hy_w1'], p['hy_b1'], p['hy_freq'], p['hy_w2'], p['hy_b2'],
                                p['hy_w3'], cmat, smat)
    a = jnp.concatenate([a[0::2], a[1::2]], axis=0)
    bc = jnp.concatenate([bc[0::2], bc[1::2]], axis=0)
    return _hy_conv_call(u, p['hy_conv_w'], p['hy_conv_b'], _dft_half_tables(L), a, bc, an, p['hy_bias'])


def _out_ffn_kernel(final, h_ref, ssd_ref, hy_ref, ret_ref, s5_ref, mod_ref, g2_ref, wo_ref,
                    wup_ref, wdn_ref, fg_ref, o_ref, acc_s):
    y = jnp.zeros(h_ref.shape[1:], F32)
    for j, blk in enumerate((ssd_ref[0], hy_ref[0], ret_ref[0], s5_ref[...])):
        y = y + jnp.dot(blk.astype(BF16), wo_ref[W_GRP * j:W_GRP * (j + 1), :],
                        preferred_element_type=F32)
    h1 = h_ref[0] + mod_ref[0, 2:3, :] * y
    xn = h1 * lax.rsqrt(jnp.mean(h1 * h1, axis=-1, keepdims=True) + EPS) * g2_ref[...]
    xm = (xn * (1.0 + mod_ref[0, 4:5, :]) + mod_ref[0, 3:4, :]).astype(BF16)
    fc = 256
    for j in range(D_FF // fc):
        gg = jnp.dot(xm, wup_ref[:, fc * j:fc * (j + 1)], preferred_element_type=F32)
        uu = jnp.dot(xm, wup_ref[:, D_FF + fc * j:D_FF + fc * (j + 1)], preferred_element_type=F32)
        part = jnp.dot((_silu(gg) * uu).astype(BF16), wdn_ref[fc * j:fc * (j + 1), :],
                       preferred_element_type=F32)
        if j == 0:
            acc_s[...] = part
        else:
            acc_s[...] += part
    h2 = h1 + mod_ref[0, 5:6, :] * acc_s[...]
    if final:
        h2 = h2 * lax.rsqrt(jnp.mean(h2 * h2, axis=-1, keepdims=True) + EPS) * fg_ref[...]
    o_ref[0] = h2


def _out_ffn_call(h, mix, mods, ctx_stream, g2, wo, wup, wdn, final_g, final):
    bsz, L, _ = h.shape
    tm = min(512, L)
    mod_map = (lambda b, i: (bsz, 0, 0)) if ctx_stream else (lambda b, i: (b, 0, 0))
    tok = lambda w: pl.BlockSpec((1, tm, w), lambda b, i: (b, i, 0))
    return pl.pallas_call(
        functools.partial(_out_ffn_kernel, final),
        grid=(bsz, L // tm),
        in_specs=[tok(D_MODEL), tok(W_GRP), tok(W_GRP), tok(W_GRP),
                  pl.BlockSpec((tm, W_GRP), lambda b, i: (i, b)),
                  pl.BlockSpec((1, 6, D_MODEL), mod_map),
                  pl.BlockSpec((1, D_MODEL), lambda b, i: (0, 0)),
                  _const_spec((D_MODEL, D_MODEL)),
                  _const_spec((D_MODEL, 2 * D_FF)),
                  _const_spec((D_FF, D_MODEL)),
                  pl.BlockSpec((1, D_MODEL), lambda b, i: (0, 0))],
        out_specs=tok(D_MODEL),
        out_shape=jax.ShapeDtypeStruct((bsz, L, D_MODEL), F32),
        scratch_shapes=[pltpu.VMEM((tm, D_MODEL), F32)],
        compiler_params=_cparams(2),
        name="out_ffn",
    )(h, *mix, mods, g2, wo, wup, wdn, final_g)


def kernel(x, c, ctx, c_ctx, mod_w, mod_b, norm1_g, norm2_g, w_in, w_out, ssd_conv_w, ssd_conv_b, ssd_a_log, ssd_dt_bias, ssd_d, ssd_norm_g, hy_conv_w, hy_conv_b, hy_w1, hy_b1, hy_freq, hy_w2, hy_b2, hy_w3, hy_bias, ret_decay, s5_a_re, s5_a_im, s5_log_dt, s5_b_re, s5_b_im, s5_c_re, s5_c_im, s5_d, s5_glu_w, s5_glu_b, ffn_w_up, ffn_w_down, final_norm_g):
    bsz = x.shape[0]
    depth = mod_w.shape[0]
    sc = jnp.concatenate([c, c_ctx[None, :], jnp.zeros((MOD_ROWS - bsz - 1, D_MODEL), F32)], axis=0)
    mods_all = _mod_call(sc, mod_w, mod_b).reshape(depth, MOD_ROWS, 6, D_MODEL)
    fg = final_norm_g.reshape(1, -1)
    h_l, h_c = x, ctx
    o_xbc, o_dt, o_hy = W_GRP, W_GRP + SSD_XBC, W_GRP + SSD_XBC + 2 * N_HEADS
    o_ret = o_hy + HY_COLS
    o_s5 = o_ret + RET_COLS
    for i in range(depth):
        last = i == depth - 1
        wi = w_in[i]
        wdt = wi[:, o_dt:o_hy]
        wcat = jnp.concatenate([wi[:, 0:o_dt], wi[:, o_hy:o_s5 + W_GRP], wdt,
                                jnp.zeros((D_MODEL, 128 - 2 * N_HEADS), F32)], axis=1).astype(BF16)
        mods = mods_all[i]
        g1 = norm1_g[i].reshape(1, -1)
        zl, xbcl, hyl, retl, dtl, s5l = _inproj_call(h_l, g1, mods, False, wcat)
        zc, xbcc, hyc, retc, dtc, s5c = _inproj_call(h_c, g1, mods, True, wcat)
        ssd_c, ssd_l = _ssd_call((zc, xbcc, dtc), (zl, xbcl, dtl), ssd_conv_w[i],
                                 ssd_conv_b[i], ssd_a_log[i], ssd_dt_bias[i], ssd_d[i], ssd_norm_g[i])
        ret_c, ret_l = _ret_call(retc, retl, ret_decay[i])
        p = dict(s5_a_re=s5_a_re[i], s5_a_im=s5_a_im[i], s5_log_dt=s5_log_dt[i], s5_b_re=s5_b_re[i],
                 s5_b_im=s5_b_im[i], s5_c_re=s5_c_re[i], s5_c_im=s5_c_im[i], s5_d=s5_d[i],
                 s5_glu_w=s5_glu_w[i], s5_glu_b=s5_glu_b[i],
                 hy_conv_w=hy_conv_w[i], hy_conv_b=hy_conv_b[i], hy_w1=hy_w1[i], hy_b1=hy_b1[i],
                 hy_freq=hy_freq[i], hy_w2=hy_w2[i], hy_b2=hy_b2[i], hy_w3=hy_w3[i], hy_bias=hy_bias[i])
        s5_c, s5_l = _s5_mixer(s5c, s5l, bsz, p)
        hy_l = _hyena_mixer(hyl, p)
        g2 = norm2_g[i].reshape(1, -1)
        wo = w_out[i].astype(BF16)
        wup = ffn_w_up[i].astype(BF16)
        wdn = ffn_w_down[i].astype(BF16)
        h_l = _out_ffn_call(h_l, (ssd_l, hy_l, ret_l, s5_l), mods, False, g2, wo, wup, wdn, fg, last)
        if not last:
            hy_c = _hyena_mixer(hyc, p)
            h_c = _out_ffn_call(h_c, (ssd_c, hy_c, ret_c, s5_c), mods, True, g2, wo, wup, wdn, fg, False)
    return h_l
```

```python
import functools
import math

import numpy as np
import jax
import jax.numpy as jnp
from jax import lax
from jax.experimental import pallas as pl
from jax.experimental.pallas import tpu as pltpu

F32 = jnp.float32
BF16 = jnp.bfloat16
EPS = 1e-6

D_MODEL = 1024
W_GRP = 256
T = 128
N_HEADS = 4
HEAD_DIM = 64
SSD_STATE = 64
SSD_XBC = W_GRP + 2 * SSD_STATE
HY_COLS = 3 * W_GRP
RET_COLS = 4 * W_GRP
GRID_W = 64
ROPE_BASE = 10000.0
HY_EMB = 33
HY_BANDS = 16
HY_FILT = 64
S5_GROUPS = 16
S5_CH = 16
S5_STATE = 64
S5_NS = S5_GROUPS * S5_STATE
D_FF = 2816
P3_GROUP = 4
S5_TS = 64
MOD_ROWS = 24

VMEM_LIMIT = 56 * 1024 * 1024


def _cparams(n_grid):
    return pltpu.CompilerParams(dimension_semantics=("arbitrary",) * n_grid,
                                vmem_limit_bytes=VMEM_LIMIT)


def _dot(a, b):
    return jnp.dot(a.astype(BF16), b.astype(BF16), preferred_element_type=F32)


def _dot_nt(a, b):
    return lax.dot_general(a.astype(BF16), b.astype(BF16), (((1,), (1,)), ((), ())),
                           preferred_element_type=F32)


def _dot_tn(a, b):
    return lax.dot_general(a.astype(BF16), b.astype(BF16), (((0,), (0,)), ((), ())),
                           preferred_element_type=F32)


def _dot_f32(a, b):
    return jnp.dot(a, b, preferred_element_type=F32, precision=lax.Precision.HIGHEST)


def _silu(x):
    return x * jax.nn.sigmoid(x)


def _const_spec(shape):
    nd = len(shape)
    return pl.BlockSpec(shape, lambda *_: (0,) * nd, pipeline_mode=pl.Buffered(1))


def _mod_kernel(sc_ref, w_ref, b_ref, o_ref):
    s = _silu(sc_ref[...])
    o_ref[0] = _dot_f32(s, w_ref[0]) + b_ref[0]


def _mod_call(sc, mod_w, mod_b):
    depth, _, n = mod_w.shape
    tn = 1536
    return pl.pallas_call(
        _mod_kernel,
        grid=(depth, n // tn),
        in_specs=[pl.BlockSpec((MOD_ROWS, D_MODEL), lambda l, j: (0, 0)),
                  pl.BlockSpec((1, D_MODEL, tn), lambda l, j: (l, 0, j)),
                  pl.BlockSpec((1, 1, tn), lambda l, j: (l, 0, j))],
        out_specs=pl.BlockSpec((1, MOD_ROWS, tn), lambda l, j: (l, 0, j)),
        out_shape=jax.ShapeDtypeStruct((depth, MOD_ROWS, n), F32),
        compiler_params=_cparams(2),
        name="adaln_mod",
    )(sc, mod_w, mod_b.reshape(depth, 1, n))


def _inproj_kernel(x_ref, g_ref, mod_ref, w_ref, z_ref, xbc_ref, hy_ref, ret_ref, dt_ref, s5_ref):
    x = x_ref[0]
    xn = x * lax.rsqrt(jnp.mean(x * x, axis=-1, keepdims=True) + EPS) * g_ref[...]
    xm = (xn * (1.0 + mod_ref[0, 1:2, :]) + mod_ref[0, 0:1, :]).astype(BF16)
    o = 0
    for ref, width in ((z_ref, W_GRP), (xbc_ref, SSD_XBC), (hy_ref, HY_COLS),
                       (ret_ref, RET_COLS)):
        ref[0] = jnp.dot(xm, w_ref[:, o:o + width], preferred_element_type=F32).astype(BF16)
        o += width
    s5_ref[...] = jnp.dot(xm, w_ref[:, o:o + W_GRP], preferred_element_type=F32).astype(BF16)
    o += W_GRP
    dt_ref[0] = jnp.dot(xm, w_ref[:, o:o + 128], preferred_element_type=F32)


def _inproj_call(h, g, mods, ctx_stream, wcat):
    bsz, L, _ = h.shape
    tm = min(512, L)
    ncols = wcat.shape[1]
    mod_map = (lambda b, i: (bsz, 0, 0)) if ctx_stream else (lambda b, i: (b, 0, 0))
    tok = lambda w: pl.BlockSpec((1, tm, w), lambda b, i: (b, i, 0))
    out_shape = [jax.ShapeDtypeStruct((bsz, L, W_GRP), BF16),
                 jax.ShapeDtypeStruct((bsz, L, SSD_XBC), BF16),
                 jax.ShapeDtypeStruct((bsz, L, HY_COLS), BF16),
                 jax.ShapeDtypeStruct((bsz, L, RET_COLS), BF16),
                 jax.ShapeDtypeStruct((bsz, L, 128), F32),
                 jax.ShapeDtypeStruct((L, bsz * W_GRP), BF16)]
    out_specs = [tok(W_GRP), tok(SSD_XBC), tok(HY_COLS), tok(RET_COLS), tok(128),
                 pl.BlockSpec((tm, W_GRP), lambda b, i: (i, b))]
    return pl.pallas_call(
        _inproj_kernel,
        grid=(bsz, L // tm),
        in_specs=[tok(D_MODEL),
                  pl.BlockSpec((1, D_MODEL), lambda b, i: (0, 0)),
                  pl.BlockSpec((1, 6, D_MODEL), mod_map),
                  _const_spec((D_MODEL, ncols))],
        out_specs=out_specs,
        out_shape=out_shape,
        compiler_params=_cparams(2),
        name="in_proj",
    )(h, g, mods, wcat)


def _halo_rows(ref, s, c, nc, L):
    sp = pl.multiple_of(jnp.maximum(s - 16, 0), 16)
    prev = ref[0, pl.ds(sp, 16), :][15:16, :].astype(F32)
    prev = jnp.where(c > 0, prev, 0.0)
    sn = pl.multiple_of(jnp.minimum(s + T, L - 16), 16)
    nxt = ref[0, pl.ds(sn, 16), :][0:1, :].astype(F32)
    nxt = jnp.where(c < nc - 1, nxt, 0.0)
    return prev, nxt


def _dwconv_chunk(x, prev, nxt, w_ref, b_ref):
    n = x.shape[0]
    row = lax.broadcasted_iota(jnp.int32, x.shape, 0)
    up = jnp.where(row == 0, prev, pltpu.roll(x, 1, 0))
    dn = jnp.where(row == n - 1, nxt, pltpu.roll(x, n - 1, 0))
    return up * w_ref[0:1, :] + x * w_ref[1:2, :] + dn * w_ref[2:3, :] + b_ref[...]


def _cumsum(x, axis):
    n = x.shape[axis]
    idx = lax.broadcasted_iota(jnp.int32, x.shape, axis)
    s = 1
    while s < n:
        x = x + jnp.where(idx >= s, pltpu.roll(x, s, axis), 0.0)
        s *= 2
    return x


def _expand_heads(c, exp_ref):
    hi = c.astype(BF16)
    lo = (c - hi.astype(F32)).astype(BF16)
    return jnp.dot(jnp.concatenate([hi, lo], axis=1), exp_ref[...], preferred_element_type=F32)


def _state_recurrence(hf_ref, hb_ref, decf_ref, decb_ref, nc, h0f, h0b):
    hf_ref[0] = h0f
    hb_ref[nc] = h0b

    def fwd(c, carry):
        hf_ref[c + 1] = decf_ref[c] * hf_ref[c] + hf_ref[c + 1]
        return carry

    lax.fori_loop(0, nc, fwd, 0)

    def bwd(k, carry):
        c = nc - 1 - k
        hb_ref[c] = decb_ref[c] * hb_ref[c + 1] + hb_ref[c]
        return carry

    lax.fori_loop(0, nc, bwd, 0)
    return hf_ref[nc], hb_ref[0]


def _split3(x):
    hi = x.astype(BF16)
    r = x - hi.astype(F32)
    mid = r.astype(BF16)
    lo = (r - mid.astype(F32)).astype(BF16)
    return jnp.concatenate([hi, mid, lo], axis=1)


def _tile_heads_bd(x, bdmask):
    xb = x.astype(BF16)
    return jnp.where(bdmask, jnp.concatenate([xb] * N_HEADS, axis=0), 0)


def _rows_to_lanes(a, lo):
    return jnp.concatenate([a[lo + h:lo + h + 1, :] for h in range(N_HEADS)], axis=1)


def _ssd_sequence(L, z_ref, xbc_ref, dt_ref, y_ref, prm, scr, h0f, h0b):
    (cw_ref, cb_ref, alog_row, bias_row, dskip_ref, ng_ref, exp_ref, exp128_ref, sel_ref) = prm
    (xs_s, bc_s, ee_s, acs_s, dt_s, hf_s, hb_s, decf_s, decb_s) = scr
    nc = L // T
    lane = lax.broadcasted_iota(jnp.int32, (T, 128), 1)
    a_row = -jnp.exp(alog_row[...])

    grp = min(P3_GROUP, nc)

    def phase1(p, carry):
        cs = [p * grp + j for j in range(grp)]
        rows = [pl.ds(pl.multiple_of(c * T, T), T) for c in cs]
        st = []
        for c, r in zip(cs, rows):
            prev, nxt = _halo_rows(xbc_ref, pl.multiple_of(c * T, T), c, nc, L)
            xact = _silu(_dwconv_chunk(xbc_ref[0, r, :].astype(F32), prev, nxt, cw_ref, cb_ref))
            xs_s[r, :] = xact[:, 0:W_GRP]
            bc_s[r, :] = xact[:, W_GRP:SSD_XBC]
            dt = jax.nn.softplus(dt_ref[0, r, :] + bias_row[...])
            dt_s[r, :] = dt
            la = dt * a_row
            acs_f = _cumsum(la, 0)
            tot = acs_f[T - 1:T, :]
            acs = jnp.where(lane < N_HEADS, acs_f, tot - acs_f + la)
            acs_s[r, :] = acs
            st.append((xact, _expand_heads(jnp.exp(acs), exp_ref),
                       _expand_heads(dt * jnp.exp(tot - acs), exp_ref)))
        for c, r, (xact, ee, wx) in zip(cs, rows, st):
            xs = xact[:, 0:W_GRP]
            bm = xact[:, W_GRP:W_GRP + SSD_STATE]
            ee_s[r, :] = ee
            hf_s[c + 1] = _dot_tn(bm, xs * wx[:, 0:W_GRP])
            hb_s[c] = _dot_tn(bm, xs * wx[:, W_GRP:2 * W_GRP])
            decf_s[c] = ee[T - 1:T, 0:W_GRP]
            decb_s[c] = ee[0:1, W_GRP:2 * W_GRP]
        return carry

    lax.fori_loop(0, nc // grp, phase1, 0)
    hf_fin, hb_fin = _state_recurrence(hf_s, hb_s, decf_s, decb_s, nc, h0f, h0b)

    ri = lax.broadcasted_iota(jnp.int32, (T, N_HEADS * T), 0)
    ci = lax.broadcasted_iota(jnp.int32, (T, N_HEADS * T), 1) % T
    strict_lower = ci < ri
    diag = ci == ri
    r4 = lax.broadcasted_iota(jnp.int32, (N_HEADS * T, W_GRP), 0) // T
    c4 = lax.broadcasted_iota(jnp.int32, (N_HEADS * T, W_GRP), 1) // HEAD_DIM
    bdmask = r4 == c4
    nl = N_HEADS * T

    nt = (((1,), (1,)), ((), ()))

    def phase3(p, carry):
        cs = [p * grp + j for j in range(grp)]
        rows = [pl.ds(pl.multiple_of(c * T, T), T) for c in cs]
        st = []
        for c, r in zip(cs, rows):
            bc = bc_s[r, :]
            bm = bc[:, 0:SSD_STATE]
            cm = bc[:, SSD_STATE:2 * SSD_STATE]
            a3 = _split3(acs_s[r, :])
            col = jnp.dot(a3, exp128_ref[...], preferred_element_type=F32)
            acst = lax.dot_general(sel_ref[...], a3, nt, preferred_element_type=F32)
            dtt = lax.dot_general(sel_ref[...], _split3(dt_s[r, :]), nt,
                                  preferred_element_type=F32)
            inter = (_dot(cm, hf_s[c]), _dot(cm, hb_s[c + 1]))
            st.append((col, acst, dtt, _dot_nt(cm, bm), inter))
        ys = []
        for r, (col, acst, dtt, g, inter) in zip(rows, st):
            g4 = jnp.concatenate([g] * N_HEADS, axis=1)
            shifted = acst - jnp.log(dtt)
            arg = jnp.where(strict_lower, col[:, 0:nl] - _rows_to_lanes(shifted, 0),
                            col[:, nl:2 * nl] - _rows_to_lanes(shifted, N_HEADS))
            w = g4 * (jnp.exp(arg) + jnp.where(diag, _rows_to_lanes(dtt, 0), 0.0))
            ys.append(jnp.dot(w.astype(BF16), _tile_heads_bd(xs_s[r, :], bdmask),
                              preferred_element_type=F32))
        for r, y, (_, _, _, _, inter) in zip(rows, ys, st):
            ee = ee_s[r, :]
            xs = xs_s[r, :]
            y = y + ee[:, 0:W_GRP] * inter[0] + ee[:, W_GRP:2 * W_GRP] * inter[1]
            y = y + xs * dskip_ref[...]
            y = y * _silu(z_ref[0, r, :].astype(F32))
            y = y * lax.rsqrt(jnp.mean(y * y, axis=-1, keepdims=True) + EPS) * ng_ref[...]
            y_ref[0, r, :] = y
        return carry

    lax.fori_loop(0, nc // grp, phase3, 0)
    return hf_fin, hb_fin


def _ssd_kernel(Lc, L, zc_ref, xbcc_ref, dtc_ref, zl_ref, xbcl_ref, dtl_ref,
                cw_ref, cb_ref, alog_row, bias_row, dskip_ref, ng_ref, exp_ref, exp128_ref, sel_ref,
                yc_ref, yl_ref, *scr):
    prm = (cw_ref, cb_ref, alog_row, bias_row, dskip_ref, ng_ref, exp_ref, exp128_ref, sel_ref)
    zero = jnp.zeros((SSD_STATE, W_GRP), F32)
    hf, hb = _ssd_sequence(Lc, zc_ref, xbcc_ref, dtc_ref, yc_ref, prm, scr, zero, zero)
    _ssd_sequence(L, zl_ref, xbcl_ref, dtl_ref, yl_ref, prm, scr, hf, hb)


@functools.lru_cache(maxsize=None)
def _ssd_tables():
    exp64 = np.zeros((256, 512), np.float32)
    exp128 = np.zeros((384, 2 * N_HEADS * T), np.float32)
    sel = np.zeros((8, 384), np.float32)
    for r in range(2):
        for h in range(N_HEADS):
            m = r * N_HEADS + h
            for part in range(2):
                exp64[128 * part + m, r * 256 + 64 * h:r * 256 + 64 * (h + 1)] = 1.0
            for part in range(3):
                exp128[128 * part + m, (r * N_HEADS + h) * T:(r * N_HEADS + h + 1) * T] = 1.0
                sel[m, 128 * part + m] = 1.0
    return exp64, exp128, sel


def _pad_row(v, n=128):
    v = v.reshape(1, -1)
    return jnp.pad(v, ((0, 0), (0, n - v.shape[1])))


def _ssd_call(uc, ul, conv_w, conv_b, a_log, dt_bias, d_skip, norm_g):
    zc, xbcc, dtc = uc
    zl, xbcl, dtl = ul
    bsz, Lc, _ = zc.shape
    L = zl.shape[1]
    nc = L // T
    alog_row = _pad_row(a_log)
    bias_row = _pad_row(dt_bias)
    dskip = jnp.repeat(d_skip, HEAD_DIM).reshape(1, W_GRP)
    exp64, exp128, sel = (jnp.asarray(t).astype(BF16) for t in _ssd_tables())

    def seq(Lx, w):
        return pl.BlockSpec((1, Lx, w), lambda b: (b, 0, 0))

    def small(shape):
        return pl.BlockSpec(shape, lambda b: (0,) * len(shape))

    in_specs = [seq(Lc, W_GRP), seq(Lc, SSD_XBC), seq(Lc, 128),
                seq(L, W_GRP), seq(L, SSD_XBC), seq(L, 128),
                small((3, SSD_XBC)), small((1, SSD_XBC)), small((1, 128)),
                small((1, 128)), small((1, W_GRP)), small((1, W_GRP)),
                small(exp64.shape), small(exp128.shape), small(sel.shape)]
    scratch = [pltpu.VMEM((L, W_GRP), F32), pltpu.VMEM((L, 128), F32), pltpu.VMEM((L, 512), F32),
               pltpu.VMEM((L, 128), F32), pltpu.VMEM((L, 128), F32),
               pltpu.VMEM((nc + 1, SSD_STATE, W_GRP), F32), pltpu.VMEM((nc + 1, SSD_STATE, W_GRP), F32),
               pltpu.VMEM((nc, 1, W_GRP), F32), pltpu.VMEM((nc, 1, W_GRP), F32)]
    return pl.pallas_call(
        functools.partial(_ssd_kernel, Lc, L),
        grid=(bsz,),
        in_specs=in_specs,
        out_specs=[seq(Lc, W_GRP), seq(L, W_GRP)],
        out_shape=[jax.ShapeDtypeStruct((bsz, Lc, W_GRP), F32), jax.ShapeDtypeStruct((bsz, L, W_GRP), F32)],
        scratch_shapes=scratch,
        compiler_params=_cparams(1),
        name="ssd_mixer",
    )(zc, xbcc, dtc, zl, xbcl, dtl, conv_w, conv_b.reshape(1, -1),
      alog_row, bias_row, dskip, norm_g.reshape(1, -1), exp64, exp128, sel)


def _group_norm_heads(y, avg_ref):
    def gmean(x):
        hi = x.astype(BF16)
        lo = (x - hi.astype(F32)).astype(BF16)
        return jnp.dot(jnp.concatenate([hi, lo], axis=1), avg_ref[...], preferred_element_type=F32)

    yc = y - gmean(y)
    return yc * lax.rsqrt(gmean(yc * yc) + EPS)


def _ret_sequence(L, rope, q_ref, k_ref, v_ref, g_ref, y_ref, cos_ref, sin_ref, perm_ref, avg_ref,
                  tabs, scr, h0f, h0b):
    (ef, eb, wf, wb, decf, decb, bdmask) = tabs
    (qr_s, kr_s, dm_s, hf_s, hb_s) = scr
    nc = L // T
    scale = HEAD_DIM ** -0.5
    r4 = lax.broadcasted_iota(jnp.int32, (N_HEADS * T, W_GRP), 0) // T
    c4 = lax.broadcasted_iota(jnp.int32, (N_HEADS * T, W_GRP), 1) // HEAD_DIM
    stackmask = r4 == c4

    def rot(x, rows):
        partner = jnp.dot(x, perm_ref[...], preferred_element_type=F32)
        return x.astype(F32) * cos_ref[rows, :] + partner * sin_ref[rows, :]

    grp = min(P3_GROUP, nc)

    def phase1(p, carry):
        cs = [p * grp + j for j in range(grp)]
        rows = [pl.ds(pl.multiple_of(c * T, T), T) for c in cs]
        qk = [(q_ref[0, r, :], k_ref[0, r, :]) for r in rows]
        if rope:
            qk = [(rot(q, r), rot(k, r)) for (q, k), r in zip(qk, rows)]
        for c, r, (q, k) in zip(cs, rows, qk):
            k = k.astype(F32) * scale
            qr_s[r, :] = q.astype(F32)
            kr_s[r, :] = k
            v = v_ref[0, r, :]
            hf_s[c + 1] = _dot_tn(k * wf, v) * bdmask
            hb_s[c] = _dot_tn(k * wb, v) * bdmask
        return carry

    lax.fori_loop(0, nc // grp, phase1, 0)

    hf_s[0] = h0f
    hb_s[nc] = h0b

    def fwd(c, carry):
        hf_s[c + 1] = decf * hf_s[c] + hf_s[c + 1]
        return carry

    lax.fori_loop(0, nc, fwd, 0)

    def bwd(kk, carry):
        c = nc - 1 - kk
        hb_s[c] = decb * hb_s[c + 1] + hb_s[c]
        return carry

    lax.fori_loop(0, nc, bwd, 0)

    def gmean(x):
        hi = x.astype(BF16)
        lo = (x - hi.astype(F32)).astype(BF16)
        return jnp.dot(jnp.concatenate([hi, lo], axis=1), avg_ref[...], preferred_element_type=F32)

    def phase3(p, carry):
        cs = [p * grp + j for j in range(grp)]
        rows = [pl.ds(pl.multiple_of(c * T, T), T) for c in cs]
        qs = [qr_s[r, :] for r in rows]
        scs = [lax.dot_general(q.astype(BF16), _tile_heads_bd(kr_s[r, :], stackmask),
                               (((1,), (1,)), ((), ())), preferred_element_type=F32)
               for q, r in zip(qs, rows)]
        inters = [_dot(q * ef, hf_s[c]) + _dot(q * eb, hb_s[c + 1]) for q, c in zip(qs, cs)]
        ys = [jnp.dot((sc * dm_s[...]).astype(BF16), _tile_heads_bd(v_ref[0, r, :], stackmask),
                      preferred_element_type=F32) + it
              for sc, r, it in zip(scs, rows, inters)]
        ycs = [y - gmean(y) for y in ys]
        vars_ = [gmean(yc * yc) for yc in ycs]
        for r, yc, var in zip(rows, ycs, vars_):
            y_ref[0, r, :] = _silu(g_ref[0, r, :].astype(F32)) * (yc * lax.rsqrt(var + EPS))
        return carry

    lax.fori_loop(0, nc // grp, phase3, 0)
    return hf_s[nc], hb_s[0]


def _ret_kernel(Lc, L, qc_ref, kc_ref, vc_ref, gc_ref, ql_ref, kl_ref, vl_ref, gl_ref,
                cos_ref, sin_ref, dec_ref, perm_ref, avg_ref, yc_ref, yl_ref,
                qr_s, kr_s, dm_s, hf_s, hb_s):
    lg = -jnp.exp(dec_ref[...])
    lgf = lg[0:1, :]
    lgb = lg[1:2, :]
    i = lax.broadcasted_iota(jnp.int32, (T, W_GRP), 0).astype(F32)
    ef = jnp.exp(lgf * (i + 1.0))
    eb = jnp.exp(lgb * (T - i))
    wf = jnp.exp(lgf * (T - 1.0 - i))
    wb = jnp.exp(lgb * i)
    decf = jnp.exp(lgf * float(T))
    decb = jnp.exp(lgb * float(T))
    r2 = lax.broadcasted_iota(jnp.int32, (W_GRP, W_GRP), 0) // HEAD_DIM
    c2 = lax.broadcasted_iota(jnp.int32, (W_GRP, W_GRP), 1) // HEAD_DIM
    bdmask = (r2 == c2).astype(F32)
    ri = lax.broadcasted_iota(jnp.int32, (T, T), 0)
    ci = lax.broadcasted_iota(jnp.int32, (T, T), 1)
    d = (ri - ci).astype(F32)
    for h in range(N_HEADS):
        lf = lgf[:, HEAD_DIM * h:HEAD_DIM * h + 1]
        lb = lgb[:, HEAD_DIM * h:HEAD_DIM * h + 1]
        dm_s[:, T * h:T * (h + 1)] = (jnp.exp(jnp.where(ci <= ri, lf * d, -jnp.inf))
                                      + jnp.exp(jnp.where(ci >= ri, -lb * d, -jnp.inf)))
    tabs = (ef, eb, wf, wb, decf, decb, bdmask)
    scr = (qr_s, kr_s, dm_s, hf_s, hb_s)
    zero = jnp.zeros((W_GRP, W_GRP), F32)
    hf, hb = _ret_sequence(Lc, False, qc_ref, kc_ref, vc_ref, gc_ref, yc_ref, cos_ref, sin_ref,
                           perm_ref, avg_ref, tabs, scr, zero, zero)
    _ret_sequence(L, True, ql_ref, kl_ref, vl_ref, gl_ref, yl_ref, cos_ref, sin_ref,
                  perm_ref, avg_ref, tabs, scr, hf, hb)


@functools.lru_cache(maxsize=None)
def _rope_tables(L):
    t = np.arange(L)
    f = 16
    inv = (ROPE_BASE ** (-np.arange(f, dtype=np.float32) / f)).astype(np.float32)
    cos = np.zeros((L, HEAD_DIM), np.float32)
    sin = np.zeros((L, HEAD_DIM), np.float32)
    for base, pos in ((0, t // GRID_W), (32, t % GRID_W)):
        ang = pos.astype(np.float32)[:, None] * inv[None, :]
        ang = ang.astype(np.float32).astype(np.float64)
        cos[:, base:base + f] = np.cos(ang)
        cos[:, base + f:base + 2 * f] = np.cos(ang)
        sin[:, base:base + f] = -np.sin(ang)
        sin[:, base + f:base + 2 * f] = np.sin(ang)
    return np.tile(cos, (1, N_HEADS)), np.tile(sin, (1, N_HEADS))


@functools.lru_cache(maxsize=None)
def _ret_tables():
    perm = np.zeros((W_GRP, W_GRP), np.float32)
    avg = np.zeros((2 * W_GRP, W_GRP), np.float32)
    for l in range(W_GRP):
        src = l + 16 if (l % 32) < 16 else l - 16
        perm[src, l] = 1.0
        g = l // HEAD_DIM
        for part in range(2):
            avg[part * W_GRP + g * HEAD_DIM:part * W_GRP + (g + 1) * HEAD_DIM, l] = 1.0 / HEAD_DIM
    return perm, avg


def _ret_call(uc, ul, decay_param):
    bsz, Lc, _ = uc.shape
    L = ul.shape[1]
    nc = L // T
    cos, sin = _rope_tables(L)
    perm, avg = (jnp.asarray(t).astype(BF16) for t in _ret_tables())
    dec = jnp.repeat(decay_param, HEAD_DIM, axis=1)

    def col(Lx, j):
        return pl.BlockSpec((1, Lx, W_GRP), lambda b, j=j: (b, 0, j))

    def seq(Lx):
        return pl.BlockSpec((1, Lx, W_GRP), lambda b: (b, 0, 0))

    in_specs = ([col(Lc, j) for j in range(4)] + [col(L, j) for j in range(4)]
                + [pl.BlockSpec((L, W_GRP), lambda b: (0, 0)), pl.BlockSpec((L, W_GRP), lambda b: (0, 0)),
                   pl.BlockSpec((2, W_GRP), lambda b: (0, 0)),
                   pl.BlockSpec((W_GRP, W_GRP), lambda b: (0, 0)),
                   pl.BlockSpec((2 * W_GRP, W_GRP), lambda b: (0, 0))])
    scratch = [pltpu.VMEM((L, W_GRP), F32), pltpu.VMEM((L, W_GRP), F32),
               pltpu.VMEM((T, N_HEADS * T), F32),
               pltpu.VMEM((nc + 1, W_GRP, W_GRP), F32), pltpu.VMEM((nc + 1, W_GRP, W_GRP), F32)]
    return pl.pallas_call(
        functools.partial(_ret_kernel, Lc, L),
        grid=(bsz,),
        in_specs=in_specs,
        out_specs=[seq(Lc), seq(L)],
        out_shape=[jax.ShapeDtypeStruct((bsz, Lc, W_GRP), F32), jax.ShapeDtypeStruct((bsz, L, W_GRP), F32)],
        scratch_shapes=scratch,
        compiler_params=_cparams(1),
        name="retention_mixer",
    )(uc, uc, uc, uc, ul, ul, ul, ul, jnp.asarray(cos), jnp.asarray(sin), dec, perm, avg)


def _s5_prep_kernel(are_ref, aim_ref, ldt_ref, bre_ref, bim_ref, cre_ref, cim_ref,
                    bmat_ref, cmat_ref, ab_ref):
    a_re = are_ref[0]
    a_im = aim_ref[0]
    dt = jnp.exp(ldt_ref[0])
    mag = jnp.exp(a_re * dt)
    ab_re = mag * jnp.cos(a_im * dt)
    ab_im = mag * jnp.sin(a_im * dt)
    den = a_re * a_re + a_im * a_im
    z_re = ((ab_re - 1.0) * a_re + ab_im * a_im) / den
    z_im = (ab_im * a_re - (ab_re - 1.0) * a_im) / den
    b_re = bre_ref[...]
    b_im = bim_ref[...]
    bmat_ref[0, :, 0:S5_NS] = (b_re * z_re - b_im * z_im).astype(BF16)
    bmat_ref[0, :, S5_NS:2 * S5_NS] = (b_re * z_im + b_im * z_re).astype(BF16)
    cmat_ref[0, 0:S5_NS, :] = cre_ref[0].astype(BF16)
    cmat_ref[0, S5_NS:2 * S5_NS, :] = (-cim_ref[0]).astype(BF16)
    ab_ref[0, :, 0:S5_NS] = ab_re
    ab_ref[0, :, S5_NS:2 * S5_NS] = ab_im


def _s5_prep_call(a_re, a_im, log_dt, b_re, b_im, c_re, c_im):
    eye = jnp.eye(S5_GROUPS, dtype=F32)
    b_re_bd = jnp.einsum('gpc,gh->gchp', b_re, eye).reshape(W_GRP, S5_NS)
    b_im_bd = jnp.einsum('gpc,gh->gchp', b_im, eye).reshape(W_GRP, S5_NS)
    c_re_bd = jnp.einsum('rgcp,gh->rgphc', c_re, eye).reshape(2, S5_NS, W_GRP)
    c_im_bd = jnp.einsum('rgcp,gh->rgphc', c_im, eye).reshape(2, S5_NS, W_GRP)
    ldt = jnp.repeat(log_dt, S5_STATE, axis=1).reshape(2, 1, S5_NS)
    row = lambda: pl.BlockSpec((1, 1, S5_NS), lambda r: (r, 0, 0))
    return pl.pallas_call(
        _s5_prep_kernel,
        grid=(2,),
        in_specs=[row(), row(), row(),
                  pl.BlockSpec((W_GRP, S5_NS), lambda r: (0, 0)), pl.BlockSpec((W_GRP, S5_NS), lambda r: (0, 0)),
                  pl.BlockSpec((1, S5_NS, W_GRP), lambda r: (r, 0, 0)),
                  pl.BlockSpec((1, S5_NS, W_GRP), lambda r: (r, 0, 0))],
        out_specs=[pl.BlockSpec((1, W_GRP, 2 * S5_NS), lambda r: (r, 0, 0)),
                   pl.BlockSpec((1, 2 * S5_NS, W_GRP), lambda r: (r, 0, 0)),
                   pl.BlockSpec((1, 1, 2 * S5_NS), lambda r: (r, 0, 0))],
        out_shape=[jax.ShapeDtypeStruct((2, W_GRP, 2 * S5_NS), BF16),
                   jax.ShapeDtypeStruct((2, 2 * S5_NS, W_GRP), BF16),
                   jax.ShapeDtypeStruct((2, 1, 2 * S5_NS), F32)],
        compiler_params=_cparams(1),
        name="s5_prep",
    )(a_re.reshape(2, 1, S5_NS), a_im.reshape(2, 1, S5_NS), ldt, b_re_bd, b_im_bd, c_re_bd, c_im_bd)


def _s5_dir_kernel(bsz, n, reverse, u_ref, h0_ref, bmat_ref, cmat_ref, ab_ref, *rest):
    if reverse:
        y_ref, hs_ref, x_s, hb_s = rest
    else:
        ul_ref, yb_ref, d_ref, gw_ref, gb_ref, y_ref, hs_ref, x_s, hb_s = rest
    step = pl.program_id(0)

    @pl.when(step == 0)
    def _():
        hs_ref[...] = h0_ref[...]
        hb_s[...] = jnp.zeros_like(hb_s)

    u = u_ref[...].reshape(S5_TS * bsz, W_GRP)
    x_s[...] = jnp.dot(u, bmat_ref[0], preferred_element_type=F32)
    y = jnp.dot(hb_s[...], cmat_ref[0], preferred_element_type=F32)
    if not reverse:
        y = (y + yb_ref[...].reshape(S5_TS * bsz, W_GRP)
             + ul_ref[...].reshape(S5_TS * bsz, W_GRP).astype(F32) * d_ref[...])
        y = jax.nn.gelu(y)
        y = y * jax.nn.sigmoid(jnp.dot(y.astype(BF16), gw_ref[...], preferred_element_type=F32)
                               + gb_ref[...])
    y_ref[...] = y.reshape(S5_TS, bsz, W_GRP)

    @pl.when(step < n)
    def _():
        _s5_scan_block(bsz, reverse, ab_ref, hs_ref, x_s, hb_s)


def _s5_scan_block(bsz, reverse, ab_ref, hs_ref, x_s, hb_s):
    for q in range(S5_NS // W_GRP):
        cr = slice(W_GRP * q, W_GRP * (q + 1))
        cim = slice(S5_NS + W_GRP * q, S5_NS + W_GRP * (q + 1))
        a_re = jnp.broadcast_to(ab_ref[0, :, cr], (bsz, W_GRP))
        a_im = jnp.broadcast_to(ab_ref[0, :, cim], (bsz, W_GRP))

        def body(kk, carry, cr=cr, cim=cim, a_re=a_re, a_im=a_im):
            h_re, h_im = carry
            t = (S5_TS - 1 - kk) if reverse else kk
            rows = pl.ds(pl.multiple_of(t * bsz, bsz), bsz)
            n_re = a_re * h_re - a_im * h_im + x_s[rows, cr]
            n_im = a_re * h_im + a_im * h_re + x_s[rows, cim]
            hb_s[rows, cr] = n_re.astype(BF16)
            hb_s[rows, cim] = n_im.astype(BF16)
            return n_re, n_im

        h_re, h_im = lax.fori_loop(0, S5_TS, body, (hs_ref[:, cr], hs_ref[:, cim]), unroll=4)
        hs_ref[:, cr] = h_re
        hs_ref[:, cim] = h_im


def _s5_dir_call(r, u3, h0, bmat, cmat, ab, post=None):
    L, bsz, _ = u3.shape
    n = L // S5_TS
    blk = (S5_TS, bsz, W_GRP)
    reverse = r == 1

    def block_of(c):
        return n - 1 - c if reverse else c

    cur = lambda i: (block_of(jnp.minimum(i, n - 1)), 0, 0)
    lag = lambda i: (block_of(jnp.maximum(i - 1, 0)), 0, 0)
    state = pl.BlockSpec((bsz, 2 * S5_NS), lambda i: (0, 0))
    in_specs = [pl.BlockSpec(blk, cur), state,
                pl.BlockSpec((1, W_GRP, 2 * S5_NS), lambda i: (r, 0, 0)),
                pl.BlockSpec((1, 2 * S5_NS, W_GRP), lambda i: (r, 0, 0)),
                pl.BlockSpec((1, 1, 2 * S5_NS), lambda i: (r, 0, 0))]
    args = [u3, h0, bmat, cmat, ab]
    if not reverse:
        yb, d, glu_w, glu_b = post
        in_specs += [pl.BlockSpec(blk, lag), pl.BlockSpec(blk, lag),
                     pl.BlockSpec((1, W_GRP), lambda i: (0, 0)),
                     pl.BlockSpec((W_GRP, W_GRP), lambda i: (0, 0)),
                     pl.BlockSpec((1, W_GRP), lambda i: (0, 0))]
        args += [u3, yb, d.reshape(1, -1), glu_w.astype(BF16), glu_b.reshape(1, -1)]
    return pl.pallas_call(
        functools.partial(_s5_dir_kernel, bsz, n, reverse),
        grid=(n + 1,),
        in_specs=in_specs,
        out_specs=[pl.BlockSpec(blk, lag), state],
        out_shape=[jax.ShapeDtypeStruct((L, bsz, W_GRP), F32),
                   jax.ShapeDtypeStruct((bsz, 2 * S5_NS), F32)],
        scratch_shapes=[pltpu.VMEM((S5_TS * bsz, 2 * S5_NS), F32),
                        pltpu.VMEM((S5_TS * bsz, 2 * S5_NS), BF16)],
        compiler_params=_cparams(1),
        name="s5_bwd" if reverse else "s5_fwd",
    )(*args)


def _s5_mixer(u5c, u5l, bsz, p):
    bmat, cmat, ab = _s5_prep_call(p['s5_a_re'], p['s5_a_im'], p['s5_log_dt'], p['s5_b_re'],
                                   p['s5_b_im'], p['s5_c_re'], p['s5_c_im'])
    Lc, L = u5c.shape[0], u5l.shape[0]
    u3c = u5c.reshape(Lc, bsz, W_GRP)
    u3l = u5l.reshape(L, bsz, W_GRP)
    h0 = jnp.zeros((bsz, 2 * S5_NS), F32)
    head = (p['s5_d'], p['s5_glu_w'], p['s5_glu_b'])
    ybc, hbc = _s5_dir_call(1, u3c, h0, bmat, cmat, ab)
    ybl, _ = _s5_dir_call(1, u3l, hbc, bmat, cmat, ab)
    s5c, hfc = _s5_dir_call(0, u3c, h0, bmat, cmat, ab, post=(ybc,) + head)
    s5l, _ = _s5_dir_call(0, u3l, hfc, bmat, cmat, ab, post=(ybl,) + head)
    return s5c.reshape(Lc, bsz * W_GRP), s5l.reshape(L, bsz * W_GRP)


@functools.lru_cache(maxsize=None)
def _dft_tables(L):
    k = np.arange(L, dtype=np.int64)
    ft = (k[:, None] * k[None, :]) % (2 * L)
    ang = ft.astype(np.float64) * (math.pi / L)
    return np.cos(ang).astype(np.float32), np.sin(ang).astype(np.float32)


@functools.lru_cache(maxsize=None)
def _dft_tables_split(L):
    c, s = _dft_tables(L)
    return (np.concatenate([c[0::2], c[1::2]], axis=0), np.concatenate([s[0::2], s[1::2]], axis=0))


@functools.lru_cache(maxsize=None)
def _hyena_consts(L):
    t = np.linspace(0.0, 1.0, L, dtype=np.float32)[:, None]
    w = (2.0 * math.pi * np.arange(L, dtype=np.float32)[:, None] / L).astype(np.float32)
    bands = np.linspace(1e-4, HY_BANDS - 1, HY_BANDS, dtype=np.float32)[None, :]
    bw = (bands * w).astype(np.float32).astype(np.float64)
    feats = np.zeros((L, 128), np.float32)
    feats[:, 0:1] = t
    feats[:, 1:1 + HY_BANDS] = np.cos(bw)
    feats[:, 1 + HY_BANDS:HY_EMB] = -np.sin(bw)
    max_decay = math.log(1e-2) / 0.3
    min_decay = math.log(1e-2) / 1.5
    deltas = np.abs(np.linspace(min_decay, max_decay, 4 * W_GRP, dtype=np.float32))[None, :]
    return feats, deltas.astype(np.float32)


HY_RB = 256


def _hy_filter_kernel(feats_ref, w1_ref, b1_ref, fr_ref, w2_ref, b2_ref, w3_ref, del_ref,
                      p_ref, q_ref, nrm_ref, an_ref):
    i = pl.program_id(0)
    feats = feats_ref[...]
    fr = fr_ref[...]
    h = jnp.sin(fr * (_dot_f32(feats, w1_ref[...]) + b1_ref[...]))
    h = jnp.sin(fr * (_dot_f32(h, w2_ref[...]) + b2_ref[...]))
    h = _dot_f32(h, w3_ref[...])
    h = h * jnp.exp(-feats[:, 0:1] * del_ref[...])
    half = 2 * W_GRP
    hf = h[:, 0:half]
    row = lax.broadcasted_iota(jnp.int32, (HY_RB, half), 0) + i * HY_RB
    hb = jnp.where(row == 0, 0.0, h[:, half:2 * half])
    p = hf + hb
    sign = (1 - 2 * (row % 2)).astype(F32)
    p_ref[...] = p.astype(BF16)
    q_ref[...] = (hb - hf).astype(BF16)

    @pl.when(i == 0)
    def _():
        nrm_ref[...] = jnp.full_like(nrm_ref, EPS)
        an_ref[...] = jnp.zeros_like(an_ref)

    nrm_ref[...] += (jnp.sum(jnp.abs(hf), axis=0, keepdims=True)
                     + jnp.sum(jnp.abs(hb), axis=0, keepdims=True))
    an_ref[...] += jnp.sum(p * sign, axis=0, keepdims=True)


def _hy_spectrum_kernel(L, c_ref, s_ref, p_ref, q_ref, nrm_ref, ans_ref, a_ref, bc_ref, an_ref):
    i = pl.program_id(0)
    n = 2.0 * L
    inv = 1.0 / nrm_ref[...]
    row = lax.broadcasted_iota(jnp.int32, a_ref.shape, 0) + i * HY_RB
    wv = jnp.where(row == 0, 1.0 / n, 2.0 / n) * inv
    a_ref[...] = wv * jnp.dot(c_ref[...], p_ref[...], preferred_element_type=F32)
    bc_ref[...] = wv * jnp.dot(s_ref[...], q_ref[...], preferred_element_type=F32)
    an_ref[...] = ans_ref[...] * inv / n


def _hy_filter_call(L, w1, b1, freq, w2, b2, w3, cmat, smat):
    feats, deltas = _hyena_consts(L)
    w1p = jnp.pad(w1, ((0, 128 - HY_EMB), (0, 0)))
    half = 2 * W_GRP
    nb = L // HY_RB
    full = lambda a: pl.BlockSpec(a.shape, lambda i, nd=a.ndim: (0,) * nd)
    small = (w1p, b1.reshape(1, -1), freq.reshape(1, -1), w2, b2.reshape(1, -1), w3, jnp.asarray(deltas))
    rowblk = lambda w: pl.BlockSpec((HY_RB, w), lambda i: (i, 0))
    vec = pl.BlockSpec((1, half), lambda i: (0, 0))
    p, q, nrm, ans = pl.pallas_call(
        _hy_filter_kernel,
        grid=(nb,),
        in_specs=[rowblk(128)] + [full(a) for a in small],
        out_specs=[rowblk(half), rowblk(half), vec, vec],
        out_shape=[jax.ShapeDtypeStruct((L, half), BF16), jax.ShapeDtypeStruct((L, half), BF16),
                   jax.ShapeDtypeStruct((1, half), F32), jax.ShapeDtypeStruct((1, half), F32)],
        compiler_params=_cparams(1),
        name="hyena_filter",
    )(jnp.asarray(feats), *small)
    return pl.pallas_call(
        functools.partial(_hy_spectrum_kernel, L),
        grid=(nb,),
        in_specs=[rowblk(L), rowblk(L), full(p), full(q), vec, vec],
        out_specs=[rowblk(half), rowblk(half), vec],
        out_shape=[jax.ShapeDtypeStruct((L, half), F32), jax.ShapeDtypeStruct((L, half), F32),
                   jax.ShapeDtypeStruct((1, half), F32)],
        compiler_params=_cparams(1),
        name="hyena_spectrum",
    )(cmat, smat, p, q, nrm, ans)


def _alt_sign(shape):
    return (1 - 2 * (lax.broadcasted_iota(jnp.int32, shape, 0) % 2)).astype(F32)


def _reverse_shift(x, j_ref):
    hi = x.astype(BF16)
    lo = (x - hi.astype(F32)).astype(BF16)
    j = j_ref[...]
    return (jnp.dot(j, hi, preferred_element_type=F32) + jnp.dot(j, lo, preferred_element_type=F32))


def _hy_conv_kernel(L, u_ref, cw_ref, cb_ref, ce_ref, se_ref, co_ref, so_ref, cot_ref, sot_ref, j_ref,
                    a_ref, bc_ref, an_ref, bias_ref, o_ref,
                    x1_s, x2_s, z_s, zs_s, zd_s, xe_s, ye_s, xo_s, yo_s, d_s, acc_s):
    nc = L // T
    H = L // 2
    fb = min(256, H)
    nfb = H // fb
    rb = min(512, H)
    nrb = H // rb

    def conv(c, carry):
        s = pl.multiple_of(c * T, T)
        rows = pl.ds(s, T)
        prev, nxt = _halo_rows(u_ref, s, c, nc, L)
        y = _dwconv_chunk(u_ref[0, rows, :].astype(F32), prev, nxt, cw_ref, cb_ref)
        x1_s[rows, :] = y[:, 0:W_GRP]
        x2_s[rows, :] = y[:, W_GRP:2 * W_GRP]
        z_s[rows, :] = y[:, 2 * W_GRP:3 * W_GRP]
        return carry

    lax.fori_loop(0, nc, conv, 0)
    row0 = lax.broadcasted_iota(jnp.int32, (fb, W_GRP), 0) == 0

    for o, gate_s in enumerate((x1_s, x2_s)):
        cols = slice(W_GRP * o, W_GRP * (o + 1))
        acc_s[...] = jnp.zeros_like(acc_s)

        def fold(b, carry):
            top_rows = pl.ds(pl.multiple_of(b * fb, fb), fb)
            top = z_s[top_rows, :]
            w = z_s[pl.ds(pl.multiple_of(H + (nfb - 1 - b) * fb, fb), fb), :]
            edge = z_s[pl.ds(pl.multiple_of(jnp.minimum(H + (nfb - b) * fb, L - 8), 8), 8), :][0:1, :]
            zr = jnp.where(row0, jnp.where(b > 0, edge, 0.0), _reverse_shift(w, j_ref))
            zs_s[top_rows, :] = (top + zr).astype(BF16)
            zd_s[top_rows, :] = (top - zr).astype(BF16)
            acc_s[0:1, :] += jnp.sum((top + w) * _alt_sign(top.shape), axis=0, keepdims=True)
            return carry

        lax.fori_loop(0, nfb, fold, 0)
        z_mid = z_s[H:H + 1, :]

        def fwd(j, carry, cols=cols, z_mid=z_mid):
            rows = pl.ds(pl.multiple_of(j * rb, rb), rb)
            odd_rows = pl.ds(pl.multiple_of(H + j * rb, rb), rb)
            zs = zs_s[...]
            zd = zd_s[...]
            mid = _alt_sign((rb, W_GRP)) * z_mid
            pe = jnp.dot(ce_ref[rows, :], zs, preferred_element_type=F32) + mid
            qo = jnp.dot(so_ref[rows, :], zs, preferred_element_type=F32) + mid
            po = jnp.dot(co_ref[rows, :], zd, preferred_element_type=F32)
            qe = jnp.dot(se_ref[rows, :], zd, preferred_element_type=F32)
            ae = a_ref[rows, cols]
            bce = bc_ref[rows, cols]
            ao = a_ref[odd_rows, cols]
            bco = bc_ref[odd_rows, cols]
            xe = pe * ae + qe * bce
            yo = qo * ao - po * bco
            xe_s[rows, :] = xe.astype(BF16)
            ye_s[rows, :] = (qe * ae - pe * bce).astype(BF16)
            xo_s[rows, :] = (po * ao + qo * bco).astype(BF16)
            yo_s[rows, :] = yo.astype(BF16)
            acc_s[1:2, :] += jnp.sum((xe + yo) * _alt_sign(xe.shape), axis=0, keepdims=True)
            return carry

        lax.fori_loop(0, nrb, fwd, 0)
        nyq = acc_s[0:1, :] * an_ref[:, cols]

        def finish(rows, y, cols=cols, gate_s=gate_s, nyq=nyq, o=o):
            res = gate_s[rows, :] * (y + _alt_sign(y.shape) * nyq + z_s[rows, :] * bias_ref[:, cols])
            if o == 0:
                z_s[rows, :] = res
            else:
                o_ref[0, rows, :] = res

        def inv(j, carry, finish=finish):
            rows = pl.ds(pl.multiple_of(j * rb, rb), rb)
            u1 = (jnp.dot(ce_ref[rows, :], xe_s[...], preferred_element_type=F32)
                  + jnp.dot(sot_ref[rows, :], yo_s[...], preferred_element_type=F32))
            u2 = (jnp.dot(cot_ref[rows, :], xo_s[...], preferred_element_type=F32)
                  + jnp.dot(se_ref[rows, :], ye_s[...], preferred_element_type=F32))
            d_s[rows, :] = u1 - u2
            finish(rows, u1 + u2)
            return carry

        lax.fori_loop(0, nrb, inv, 0)
        y_mid = acc_s[1:2, :]

        def second(b, carry, finish=finish, y_mid=y_mid):
            rows = pl.ds(pl.multiple_of(H + b * fb, fb), fb)
            dblk = d_s[pl.ds(pl.multiple_of((nfb - 1 - b) * fb, fb), fb), :]
            edge = d_s[pl.ds(pl.multiple_of(jnp.minimum((nfb - b) * fb, H - 8), 8), 8), :][0:1, :]
            y = jnp.where(row0, jnp.where(b > 0, edge, y_mid), _reverse_shift(dblk, j_ref))
            finish(rows, y)
            return carry

        lax.fori_loop(0, nfb, second, 0)


@functools.lru_cache(maxsize=None)
def _dft_half_tables(L):
    c, s = _dft_tables(L)
    H = L // 2
    fb = min(256, H)
    ce, co = c[0::2, :H], c[1::2, :H]
    se, so = s[0::2, :H], s[1::2, :H]
    j = np.zeros((fb, fb), np.float32)
    for r in range(1, fb):
        j[r, fb - r] = 1.0
    return tuple(np.ascontiguousarray(m) for m in (ce, se, co, so, co.T, so.T, j))


def _hy_conv_call(u, conv_w, conv_b, tables, a, bc, an, bias):
    bsz, L, _ = u.shape
    H = L // 2
    half = 2 * W_GRP
    mats = [jnp.asarray(m).astype(BF16) for m in tables]
    return pl.pallas_call(
        functools.partial(_hy_conv_kernel, L),
        grid=(bsz,),
        in_specs=[pl.BlockSpec((1, L, HY_COLS), lambda b: (b, 0, 0)),
                  pl.BlockSpec((3, HY_COLS), lambda b: (0, 0)),
                  pl.BlockSpec((1, HY_COLS), lambda b: (0, 0))]
                 + [_const_spec(m.shape) for m in mats]
                 + [_const_spec((L, half)), _const_spec((L, half)),
                    pl.BlockSpec((1, half), lambda b: (0, 0)),
                    pl.BlockSpec((1, half), lambda b: (0, 0))],
        out_specs=pl.BlockSpec((1, L, W_GRP), lambda b: (b, 0, 0)),
        out_shape=jax.ShapeDtypeStruct((bsz, L, W_GRP), F32),
        scratch_shapes=[pltpu.VMEM((L, W_GRP), F32), pltpu.VMEM((L, W_GRP), F32),
                        pltpu.VMEM((L, W_GRP), F32),
                        pltpu.VMEM((H, W_GRP), BF16), pltpu.VMEM((H, W_GRP), BF16),
                        pltpu.VMEM((H, W_GRP), BF16), pltpu.VMEM((H, W_GRP), BF16),
                        pltpu.VMEM((H, W_GRP), BF16), pltpu.VMEM((H, W_GRP), BF16),
                        pltpu.VMEM((H, W_GRP), F32), pltpu.VMEM((8, W_GRP), F32)],
        compiler_params=_cparams(1),
        name="hyena_conv",
    )(u, conv_w, conv_b.reshape(1, -1), *mats, a, bc, an, bias.reshape(1, -1))


def _hyena_mixer(u, p):
    L = u.shape[1]
    cnp, snp = _dft_tables_split(L)
    cmat = jnp.asarray(cnp).astype(BF16)
    smat = jnp.asarray(snp).astype(BF16)
    a, bc, an = _hy_filter_call(L, p['hy_w1'], p['hy_b1'], p['hy_freq'], p['hy_w2'], p['hy_b2'],
                                p['hy_w3'], cmat, smat)
    return _hy_conv_call(u, p['hy_conv_w'], p['hy_conv_b'], _dft_half_tables(L), a, bc, an, p['hy_bias'])


def _out_ffn_kernel(final, h_ref, ssd_ref, hy_ref, ret_ref, s5_ref, mod_ref, g2_ref, wo_ref,
                    wup_ref, wdn_ref, fg_ref, o_ref, acc_s):
    y = jnp.zeros(h_ref.shape[1:], F32)
    for j, blk in enumerate((ssd_ref[0], hy_ref[0], ret_ref[0], s5_ref[...])):
        y = y + jnp.dot(blk.astype(BF16), wo_ref[W_GRP * j:W_GRP * (j + 1), :],
                        preferred_element_type=F32)
    h1 = h_ref[0] + mod_ref[0, 2:3, :] * y
    xn = h1 * lax.rsqrt(jnp.mean(h1 * h1, axis=-1, keepdims=True) + EPS) * g2_ref[...]
    xm = (xn * (1.0 + mod_ref[0, 4:5, :]) + mod_ref[0, 3:4, :]).astype(BF16)
    fc = 256
    for j in range(D_FF // fc):
        gg = jnp.dot(xm, wup_ref[:, fc * j:fc * (j + 1)], preferred_element_type=F32)
        uu = jnp.dot(xm, wup_ref[:, D_FF + fc * j:D_FF + fc * (j + 1)], preferred_element_type=F32)
        part = jnp.dot((_silu(gg) * uu).astype(BF16), wdn_ref[fc * j:fc * (j + 1), :],
                       preferred_element_type=F32)
        if j == 0:
            acc_s[...] = part
        else:
            acc_s[...] += part
    h2 = h1 + mod_ref[0, 5:6, :] * acc_s[...]
    if final:
        h2 = h2 * lax.rsqrt(jnp.mean(h2 * h2, axis=-1, keepdims=True) + EPS) * fg_ref[...]
    o_ref[0] = h2


def _out_ffn_call(h, mix, mods, ctx_stream, g2, wo, wup, wdn, final_g, final):
    bsz, L, _ = h.shape
    tm = min(512, L)
    mod_map = (lambda b, i: (bsz, 0, 0)) if ctx_stream else (lambda b, i: (b, 0, 0))
    tok = lambda w: pl.BlockSpec((1, tm, w), lambda b, i: (b, i, 0))
    return pl.pallas_call(
        functools.partial(_out_ffn_kernel, final),
        grid=(bsz, L // tm),
        in_specs=[tok(D_MODEL), tok(W_GRP), tok(W_GRP), tok(W_GRP),
                  pl.BlockSpec((tm, W_GRP), lambda b, i: (i, b)),
                  pl.BlockSpec((1, 6, D_MODEL), mod_map),
                  pl.BlockSpec((1, D_MODEL), lambda b, i: (0, 0)),
                  _const_spec((D_MODEL, D_MODEL)),
                  _const_spec((D_MODEL, 2 * D_FF)),
                  _const_spec((D_FF, D_MODEL)),
                  pl.BlockSpec((1, D_MODEL), lambda b, i: (0, 0))],
        out_specs=tok(D_MODEL),
        out_shape=jax.ShapeDtypeStruct((bsz, L, D_MODEL), F32),
        scratch_shapes=[pltpu.VMEM((tm, D_MODEL), F32)],
        compiler_params=_cparams(2),
        name="out_ffn",
    )(h, *mix, mods, g2, wo, wup, wdn, final_g)


def kernel(x, c, ctx, c_ctx, mod_w, mod_b, norm1_g, norm2_g, w_in, w_out, ssd_conv_w, ssd_conv_b, ssd_a_log, ssd_dt_bias, ssd_d, ssd_norm_g, hy_conv_w, hy_conv_b, hy_w1, hy_b1, hy_freq, hy_w2, hy_b2, hy_w3, hy_bias, ret_decay, s5_a_re, s5_a_im, s5_log_dt, s5_b_re, s5_b_im, s5_c_re, s5_c_im, s5_d, s5_glu_w, s5_glu_b, ffn_w_up, ffn_w_down, final_norm_g):
    bsz = x.shape[0]
    depth = mod_w.shape[0]
    sc = jnp.concatenate([c, c_ctx[None, :], jnp.zeros((MOD_ROWS - bsz - 1, D_MODEL), F32)], axis=0)
    mods_all = _mod_call(sc, mod_w, mod_b).reshape(depth, MOD_ROWS, 6, D_MODEL)
    fg = final_norm_g.reshape(1, -1)
    h_l, h_c = x, ctx
    o_xbc, o_dt, o_hy = W_GRP, W_GRP + SSD_XBC, W_GRP + SSD_XBC + 2 * N_HEADS
    o_ret = o_hy + HY_COLS
    o_s5 = o_ret + RET_COLS
    for i in range(depth):
        last = i == depth - 1
        wi = w_in[i]
        wdt = wi[:, o_dt:o_hy]
        wcat = jnp.concatenate([wi[:, 0:o_dt], wi[:, o_hy:o_s5 + W_GRP], wdt,
                                jnp.zeros((D_MODEL, 128 - 2 * N_HEADS), F32)], axis=1).astype(BF16)
        mods = mods_all[i]
        g1 = norm1_g[i].reshape(1, -1)
        zl, xbcl, hyl, retl, dtl, s5l = _inproj_call(h_l, g1, mods, False, wcat)
        zc, xbcc, hyc, retc, dtc, s5c = _inproj_call(h_c, g1, mods, True, wcat)
        ssd_c, ssd_l = _ssd_call((zc, xbcc, dtc), (zl, xbcl, dtl), ssd_conv_w[i],
                                 ssd_conv_b[i], ssd_a_log[i], ssd_dt_bias[i], ssd_d[i], ssd_norm_g[i])
        ret_c, ret_l = _ret_call(retc, retl, ret_decay[i])
        p = dict(s5_a_re=s5_a_re[i], s5_a_im=s5_a_im[i], s5_log_dt=s5_log_dt[i], s5_b_re=s5_b_re[i],
                 s5_b_im=s5_b_im[i], s5_c_re=s5_c_re[i], s5_c_im=s5_c_im[i], s5_d=s5_d[i],
                 s5_glu_w=s5_glu_w[i], s5_glu_b=s5_glu_b[i],
                 hy_conv_w=hy_conv_w[i], hy_conv_b=hy_conv_b[i], hy_w1=hy_w1[i], hy_b1=hy_b1[i],
                 hy_freq=hy_freq[i], hy_w2=hy_w2[i], hy_b2=hy_b2[i], hy_w3=hy_w3[i], hy_bias=hy_bias[i])
        s5_c, s5_l = _s5_mixer(s5c, s5l, bsz, p)
        hy_l = _hyena_mixer(hyl, p)
        g2 = norm2_g[i].reshape(1, -1)
        wo = w_out[i].astype(BF16)
        wup = ffn_w_up[i].astype(BF16)
        wdn = ffn_w_down[i].astype(BF16)
        h_l = _out_ffn_call(h_l, (ssd_l, hy_l, ret_l, s5_l), mods, False, g2, wo, wup, wdn, fg, last)
        if not last:
            hy_c = _hyena_mixer(hyc, p)
            h_c = _out_ffn_call(h_c, (ssd_c, hy_c, ret_c, s5_c), mods, True, g2, wo, wup, wdn, fg, False)
    return h_l
```

```python
import functools
import math

import numpy as np
import jax
import jax.numpy as jnp
from jax import lax
from jax.experimental import pallas as pl
from jax.experimental.pallas import tpu as pltpu

F32 = jnp.float32
BF16 = jnp.bfloat16
EPS = 1e-6

D_MODEL = 1024
W_GRP = 256
T = 128
N_HEADS = 4
HEAD_DIM = 64
SSD_STATE = 64
SSD_XBC = W_GRP + 2 * SSD_STATE
HY_COLS = 3 * W_GRP
RET_COLS = 4 * W_GRP
GRID_W = 64
ROPE_BASE = 10000.0
HY_EMB = 33
HY_BANDS = 16
HY_FILT = 64
S5_GROUPS = 16
S5_CH = 16
S5_STATE = 64
S5_NS = S5_GROUPS * S5_STATE
D_FF = 2816
P3_GROUP = 4
S5_TS = 64
MOD_ROWS = 24

VMEM_LIMIT = 56 * 1024 * 1024


def _cparams(n_grid):
    return pltpu.CompilerParams(dimension_semantics=("arbitrary",) * n_grid,
                                vmem_limit_bytes=VMEM_LIMIT)


def _dot(a, b):
    return jnp.dot(a.astype(BF16), b.astype(BF16), preferred_element_type=F32)


def _dot_nt(a, b):
    return lax.dot_general(a.astype(BF16), b.astype(BF16), (((1,), (1,)), ((), ())),
                           preferred_element_type=F32)


def _dot_tn(a, b):
    return lax.dot_general(a.astype(BF16), b.astype(BF16), (((0,), (0,)), ((), ())),
                           preferred_element_type=F32)


def _dot_f32(a, b):
    return jnp.dot(a, b, preferred_element_type=F32, precision=lax.Precision.HIGHEST)


def _silu(x):
    return x * jax.nn.sigmoid(x)


def _const_spec(shape):
    nd = len(shape)
    return pl.BlockSpec(shape, lambda *_: (0,) * nd, pipeline_mode=pl.Buffered(1))


def _mod_kernel(sc_ref, w_ref, b_ref, o_ref):
    s = _silu(sc_ref[...])
    o_ref[0] = _dot_f32(s, w_ref[0]) + b_ref[0]


def _mod_call(sc, mod_w, mod_b):
    depth, _, n = mod_w.shape
    tn = 1536
    return pl.pallas_call(
        _mod_kernel,
        grid=(depth, n // tn),
        in_specs=[pl.BlockSpec((MOD_ROWS, D_MODEL), lambda l, j: (0, 0)),
                  pl.BlockSpec((1, D_MODEL, tn), lambda l, j: (l, 0, j)),
                  pl.BlockSpec((1, 1, tn), lambda l, j: (l, 0, j))],
        out_specs=pl.BlockSpec((1, MOD_ROWS, tn), lambda l, j: (l, 0, j)),
        out_shape=jax.ShapeDtypeStruct((depth, MOD_ROWS, n), F32),
        compiler_params=_cparams(2),
        name="adaln_mod",
    )(sc, mod_w, mod_b.reshape(depth, 1, n))


def _inproj_kernel(x_ref, g_ref, mod_ref, w_ref, z_ref, xbc_ref, hy_ref, ret_ref, dt_ref, s5_ref):
    x = x_ref[0]
    xn = x * lax.rsqrt(jnp.mean(x * x, axis=-1, keepdims=True) + EPS) * g_ref[...]
    xm = (xn * (1.0 + mod_ref[0, 1:2, :]) + mod_ref[0, 0:1, :]).astype(BF16)
    o = 0
    for ref, width in ((z_ref, W_GRP), (xbc_ref, SSD_XBC), (hy_ref, HY_COLS),
                       (ret_ref, RET_COLS)):
        ref[0] = jnp.dot(xm, w_ref[:, o:o + width], preferred_element_type=F32).astype(BF16)
        o += width
    s5_ref[...] = jnp.dot(xm, w_ref[:, o:o + W_GRP], preferred_element_type=F32).astype(BF16)
    o += W_GRP
    dt_ref[0] = jnp.dot(xm, w_ref[:, o:o + 128], preferred_element_type=F32)


def _inproj_call(h, g, mods, ctx_stream, wcat):
    bsz, L, _ = h.shape
    tm = min(512, L)
    ncols = wcat.shape[1]
    mod_map = (lambda b, i: (bsz, 0, 0)) if ctx_stream else (lambda b, i: (b, 0, 0))
    tok = lambda w: pl.BlockSpec((1, tm, w), lambda b, i: (b, i, 0))
    out_shape = [jax.ShapeDtypeStruct((bsz, L, W_GRP), BF16),
                 jax.ShapeDtypeStruct((bsz, L, SSD_XBC), BF16),
                 jax.ShapeDtypeStruct((bsz, L, HY_COLS), BF16),
                 jax.ShapeDtypeStruct((bsz, L, RET_COLS), BF16),
                 jax.ShapeDtypeStruct((bsz, L, 128), F32),
                 jax.ShapeDtypeStruct((L, bsz * W_GRP), BF16)]
    out_specs = [tok(W_GRP), tok(SSD_XBC), tok(HY_COLS), tok(RET_COLS), tok(128),
                 pl.BlockSpec((tm, W_GRP), lambda b, i: (i, b))]
    return pl.pallas_call(
        _inproj_kernel,
        grid=(bsz, L // tm),
        in_specs=[tok(D_MODEL),
                  pl.BlockSpec((1, D_MODEL), lambda b, i: (0, 0)),
                  pl.BlockSpec((1, 6, D_MODEL), mod_map),
                  _const_spec((D_MODEL, ncols))],
        out_specs=out_specs,
        out_shape=out_shape,
        compiler_params=_cparams(2),
        name="in_proj",
    )(h, g, mods, wcat)


def _halo_rows(ref, s, c, nc, L):
    sp = pl.multiple_of(jnp.maximum(s - 16, 0), 16)
    prev = ref[0, pl.ds(sp, 16), :][15:16, :].astype(F32)
    prev = jnp.where(c > 0, prev, 0.0)
    sn = pl.multiple_of(jnp.minimum(s + T, L - 16), 16)
    nxt = ref[0, pl.ds(sn, 16), :][0:1, :].astype(F32)
    nxt = jnp.where(c < nc - 1, nxt, 0.0)
    return prev, nxt


def _dwconv_chunk(x, prev, nxt, w_ref, b_ref):
    n = x.shape[0]
    row = lax.broadcasted_iota(jnp.int32, x.shape, 0)
    up = jnp.where(row == 0, prev, pltpu.roll(x, 1, 0))
    dn = jnp.where(row == n - 1, nxt, pltpu.roll(x, n - 1, 0))
    return up * w_ref[0:1, :] + x * w_ref[1:2, :] + dn * w_ref[2:3, :] + b_ref[...]


def _cumsum(x, axis):
    n = x.shape[axis]
    idx = lax.broadcasted_iota(jnp.int32, x.shape, axis)
    s = 1
    while s < n:
        x = x + jnp.where(idx >= s, pltpu.roll(x, s, axis), 0.0)
        s *= 2
    return x


def _expand_heads(c, exp_ref):
    hi = c.astype(BF16)
    lo = (c - hi.astype(F32)).astype(BF16)
    return jnp.dot(jnp.concatenate([hi, lo], axis=1), exp_ref[...], preferred_element_type=F32)


def _state_recurrence(hf_ref, hb_ref, decf_ref, decb_ref, nc, h0f, h0b):
    hf_ref[0] = h0f
    hb_ref[nc] = h0b

    def fwd(c, carry):
        hf_ref[c + 1] = decf_ref[c] * hf_ref[c] + hf_ref[c + 1]
        return carry

    lax.fori_loop(0, nc, fwd, 0)

    def bwd(k, carry):
        c = nc - 1 - k
        hb_ref[c] = decb_ref[c] * hb_ref[c + 1] + hb_ref[c]
        return carry

    lax.fori_loop(0, nc, bwd, 0)
    return hf_ref[nc], hb_ref[0]


def _split3(x):
    hi = x.astype(BF16)
    r = x - hi.astype(F32)
    mid = r.astype(BF16)
    lo = (r - mid.astype(F32)).astype(BF16)
    return jnp.concatenate([hi, mid, lo], axis=1)


def _tile_heads_bd(x, bdmask):
    xb = x.astype(BF16)
    return jnp.where(bdmask, jnp.concatenate([xb] * N_HEADS, axis=0), 0)


def _rows_to_lanes(a, lo):
    return jnp.concatenate([a[lo + h:lo + h + 1, :] for h in range(N_HEADS)], axis=1)


def _ssd_sequence(L, z_ref, xbc_ref, dt_ref, y_ref, prm, scr, h0f, h0b):
    (cw_ref, cb_ref, alog_row, bias_row, dskip_ref, ng_ref, exp_ref, exp128_ref, sel_ref, tri_ref) = prm
    (xs_s, bc_s, ee_s, acs_s, dt_s, hf_s, hb_s, decf_s, decb_s) = scr
    nc = L // T
    lane = lax.broadcasted_iota(jnp.int32, (T, 128), 1)
    a_row = -jnp.exp(alog_row[...])

    grp = min(P3_GROUP, nc)

    def phase1(p, carry):
        cs = [p * grp + j for j in range(grp)]
        rows = [pl.ds(pl.multiple_of(c * T, T), T) for c in cs]
        pre = []
        for r in rows:
            dt = jax.nn.softplus(dt_ref[0, r, :] + bias_row[...])
            dt_s[r, :] = dt
            la = dt * a_row
            hi = la.astype(BF16)
            r1 = la - hi.astype(F32)
            mid = r1.astype(BF16)
            lo = (r1 - mid.astype(F32)).astype(BF16)
            acs_f = jnp.dot(tri_ref[...], jnp.concatenate([hi, mid, lo], axis=0),
                            preferred_element_type=F32)
            pre.append((dt, la, acs_f))
        st = []
        for c, r, (dt, la, acs_f) in zip(cs, rows, pre):
            prev, nxt = _halo_rows(xbc_ref, pl.multiple_of(c * T, T), c, nc, L)
            xact = _silu(_dwconv_chunk(xbc_ref[0, r, :].astype(F32), prev, nxt, cw_ref, cb_ref))
            xs_s[r, :] = xact[:, 0:W_GRP]
            bc_s[r, :] = xact[:, W_GRP:SSD_XBC]
            tot = acs_f[T - 1:T, :]
            acs = jnp.where(lane < N_HEADS, acs_f, tot - acs_f + la)
            acs_s[r, :] = acs
            st.append((xact, _expand_heads(jnp.exp(acs), exp_ref),
                       _expand_heads(dt * jnp.exp(tot - acs), exp_ref)))
        for c, r, (xact, ee, wx) in zip(cs, rows, st):
            xs = xact[:, 0:W_GRP]
            bm = xact[:, W_GRP:W_GRP + SSD_STATE]
            ee_s[r, :] = ee
            hf_s[c + 1] = _dot_tn(bm, xs * wx[:, 0:W_GRP])
            hb_s[c] = _dot_tn(bm, xs * wx[:, W_GRP:2 * W_GRP])
            decf_s[c] = ee[T - 1:T, 0:W_GRP]
            decb_s[c] = ee[0:1, W_GRP:2 * W_GRP]
        return carry

    lax.fori_loop(0, nc // grp, phase1, 0)
    hf_fin, hb_fin = _state_recurrence(hf_s, hb_s, decf_s, decb_s, nc, h0f, h0b)
    if y_ref is None:
        return hf_fin, hb_fin

    ri = lax.broadcasted_iota(jnp.int32, (T, N_HEADS * T), 0)
    ci = lax.broadcasted_iota(jnp.int32, (T, N_HEADS * T), 1) % T
    strict_lower = ci < ri
    diag = ci == ri
    r4 = lax.broadcasted_iota(jnp.int32, (N_HEADS * T, W_GRP), 0) // T
    c4 = lax.broadcasted_iota(jnp.int32, (N_HEADS * T, W_GRP), 1) // HEAD_DIM
    bdmask = r4 == c4
    nl = N_HEADS * T

    nt = (((1,), (1,)), ((), ()))

    def phase3(p, carry):
        cs = [p * grp + j for j in range(grp)]
        rows = [pl.ds(pl.multiple_of(c * T, T), T) for c in cs]
        st = []
        for c, r in zip(cs, rows):
            bc = bc_s[r, :]
            bm = bc[:, 0:SSD_STATE]
            cm = bc[:, SSD_STATE:2 * SSD_STATE]
            a3 = _split3(acs_s[r, :])
            col = jnp.dot(a3, exp128_ref[...], preferred_element_type=F32)
            acst = lax.dot_general(sel_ref[...], a3, nt, preferred_element_type=F32)
            dtt = lax.dot_general(sel_ref[...], _split3(dt_s[r, :]), nt,
                                  preferred_element_type=F32)
            inter = (_dot(cm, hf_s[c]), _dot(cm, hb_s[c + 1]))
            st.append((col, acst, dtt, _dot_nt(cm, bm), inter))
        ys = []
        for r, (col, acst, dtt, g, inter) in zip(rows, st):
            g4 = jnp.concatenate([g] * N_HEADS, axis=1)
            shifted = acst - jnp.log(dtt)
            arg = jnp.where(strict_lower, col[:, 0:nl] - _rows_to_lanes(shifted, 0),
                            col[:, nl:2 * nl] - _rows_to_lanes(shifted, N_HEADS))
            w = g4 * (jnp.exp(arg) + jnp.where(diag, _rows_to_lanes(dtt, 0), 0.0))
            ys.append(jnp.dot(w.astype(BF16), _tile_heads_bd(xs_s[r, :], bdmask),
                              preferred_element_type=F32))
        for r, y, (_, _, _, _, inter) in zip(rows, ys, st):
            ee = ee_s[r, :]
            xs = xs_s[r, :]
            y = y + ee[:, 0:W_GRP] * inter[0] + ee[:, W_GRP:2 * W_GRP] * inter[1]
            y = y + xs * dskip_ref[...]
            y = y * _silu(z_ref[0, r, :].astype(F32))
            y = y * lax.rsqrt(jnp.mean(y * y, axis=-1, keepdims=True) + EPS) * ng_ref[...]
            y_ref[0, r, :] = y.astype(y_ref.dtype)
        return carry

    lax.fori_loop(0, nc // grp, phase3, 0)
    return hf_fin, hb_fin


def _ssd_kernel(Lc, L, ctx_out, zc_ref, xbcc_ref, dtc_ref, zl_ref, xbcl_ref, dtl_ref,
                cw_ref, cb_ref, alog_row, bias_row, dskip_ref, ng_ref, exp_ref, exp128_ref, sel_ref,
                tri_ref, *rest):
    yc_ref, yl_ref, scr = (rest[0], rest[1], rest[2:]) if ctx_out else (None, rest[0], rest[1:])
    prm = (cw_ref, cb_ref, alog_row, bias_row, dskip_ref, ng_ref, exp_ref, exp128_ref, sel_ref, tri_ref)
    zero = jnp.zeros((SSD_STATE, W_GRP), F32)
    hf, hb = _ssd_sequence(Lc, zc_ref, xbcc_ref, dtc_ref, yc_ref, prm, scr, zero, zero)
    _ssd_sequence(L, zl_ref, xbcl_ref, dtl_ref, yl_ref, prm, scr, hf, hb)


@functools.lru_cache(maxsize=None)
def _ssd_tables():
    exp64 = np.zeros((256, 512), np.float32)
    exp128 = np.zeros((384, 2 * N_HEADS * T), np.float32)
    sel = np.zeros((8, 384), np.float32)
    for r in range(2):
        for h in range(N_HEADS):
            m = r * N_HEADS + h
            for part in range(2):
                exp64[128 * part + m, r * 256 + 64 * h:r * 256 + 64 * (h + 1)] = 1.0
            for part in range(3):
                exp128[128 * part + m, (r * N_HEADS + h) * T:(r * N_HEADS + h + 1) * T] = 1.0
                sel[m, 128 * part + m] = 1.0
    tri = np.tile(np.tril(np.ones((T, T), np.float32)), (1, 3))
    return exp64, exp128, sel, tri


def _pad_row(v, n=128):
    v = v.reshape(1, -1)
    return jnp.pad(v, ((0, 0), (0, n - v.shape[1])))


def _ssd_call(uc, ul, conv_w, conv_b, a_log, dt_bias, d_skip, norm_g, ctx_out):
    zc, xbcc, dtc = uc
    zl, xbcl, dtl = ul
    bsz, Lc, _ = zc.shape
    L = zl.shape[1]
    nc = L // T
    alog_row = _pad_row(a_log)
    bias_row = _pad_row(dt_bias)
    dskip = jnp.repeat(d_skip, HEAD_DIM).reshape(1, W_GRP)
    exp64, exp128, sel, tri = (jnp.asarray(t).astype(BF16) for t in _ssd_tables())

    def seq(Lx, w):
        return pl.BlockSpec((1, Lx, w), lambda b: (b, 0, 0))

    def small(shape):
        return pl.BlockSpec(shape, lambda b: (0,) * len(shape))

    in_specs = [seq(Lc, W_GRP), seq(Lc, SSD_XBC), seq(Lc, 128),
                seq(L, W_GRP), seq(L, SSD_XBC), seq(L, 128),
                small((3, SSD_XBC)), small((1, SSD_XBC)), small((1, 128)),
                small((1, 128)), small((1, W_GRP)), small((1, W_GRP)),
                small(exp64.shape), small(exp128.shape), small(sel.shape), small(tri.shape)]
    scratch = [pltpu.VMEM((L, W_GRP), F32), pltpu.VMEM((L, 128), F32), pltpu.VMEM((L, 512), F32),
               pltpu.VMEM((L, 128), F32), pltpu.VMEM((L, 128), F32),
               pltpu.VMEM((nc + 1, SSD_STATE, W_GRP), F32), pltpu.VMEM((nc + 1, SSD_STATE, W_GRP), F32),
               pltpu.VMEM((nc, 1, W_GRP), F32), pltpu.VMEM((nc, 1, W_GRP), F32)]
    return pl.pallas_call(
        functools.partial(_ssd_kernel, Lc, L, ctx_out),
        grid=(bsz,),
        in_specs=in_specs,
        out_specs=([seq(Lc, W_GRP)] if ctx_out else []) + [seq(L, W_GRP)],
        out_shape=([jax.ShapeDtypeStruct((bsz, Lc, W_GRP), BF16)] if ctx_out else [])
                  + [jax.ShapeDtypeStruct((bsz, L, W_GRP), BF16)],
        scratch_shapes=scratch,
        compiler_params=_cparams(1),
        name="ssd_mixer",
    )(zc, xbcc, dtc, zl, xbcl, dtl, conv_w, conv_b.reshape(1, -1),
      alog_row, bias_row, dskip, norm_g.reshape(1, -1), exp64, exp128, sel, tri)


def _ret_sequence(L, rope, q_ref, k_ref, v_ref, g_ref, y_ref, cos_ref, sin_ref, perm_ref, avg_ref,
                  tabs, scr, h0f, h0b):
    (ef, eb, wf, wb, decf, decb, bdmask) = tabs
    (qr_s, kr_s, dm_s, hf_s, hb_s) = scr
    nc = L // T
    scale = HEAD_DIM ** -0.5
    r4 = lax.broadcasted_iota(jnp.int32, (N_HEADS * T, W_GRP), 0) // T
    c4 = lax.broadcasted_iota(jnp.int32, (N_HEADS * T, W_GRP), 1) // HEAD_DIM
    stackmask = r4 == c4

    def rot(x, rows):
        partner = jnp.dot(x, perm_ref[...], preferred_element_type=F32)
        return x.astype(F32) * cos_ref[rows, :] + partner * sin_ref[rows, :]

    grp = min(P3_GROUP, nc)

    def phase1(p, carry):
        cs = [p * grp + j for j in range(grp)]
        rows = [pl.ds(pl.multiple_of(c * T, T), T) for c in cs]
        qk = [(q_ref[0, r, :], k_ref[0, r, :]) for r in rows]
        if rope:
            qk = [(rot(q, r), rot(k, r)) for (q, k), r in zip(qk, rows)]
        for c, r, (q, k) in zip(cs, rows, qk):
            k = k.astype(F32) * scale
            qr_s[r, :] = q.astype(F32)
            kr_s[r, :] = k
            v = v_ref[0, r, :]
            hf_s[c + 1] = _dot_tn(k * wf, v) * bdmask
            hb_s[c] = _dot_tn(k * wb, v) * bdmask
        return carry

    lax.fori_loop(0, nc // grp, phase1, 0)

    hf_s[0] = h0f
    hb_s[nc] = h0b

    def fwd(c, carry):
        hf_s[c + 1] = decf * hf_s[c] + hf_s[c + 1]
        return carry

    lax.fori_loop(0, nc, fwd, 0)

    def bwd(kk, carry):
        c = nc - 1 - kk
        hb_s[c] = decb * hb_s[c + 1] + hb_s[c]
        return carry

    lax.fori_loop(0, nc, bwd, 0)
    if y_ref is None:
        return hf_s[nc], hb_s[0]

    def gmean(x):
        hi = x.astype(BF16)
        lo = (x - hi.astype(F32)).astype(BF16)
        return jnp.dot(jnp.concatenate([hi, lo], axis=1), avg_ref[...], preferred_element_type=F32)

    def gmean_sq(x):
        return jnp.dot((x * x).astype(BF16), avg_ref[0:W_GRP, :], preferred_element_type=F32)

    def phase3(p, carry):
        cs = [p * grp + j for j in range(grp)]
        rows = [pl.ds(pl.multiple_of(c * T, T), T) for c in cs]
        qs = [qr_s[r, :] for r in rows]
        scs = [lax.dot_general(q.astype(BF16), _tile_heads_bd(kr_s[r, :], stackmask),
                               (((1,), (1,)), ((), ())), preferred_element_type=F32)
               for q, r in zip(qs, rows)]
        inters = [_dot(q * ef, hf_s[c]) + _dot(q * eb, hb_s[c + 1]) for q, c in zip(qs, cs)]
        ys = [jnp.dot((sc * dm_s[...]).astype(BF16), _tile_heads_bd(v_ref[0, r, :], stackmask),
                      preferred_element_type=F32) + it
              for sc, r, it in zip(scs, rows, inters)]
        ycs = [y - gmean(y) for y in ys]
        vars_ = [gmean_sq(yc) for yc in ycs]
        for r, yc, var in zip(rows, ycs, vars_):
            y_ref[0, r, :] = (_silu(g_ref[0, r, :].astype(F32))
                              * (yc * lax.rsqrt(var + EPS))).astype(y_ref.dtype)
        return carry

    lax.fori_loop(0, nc // grp, phase3, 0)
    return hf_s[nc], hb_s[0]


def _ret_kernel(Lc, L, ctx_out, qc_ref, kc_ref, vc_ref, gc_ref, ql_ref, kl_ref, vl_ref, gl_ref,
                cos_ref, sin_ref, dec_ref, perm_ref, avg_ref, *rest):
    yc_ref, rest = (rest[0], rest[1:]) if ctx_out else (None, rest)
    yl_ref, qr_s, kr_s, dm_s, hf_s, hb_s = rest
    lg = -jnp.exp(dec_ref[...])
    lgf = lg[0:1, :]
    lgb = lg[1:2, :]
    i = lax.broadcasted_iota(jnp.int32, (T, W_GRP), 0).astype(F32)
    ef = jnp.exp(lgf * (i + 1.0))
    eb = jnp.exp(lgb * (T - i))
    wf = jnp.exp(lgf * (T - 1.0 - i))
    wb = jnp.exp(lgb * i)
    decf = jnp.exp(lgf * float(T))
    decb = jnp.exp(lgb * float(T))
    r2 = lax.broadcasted_iota(jnp.int32, (W_GRP, W_GRP), 0) // HEAD_DIM
    c2 = lax.broadcasted_iota(jnp.int32, (W_GRP, W_GRP), 1) // HEAD_DIM
    bdmask = (r2 == c2).astype(F32)
    ri = lax.broadcasted_iota(jnp.int32, (T, T), 0)
    ci = lax.broadcasted_iota(jnp.int32, (T, T), 1)
    d = (ri - ci).astype(F32)
    for h in range(N_HEADS):
        lf = lgf[:, HEAD_DIM * h:HEAD_DIM * h + 1]
        lb = lgb[:, HEAD_DIM * h:HEAD_DIM * h + 1]
        dm_s[:, T * h:T * (h + 1)] = (jnp.exp(jnp.where(ci <= ri, lf * d, -jnp.inf))
                                      + jnp.exp(jnp.where(ci >= ri, -lb * d, -jnp.inf)))
    tabs = (ef, eb, wf, wb, decf, decb, bdmask)
    scr = (qr_s, kr_s, dm_s, hf_s, hb_s)
    zero = jnp.zeros((W_GRP, W_GRP), F32)
    hf, hb = _ret_sequence(Lc, False, qc_ref, kc_ref, vc_ref, gc_ref, yc_ref, cos_ref, sin_ref,
                           perm_ref, avg_ref, tabs, scr, zero, zero)
    _ret_sequence(L, True, ql_ref, kl_ref, vl_ref, gl_ref, yl_ref, cos_ref, sin_ref,
                  perm_ref, avg_ref, tabs, scr, hf, hb)


@functools.lru_cache(maxsize=None)
def _rope_tables(L):
    t = np.arange(L)
    f = 16
    inv = (ROPE_BASE ** (-np.arange(f, dtype=np.float32) / f)).astype(np.float32)
    cos = np.zeros((L, HEAD_DIM), np.float32)
    sin = np.zeros((L, HEAD_DIM), np.float32)
    for base, pos in ((0, t // GRID_W), (32, t % GRID_W)):
        ang = pos.astype(np.float32)[:, None] * inv[None, :]
        ang = ang.astype(np.float32).astype(np.float64)
        cos[:, base:base + f] = np.cos(ang)
        cos[:, base + f:base + 2 * f] = np.cos(ang)
        sin[:, base:base + f] = -np.sin(ang)
        sin[:, base + f:base + 2 * f] = np.sin(ang)
    return np.tile(cos, (1, N_HEADS)), np.tile(sin, (1, N_HEADS))


@functools.lru_cache(maxsize=None)
def _ret_tables():
    perm = np.zeros((W_GRP, W_GRP), np.float32)
    avg = np.zeros((2 * W_GRP, W_GRP), np.float32)
    for l in range(W_GRP):
        src = l + 16 if (l % 32) < 16 else l - 16
        perm[src, l] = 1.0
        g = l // HEAD_DIM
        for part in range(2):
            avg[part * W_GRP + g * HEAD_DIM:part * W_GRP + (g + 1) * HEAD_DIM, l] = 1.0 / HEAD_DIM
    return perm, avg


def _ret_call(uc, ul, decay_param, ctx_out):
    bsz, Lc, _ = uc.shape
    L = ul.shape[1]
    nc = L // T
    cos, sin = _rope_tables(L)
    perm, avg = (jnp.asarray(t).astype(BF16) for t in _ret_tables())
    dec = jnp.repeat(decay_param, HEAD_DIM, axis=1)

    def col(Lx, j):
        return pl.BlockSpec((1, Lx, W_GRP), lambda b, j=j: (b, 0, j))

    def seq(Lx):
        return pl.BlockSpec((1, Lx, W_GRP), lambda b: (b, 0, 0))

    in_specs = ([col(Lc, j) for j in range(4)] + [col(L, j) for j in range(4)]
                + [pl.BlockSpec((L, W_GRP), lambda b: (0, 0)), pl.BlockSpec((L, W_GRP), lambda b: (0, 0)),
                   pl.BlockSpec((2, W_GRP), lambda b: (0, 0)),
                   pl.BlockSpec((W_GRP, W_GRP), lambda b: (0, 0)),
                   pl.BlockSpec((2 * W_GRP, W_GRP), lambda b: (0, 0))])
    scratch = [pltpu.VMEM((L, W_GRP), F32), pltpu.VMEM((L, W_GRP), F32),
               pltpu.VMEM((T, N_HEADS * T), F32),
               pltpu.VMEM((nc + 1, W_GRP, W_GRP), F32), pltpu.VMEM((nc + 1, W_GRP, W_GRP), F32)]
    return pl.pallas_call(
        functools.partial(_ret_kernel, Lc, L, ctx_out),
        grid=(bsz,),
        in_specs=in_specs,
        out_specs=([seq(Lc)] if ctx_out else []) + [seq(L)],
        out_shape=([jax.ShapeDtypeStruct((bsz, Lc, W_GRP), BF16)] if ctx_out else [])
                  + [jax.ShapeDtypeStruct((bsz, L, W_GRP), BF16)],
        scratch_shapes=scratch,
        compiler_params=_cparams(1),
        name="retention_mixer",
    )(uc, uc, uc, uc, ul, ul, ul, ul, jnp.asarray(cos), jnp.asarray(sin), dec, perm, avg)


def _s5_prep_kernel(are_ref, aim_ref, ldt_ref, bre_ref, bim_ref, cre_ref, cim_ref,
                    bmat_ref, cmat_ref, ab_ref):
    a_re = are_ref[0]
    a_im = aim_ref[0]
    dt = jnp.exp(ldt_ref[0])
    mag = jnp.exp(a_re * dt)
    ab_re = mag * jnp.cos(a_im * dt)
    ab_im = mag * jnp.sin(a_im * dt)
    den = a_re * a_re + a_im * a_im
    z_re = ((ab_re - 1.0) * a_re + ab_im * a_im) / den
    z_im = (ab_im * a_re - (ab_re - 1.0) * a_im) / den
    b_re = bre_ref[...]
    b_im = bim_ref[...]
    bmat_ref[0, :, 0:S5_NS] = (b_re * z_re - b_im * z_im).astype(BF16)
    bmat_ref[0, :, S5_NS:2 * S5_NS] = (b_re * z_im + b_im * z_re).astype(BF16)
    cmat_ref[0, 0:S5_NS, :] = cre_ref[0].astype(BF16)
    cmat_ref[0, S5_NS:2 * S5_NS, :] = (-cim_ref[0]).astype(BF16)
    ab_ref[0, :, 0:S5_NS] = ab_re
    ab_ref[0, :, S5_NS:2 * S5_NS] = ab_im


def _s5_prep_call(a_re, a_im, log_dt, b_re, b_im, c_re, c_im):
    eye = jnp.eye(S5_GROUPS, dtype=F32)
    b_re_bd = jnp.einsum('gpc,gh->gchp', b_re, eye).reshape(W_GRP, S5_NS)
    b_im_bd = jnp.einsum('gpc,gh->gchp', b_im, eye).reshape(W_GRP, S5_NS)
    c_re_bd = jnp.einsum('rgcp,gh->rgphc', c_re, eye).reshape(2, S5_NS, W_GRP)
    c_im_bd = jnp.einsum('rgcp,gh->rgphc', c_im, eye).reshape(2, S5_NS, W_GRP)
    ldt = jnp.repeat(log_dt, S5_STATE, axis=1).reshape(2, 1, S5_NS)
    row = lambda: pl.BlockSpec((1, 1, S5_NS), lambda r: (r, 0, 0))
    return pl.pallas_call(
        _s5_prep_kernel,
        grid=(2,),
        in_specs=[row(), row(), row(),
                  pl.BlockSpec((W_GRP, S5_NS), lambda r: (0, 0)), pl.BlockSpec((W_GRP, S5_NS), lambda r: (0, 0)),
                  pl.BlockSpec((1, S5_NS, W_GRP), lambda r: (r, 0, 0)),
                  pl.BlockSpec((1, S5_NS, W_GRP), lambda r: (r, 0, 0))],
        out_specs=[pl.BlockSpec((1, W_GRP, 2 * S5_NS), lambda r: (r, 0, 0)),
                   pl.BlockSpec((1, 2 * S5_NS, W_GRP), lambda r: (r, 0, 0)),
                   pl.BlockSpec((1, 1, 2 * S5_NS), lambda r: (r, 0, 0))],
        out_shape=[jax.ShapeDtypeStruct((2, W_GRP, 2 * S5_NS), BF16),
                   jax.ShapeDtypeStruct((2, 2 * S5_NS, W_GRP), BF16),
                   jax.ShapeDtypeStruct((2, 1, 2 * S5_NS), F32)],
        compiler_params=_cparams(1),
        name="s5_prep",
    )(a_re.reshape(2, 1, S5_NS), a_im.reshape(2, 1, S5_NS), ldt, b_re_bd, b_im_bd, c_re_bd, c_im_bd)


def _s5_dir_kernel(bsz, n, reverse, u_ref, h0_ref, bmat_ref, cmat_ref, ab_ref, *rest):
    if reverse:
        y_ref, hs_ref, x_s, hb_s = rest
    else:
        ul_ref, yb_ref, d_ref, gw_ref, gb_ref, y_ref, hs_ref, x_s, hb_s = rest
    step = pl.program_id(0)

    @pl.when(step == 0)
    def _():
        hs_ref[...] = h0_ref[...]
        hb_s[...] = jnp.zeros_like(hb_s)

    u = u_ref[...].reshape(S5_TS * bsz, W_GRP)
    x_s[...] = jnp.dot(u, bmat_ref[0], preferred_element_type=F32)
    y = jnp.dot(hb_s[...], cmat_ref[0], preferred_element_type=F32)
    if not reverse:
        y = (y + yb_ref[...].reshape(S5_TS * bsz, W_GRP)
             + ul_ref[...].reshape(S5_TS * bsz, W_GRP).astype(F32) * d_ref[...])
        y = jax.nn.gelu(y)
        y = y * jax.nn.sigmoid(jnp.dot(y.astype(BF16), gw_ref[...], preferred_element_type=F32)
                               + gb_ref[...])
    y_ref[...] = y.astype(y_ref.dtype).reshape(S5_TS, bsz, W_GRP)

    @pl.when(step < n)
    def _():
        _s5_scan_block(bsz, reverse, ab_ref, hs_ref, x_s, hb_s)


def _s5_scan_block(bsz, reverse, ab_ref, hs_ref, x_s, hb_s):
    for q in range(S5_NS // W_GRP):
        cr = slice(W_GRP * q, W_GRP * (q + 1))
        cim = slice(S5_NS + W_GRP * q, S5_NS + W_GRP * (q + 1))
        a_re = jnp.broadcast_to(ab_ref[0, :, cr], (bsz, W_GRP))
        a_im = jnp.broadcast_to(ab_ref[0, :, cim], (bsz, W_GRP))

        def body(kk, carry, cr=cr, cim=cim, a_re=a_re, a_im=a_im):
            h_re, h_im = carry
            t = (S5_TS - 1 - kk) if reverse else kk
            rows = pl.ds(pl.multiple_of(t * bsz, bsz), bsz)
            n_re = a_re * h_re - a_im * h_im + x_s[rows, cr]
            n_im = a_re * h_im + a_im * h_re + x_s[rows, cim]
            hb_s[rows, cr] = n_re.astype(BF16)
            hb_s[rows, cim] = n_im.astype(BF16)
            return n_re, n_im

        h_re, h_im = lax.fori_loop(0, S5_TS, body, (hs_ref[:, cr], hs_ref[:, cim]), unroll=4)
        hs_ref[:, cr] = h_re
        hs_ref[:, cim] = h_im


def _s5_dir_call(r, u3, h0, bmat, cmat, ab, post=None):
    L, bsz, _ = u3.shape
    n = L // S5_TS
    blk = (S5_TS, bsz, W_GRP)
    reverse = r == 1

    def block_of(c):
        return n - 1 - c if reverse else c

    cur = lambda i: (block_of(jnp.minimum(i, n - 1)), 0, 0)
    lag = lambda i: (block_of(jnp.maximum(i - 1, 0)), 0, 0)
    state = pl.BlockSpec((bsz, 2 * S5_NS), lambda i: (0, 0))
    in_specs = [pl.BlockSpec(blk, cur), state,
                pl.BlockSpec((1, W_GRP, 2 * S5_NS), lambda i: (r, 0, 0)),
                pl.BlockSpec((1, 2 * S5_NS, W_GRP), lambda i: (r, 0, 0)),
                pl.BlockSpec((1, 1, 2 * S5_NS), lambda i: (r, 0, 0))]
    args = [u3, h0, bmat, cmat, ab]
    if not reverse:
        yb, d, glu_w, glu_b = post
        in_specs += [pl.BlockSpec(blk, lag), pl.BlockSpec(blk, lag),
                     pl.BlockSpec((1, W_GRP), lambda i: (0, 0)),
                     pl.BlockSpec((W_GRP, W_GRP), lambda i: (0, 0)),
                     pl.BlockSpec((1, W_GRP), lambda i: (0, 0))]
        args += [u3, yb, d.reshape(1, -1), glu_w.astype(BF16), glu_b.reshape(1, -1)]
    return pl.pallas_call(
        functools.partial(_s5_dir_kernel, bsz, n, reverse),
        grid=(n + 1,),
        in_specs=in_specs,
        out_specs=[pl.BlockSpec(blk, lag), state],
        out_shape=[jax.ShapeDtypeStruct((L, bsz, W_GRP), F32 if reverse else BF16),
                   jax.ShapeDtypeStruct((bsz, 2 * S5_NS), F32)],
        scratch_shapes=[pltpu.VMEM((S5_TS * bsz, 2 * S5_NS), F32),
                        pltpu.VMEM((S5_TS * bsz, 2 * S5_NS), BF16)],
        compiler_params=_cparams(1),
        name="s5_bwd" if reverse else "s5_fwd",
    )(*args)


def _s5_mixer(u5c, u5l, bsz, p):
    bmat, cmat, ab = _s5_prep_call(p['s5_a_re'], p['s5_a_im'], p['s5_log_dt'], p['s5_b_re'],
                                   p['s5_b_im'], p['s5_c_re'], p['s5_c_im'])
    Lc, L = u5c.shape[0], u5l.shape[0]
    u3c = u5c.reshape(Lc, bsz, W_GRP)
    u3l = u5l.reshape(L, bsz, W_GRP)
    h0 = jnp.zeros((bsz, 2 * S5_NS), F32)
    head = (p['s5_d'], p['s5_glu_w'], p['s5_glu_b'])
    ybc, hbc = _s5_dir_call(1, u3c, h0, bmat, cmat, ab)
    ybl, _ = _s5_dir_call(1, u3l, hbc, bmat, cmat, ab)
    s5c, hfc = _s5_dir_call(0, u3c, h0, bmat, cmat, ab, post=(ybc,) + head)
    s5l, _ = _s5_dir_call(0, u3l, hfc, bmat, cmat, ab, post=(ybl,) + head)
    return s5c.reshape(Lc, bsz * W_GRP), s5l.reshape(L, bsz * W_GRP)


@functools.lru_cache(maxsize=None)
def _dft_tables(L):
    k = np.arange(L, dtype=np.int64)
    ft = (k[:, None] * k[None, :]) % (2 * L)
    ang = ft.astype(np.float64) * (math.pi / L)
    return np.cos(ang).astype(np.float32), np.sin(ang).astype(np.float32)


@functools.lru_cache(maxsize=None)
def _dft_tables_split(L):
    c, s = _dft_tables(L)
    return (np.concatenate([c[0::2], c[1::2]], axis=0), np.concatenate([s[0::2], s[1::2]], axis=0))


@functools.lru_cache(maxsize=None)
def _hyena_consts(L):
    t = np.linspace(0.0, 1.0, L, dtype=np.float32)[:, None]
    w = (2.0 * math.pi * np.arange(L, dtype=np.float32)[:, None] / L).astype(np.float32)
    bands = np.linspace(1e-4, HY_BANDS - 1, HY_BANDS, dtype=np.float32)[None, :]
    bw = (bands * w).astype(np.float32).astype(np.float64)
    feats = np.zeros((L, 128), np.float32)
    feats[:, 0:1] = t
    feats[:, 1:1 + HY_BANDS] = np.cos(bw)
    feats[:, 1 + HY_BANDS:HY_EMB] = -np.sin(bw)
    max_decay = math.log(1e-2) / 0.3
    min_decay = math.log(1e-2) / 1.5
    deltas = np.abs(np.linspace(min_decay, max_decay, 4 * W_GRP, dtype=np.float32))[None, :]
    return feats, deltas.astype(np.float32)


HY_RB = 256


def _hy_filter_kernel(feats_ref, w1_ref, b1_ref, fr_ref, w2_ref, b2_ref, w3_ref, del_ref,
                      p_ref, q_ref, nrm_ref, an_ref):
    i = pl.program_id(0)
    feats = feats_ref[...]
    fr = fr_ref[...]
    h = jnp.sin(fr * (_dot_f32(feats, w1_ref[...]) + b1_ref[...]))
    h = jnp.sin(fr * (_dot_f32(h, w2_ref[...]) + b2_ref[...]))
    h = _dot_f32(h, w3_ref[...])
    h = h * jnp.exp(-feats[:, 0:1] * del_ref[...])
    half = 2 * W_GRP
    hf = h[:, 0:half]
    row = lax.broadcasted_iota(jnp.int32, (HY_RB, half), 0) + i * HY_RB
    hb = jnp.where(row == 0, 0.0, h[:, half:2 * half])
    p = hf + hb
    sign = (1 - 2 * (row % 2)).astype(F32)
    p_ref[...] = p.astype(BF16)
    q_ref[...] = (hb - hf).astype(BF16)

    @pl.when(i == 0)
    def _():
        nrm_ref[...] = jnp.full_like(nrm_ref, EPS)
        an_ref[...] = jnp.zeros_like(an_ref)

    nrm_ref[...] += (jnp.sum(jnp.abs(hf), axis=0, keepdims=True)
                     + jnp.sum(jnp.abs(hb), axis=0, keepdims=True))
    an_ref[...] += jnp.sum(p * sign, axis=0, keepdims=True)


def _hy_spectrum_kernel(L, c_ref, s_ref, p_ref, q_ref, nrm_ref, ans_ref, a_ref, bc_ref, an_ref):
    i = pl.program_id(0)
    n = 2.0 * L
    inv = 1.0 / nrm_ref[...]
    row = lax.broadcasted_iota(jnp.int32, a_ref.shape, 0) + i * HY_RB
    wv = jnp.where(row == 0, 1.0 / n, 2.0 / n) * inv
    a_ref[...] = wv * jnp.dot(c_ref[...], p_ref[...], preferred_element_type=F32)
    bc_ref[...] = wv * jnp.dot(s_ref[...], q_ref[...], preferred_element_type=F32)
    an_ref[...] = ans_ref[...] * inv / n


def _hy_filter_call(L, w1, b1, freq, w2, b2, w3, cmat, smat):
    feats, deltas = _hyena_consts(L)
    w1p = jnp.pad(w1, ((0, 128 - HY_EMB), (0, 0)))
    half = 2 * W_GRP
    nb = L // HY_RB
    full = lambda a: pl.BlockSpec(a.shape, lambda i, nd=a.ndim: (0,) * nd)
    small = (w1p, b1.reshape(1, -1), freq.reshape(1, -1), w2, b2.reshape(1, -1), w3, jnp.asarray(deltas))
    rowblk = lambda w: pl.BlockSpec((HY_RB, w), lambda i: (i, 0))
    vec = pl.BlockSpec((1, half), lambda i: (0, 0))
    p, q, nrm, ans = pl.pallas_call(
        _hy_filter_kernel,
        grid=(nb,),
        in_specs=[rowblk(128)] + [full(a) for a in small],
        out_specs=[rowblk(half), rowblk(half), vec, vec],
        out_shape=[jax.ShapeDtypeStruct((L, half), BF16), jax.ShapeDtypeStruct((L, half), BF16),
                   jax.ShapeDtypeStruct((1, half), F32), jax.ShapeDtypeStruct((1, half), F32)],
        compiler_params=_cparams(1),
        name="hyena_filter",
    )(jnp.asarray(feats), *small)
    return pl.pallas_call(
        functools.partial(_hy_spectrum_kernel, L),
        grid=(nb,),
        in_specs=[rowblk(L), rowblk(L), full(p), full(q), vec, vec],
        out_specs=[rowblk(half), rowblk(half), vec],
        out_shape=[jax.ShapeDtypeStruct((L, half), F32), jax.ShapeDtypeStruct((L, half), F32),
                   jax.ShapeDtypeStruct((1, half), F32)],
        compiler_params=_cparams(1),
        name="hyena_spectrum",
    )(cmat, smat, p, q, nrm, ans)


def _alt_sign(shape):
    return (1 - 2 * (lax.broadcasted_iota(jnp.int32, shape, 0) % 2)).astype(F32)


def _reverse_shift(x, j_ref):
    hi = x.astype(BF16)
    lo = (x - hi.astype(F32)).astype(BF16)
    j = j_ref[...]
    return (jnp.dot(j, hi, preferred_element_type=F32) + jnp.dot(j, lo, preferred_element_type=F32))


def _hy_conv_kernel(L, u_ref, cw_ref, cb_ref, ce_ref, se_ref, co_ref, so_ref, cot_ref, sot_ref, j_ref,
                    a_ref, bc_ref, an_ref, bias_ref, o_ref,
                    x1_s, x2_s, z_s, zs_s, zd_s, xe_s, ye_s, xo_s, yo_s, d_s, acc_s):
    nc = L // T
    H = L // 2
    fb = min(256, H)
    nfb = H // fb
    rb = min(1024, H)
    nrb = H // rb

    def conv(c, carry):
        s = pl.multiple_of(c * T, T)
        rows = pl.ds(s, T)
        prev, nxt = _halo_rows(u_ref, s, c, nc, L)
        y = _dwconv_chunk(u_ref[0, rows, :].astype(F32), prev, nxt, cw_ref, cb_ref)
        x1_s[rows, :] = y[:, 0:W_GRP]
        x2_s[rows, :] = y[:, W_GRP:2 * W_GRP]
        z_s[rows, :] = y[:, 2 * W_GRP:3 * W_GRP]
        return carry

    lax.fori_loop(0, nc, conv, 0)
    row0 = lax.broadcasted_iota(jnp.int32, (fb, W_GRP), 0) == 0

    for o, gate_s in enumerate((x1_s, x2_s)):
        cols = slice(W_GRP * o, W_GRP * (o + 1))
        acc_s[...] = jnp.zeros_like(acc_s)

        ws = [z_s[H + (nfb - 1 - b) * fb:H + (nfb - b) * fb, :] for b in range(nfb)]
        revs = [_reverse_shift(w, j_ref) for w in ws]
        qn = jnp.zeros((1, W_GRP), F32)
        for b in range(nfb):
            top = z_s[b * fb:(b + 1) * fb, :]
            edge = z_s[H + (nfb - b) * fb:H + (nfb - b) * fb + 1, :] if b > 0 else 0.0
            zr = jnp.where(row0, edge, revs[b])
            zs_s[b * fb:(b + 1) * fb, :] = (top + zr).astype(BF16)
            zd_s[b * fb:(b + 1) * fb, :] = (top - zr).astype(BF16)
            qn = qn + jnp.sum((top + ws[b]) * _alt_sign(top.shape), axis=0, keepdims=True)
        acc_s[0:1, :] = qn
        z_mid = z_s[H:H + 1, :]

        def fwd(j, carry, cols=cols, z_mid=z_mid):
            rows = pl.ds(pl.multiple_of(j * rb, rb), rb)
            odd_rows = pl.ds(pl.multiple_of(H + j * rb, rb), rb)
            zs = zs_s[...]
            zd = zd_s[...]
            mid = _alt_sign((rb, W_GRP)) * z_mid
            pe = jnp.dot(ce_ref[rows, :], zs, preferred_element_type=F32) + mid
            qo = jnp.dot(so_ref[rows, :], zs, preferred_element_type=F32) + mid
            po = jnp.dot(co_ref[rows, :], zd, preferred_element_type=F32)
            qe = jnp.dot(se_ref[rows, :], zd, preferred_element_type=F32)
            ae = a_ref[rows, cols]
            bce = bc_ref[rows, cols]
            ao = a_ref[odd_rows, cols]
            bco = bc_ref[odd_rows, cols]
            xe = pe * ae + qe * bce
            yo = qo * ao - po * bco
            xe_s[rows, :] = xe.astype(BF16)
            ye_s[rows, :] = (qe * ae - pe * bce).astype(BF16)
            xo_s[rows, :] = (po * ao + qo * bco).astype(BF16)
            yo_s[rows, :] = yo.astype(BF16)
            acc_s[1:2, :] += jnp.sum((xe + yo) * _alt_sign(xe.shape), axis=0, keepdims=True)
            return carry

        lax.fori_loop(0, nrb, fwd, 0)
        nyq = acc_s[0:1, :] * an_ref[:, cols]

        def finish(rows, y, cols=cols, gate_s=gate_s, nyq=nyq, o=o):
            res = gate_s[rows, :] * (y + _alt_sign(y.shape) * nyq + z_s[rows, :] * bias_ref[:, cols])
            if o == 0:
                z_s[rows, :] = res
            else:
                o_ref[0, rows, :] = res.astype(o_ref.dtype)

        def inv(j, carry, finish=finish):
            rows = pl.ds(pl.multiple_of(j * rb, rb), rb)
            u1 = (jnp.dot(ce_ref[rows, :], xe_s[...], preferred_element_type=F32)
                  + jnp.dot(sot_ref[rows, :], yo_s[...], preferred_element_type=F32))
            u2 = (jnp.dot(cot_ref[rows, :], xo_s[...], preferred_element_type=F32)
                  + jnp.dot(se_ref[rows, :], ye_s[...], preferred_element_type=F32))
            d_s[rows, :] = u1 - u2
            finish(rows, u1 + u2)
            return carry

        lax.fori_loop(0, nrb, inv, 0)
        y_mid = acc_s[1:2, :]

        drevs = [_reverse_shift(d_s[(nfb - 1 - b) * fb:(nfb - b) * fb, :], j_ref) for b in range(nfb)]
        for b in range(nfb):
            edge = d_s[(nfb - b) * fb:(nfb - b) * fb + 1, :] if b > 0 else y_mid
            finish(pl.ds(H + b * fb, fb), jnp.where(row0, edge, drevs[b]))


@functools.lru_cache(maxsize=None)
def _dft_half_tables(L):
    c, s = _dft_tables(L)
    H = L // 2
    fb = min(256, H)
    ce, co = c[0::2, :H], c[1::2, :H]
    se, so = s[0::2, :H], s[1::2, :H]
    j = np.zeros((fb, fb), np.float32)
    for r in range(1, fb):
        j[r, fb - r] = 1.0
    return tuple(np.ascontiguousarray(m) for m in (ce, se, co, so, co.T, so.T, j))


def _hy_conv_call(u, conv_w, conv_b, tables, a, bc, an, bias):
    bsz, L, _ = u.shape
    H = L // 2
    half = 2 * W_GRP
    mats = [jnp.asarray(m).astype(BF16) for m in tables]
    return pl.pallas_call(
        functools.partial(_hy_conv_kernel, L),
        grid=(bsz,),
        in_specs=[pl.BlockSpec((1, L, HY_COLS), lambda b: (b, 0, 0)),
                  pl.BlockSpec((3, HY_COLS), lambda b: (0, 0)),
                  pl.BlockSpec((1, HY_COLS), lambda b: (0, 0))]
                 + [_const_spec(m.shape) for m in mats]
                 + [_const_spec((L, half)), _const_spec((L, half)),
                    pl.BlockSpec((1, half), lambda b: (0, 0)),
                    pl.BlockSpec((1, half), lambda b: (0, 0))],
        out_specs=pl.BlockSpec((1, L, W_GRP), lambda b: (b, 0, 0)),
        out_shape=jax.ShapeDtypeStruct((bsz, L, W_GRP), BF16),
        scratch_shapes=[pltpu.VMEM((L, W_GRP), F32), pltpu.VMEM((L, W_GRP), F32),
                        pltpu.VMEM((L, W_GRP), F32),
                        pltpu.VMEM((H, W_GRP), BF16), pltpu.VMEM((H, W_GRP), BF16),
                        pltpu.VMEM((H, W_GRP), BF16), pltpu.VMEM((H, W_GRP), BF16),
                        pltpu.VMEM((H, W_GRP), BF16), pltpu.VMEM((H, W_GRP), BF16),
                        pltpu.VMEM((H, W_GRP), F32), pltpu.VMEM((8, W_GRP), F32)],
        compiler_params=_cparams(1),
        name="hyena_conv",
    )(u, conv_w, conv_b.reshape(1, -1), *mats, a, bc, an, bias.reshape(1, -1))


def _hyena_mixer(u, p):
    L = u.shape[1]
    cnp, snp = _dft_tables_split(L)
    cmat = jnp.asarray(cnp).astype(BF16)
    smat = jnp.asarray(snp).astype(BF16)
    a, bc, an = _hy_filter_call(L, p['hy_w1'], p['hy_b1'], p['hy_freq'], p['hy_w2'], p['hy_b2'],
                                p['hy_w3'], cmat, smat)
    return _hy_conv_call(u, p['hy_conv_w'], p['hy_conv_b'], _dft_half_tables(L), a, bc, an, p['hy_bias'])


def _out_ffn_kernel(final, h_ref, ssd_ref, hy_ref, ret_ref, s5_ref, mod_ref, g2_ref, wo_ref,
                    wup_ref, wdn_ref, fg_ref, o_ref, acc_s):
    y = jnp.zeros(h_ref.shape[1:], F32)
    for j, blk in enumerate((ssd_ref[0], hy_ref[0], ret_ref[0], s5_ref[...])):
        y = y + jnp.dot(blk.astype(BF16), wo_ref[W_GRP * j:W_GRP * (j + 1), :],
                        preferred_element_type=F32)
    h1 = h_ref[0] + mod_ref[0, 2:3, :] * y
    xn = h1 * lax.rsqrt(jnp.mean(h1 * h1, axis=-1, keepdims=True) + EPS) * g2_ref[...]
    xm = (xn * (1.0 + mod_ref[0, 4:5, :]) + mod_ref[0, 3:4, :]).astype(BF16)
    fc = 256
    for j in range(D_FF // fc):
        gg = jnp.dot(xm, wup_ref[:, fc * j:fc * (j + 1)], preferred_element_type=F32)
        uu = jnp.dot(xm, wup_ref[:, D_FF + fc * j:D_FF + fc * (j + 1)], preferred_element_type=F32)
        part = jnp.dot((_silu(gg) * uu).astype(BF16), wdn_ref[fc * j:fc * (j + 1), :],
                       preferred_element_type=F32)
        if j == 0:
            acc_s[...] = part
        else:
            acc_s[...] += part
    h2 = h1 + mod_ref[0, 5:6, :] * acc_s[...]
    if final:
        h2 = h2 * lax.rsqrt(jnp.mean(h2 * h2, axis=-1, keepdims=True) + EPS) * fg_ref[...]
    o_ref[0] = h2


def _out_ffn_call(h, mix, mods, ctx_stream, g2, wo, wup, wdn, final_g, final):
    bsz, L, _ = h.shape
    tm = min(512, L)
    mod_map = (lambda b, i: (bsz, 0, 0)) if ctx_stream else (lambda b, i: (b, 0, 0))
    tok = lambda w: pl.BlockSpec((1, tm, w), lambda b, i: (b, i, 0))
    return pl.pallas_call(
        functools.partial(_out_ffn_kernel, final),
        grid=(bsz, L // tm),
        in_specs=[tok(D_MODEL), tok(W_GRP), tok(W_GRP), tok(W_GRP),
                  pl.BlockSpec((tm, W_GRP), lambda b, i: (i, b)),
                  pl.BlockSpec((1, 6, D_MODEL), mod_map),
                  pl.BlockSpec((1, D_MODEL), lambda b, i: (0, 0)),
                  _const_spec((D_MODEL, D_MODEL)),
                  _const_spec((D_MODEL, 2 * D_FF)),
                  _const_spec((D_FF, D_MODEL)),
                  pl.BlockSpec((1, D_MODEL), lambda b, i: (0, 0))],
        out_specs=tok(D_MODEL),
        out_shape=jax.ShapeDtypeStruct((bsz, L, D_MODEL), F32),
        scratch_shapes=[pltpu.VMEM((tm, D_MODEL), F32)],
        compiler_params=_cparams(2),
        name="out_ffn",
    )(h, *mix, mods, g2, wo, wup, wdn, final_g)


def kernel(x, c, ctx, c_ctx, mod_w, mod_b, norm1_g, norm2_g, w_in, w_out, ssd_conv_w, ssd_conv_b, ssd_a_log, ssd_dt_bias, ssd_d, ssd_norm_g, hy_conv_w, hy_conv_b, hy_w1, hy_b1, hy_freq, hy_w2, hy_b2, hy_w3, hy_bias, ret_decay, s5_a_re, s5_a_im, s5_log_dt, s5_b_re, s5_b_im, s5_c_re, s5_c_im, s5_d, s5_glu_w, s5_glu_b, ffn_w_up, ffn_w_down, final_norm_g):
    bsz = x.shape[0]
    depth = mod_w.shape[0]
    sc = jnp.concatenate([c, c_ctx[None, :], jnp.zeros((MOD_ROWS - bsz - 1, D_MODEL), F32)], axis=0)
    mods_all = _mod_call(sc, mod_w, mod_b).reshape(depth, MOD_ROWS, 6, D_MODEL)
    fg = final_norm_g.reshape(1, -1)
    h_l, h_c = x, ctx
    o_xbc, o_dt, o_hy = W_GRP, W_GRP + SSD_XBC, W_GRP + SSD_XBC + 2 * N_HEADS
    o_ret = o_hy + HY_COLS
    o_s5 = o_ret + RET_COLS
    for i in range(depth):
        last = i == depth - 1
        wi = w_in[i]
        wdt = wi[:, o_dt:o_hy]
        wcat = jnp.concatenate([wi[:, 0:o_dt], wi[:, o_hy:o_s5 + W_GRP], wdt,
                                jnp.zeros((D_MODEL, 128 - 2 * N_HEADS), F32)], axis=1).astype(BF16)
        mods = mods_all[i]
        g1 = norm1_g[i].reshape(1, -1)
        zl, xbcl, hyl, retl, dtl, s5l = _inproj_call(h_l, g1, mods, False, wcat)
        zc, xbcc, hyc, retc, dtc, s5c = _inproj_call(h_c, g1, mods, True, wcat)
        ssd_out = _ssd_call((zc, xbcc, dtc), (zl, xbcl, dtl), ssd_conv_w[i], ssd_conv_b[i],
                            ssd_a_log[i], ssd_dt_bias[i], ssd_d[i], ssd_norm_g[i], not last)
        ret_out = _ret_call(retc, retl, ret_decay[i], not last)
        ssd_c, ssd_l = ssd_out if not last else (None, ssd_out[0])
        ret_c, ret_l = ret_out if not last else (None, ret_out[0])
        p = dict(s5_a_re=s5_a_re[i], s5_a_im=s5_a_im[i], s5_log_dt=s5_log_dt[i], s5_b_re=s5_b_re[i],
                 s5_b_im=s5_b_im[i], s5_c_re=s5_c_re[i], s5_c_im=s5_c_im[i], s5_d=s5_d[i],
                 s5_glu_w=s5_glu_w[i], s5_glu_b=s5_glu_b[i],
                 hy_conv_w=hy_conv_w[i], hy_conv_b=hy_conv_b[i], hy_w1=hy_w1[i], hy_b1=hy_b1[i],
                 hy_freq=hy_freq[i], hy_w2=hy_w2[i], hy_b2=hy_b2[i], hy_w3=hy_w3[i], hy_bias=hy_bias[i])
        s5_c, s5_l = _s5_mixer(s5c, s5l, bsz, p)
        hy_l = _hyena_mixer(hyl, p)
        g2 = norm2_g[i].reshape(1, -1)
        wo = w_out[i].astype(BF16)
        wup = ffn_w_up[i].astype(BF16)
        wdn = ffn_w_down[i].astype(BF16)
        h_l = _out_ffn_call(h_l, (ssd_l, hy_l, ret_l, s5_l), mods, False, g2, wo, wup, wdn, fg, last)
        if not last:
            hy_c = _hyena_mixer(hyc, p)
            h_c = _out_ffn_call(h_c, (ssd_c, hy_c, ret_c, s5_c), mods, True, g2, wo, wup, wdn, fg, False)
    return h_l
```

```python
import functools
import math

import numpy as np
import jax
import jax.numpy as jnp
from jax import lax
from jax.experimental import pallas as pl
from jax.experimental.pallas import tpu as pltpu

F32 = jnp.float32
BF16 = jnp.bfloat16
EPS = 1e-6

D_MODEL = 1024
W_GRP = 256
T = 128
N_HEADS = 4
HEAD_DIM = 64
SSD_STATE = 64
SSD_XBC = W_GRP + 2 * SSD_STATE
HY_COLS = 3 * W_GRP
RET_COLS = 4 * W_GRP
GRID_W = 64
ROPE_BASE = 10000.0
HY_EMB = 33
HY_BANDS = 16
HY_FILT = 64
S5_GROUPS = 16
S5_CH = 16
S5_STATE = 64
S5_NS = S5_GROUPS * S5_STATE
D_FF = 2816
P3_GROUP = 4
S5_TS = 64
MOD_ROWS = 24

VMEM_LIMIT = 56 * 1024 * 1024


def _cparams(n_grid):
    return pltpu.CompilerParams(dimension_semantics=("arbitrary",) * n_grid,
                                vmem_limit_bytes=VMEM_LIMIT)


def _dot(a, b):
    return jnp.dot(a.astype(BF16), b.astype(BF16), preferred_element_type=F32)


def _dot_nt(a, b):
    return lax.dot_general(a.astype(BF16), b.astype(BF16), (((1,), (1,)), ((), ())),
                           preferred_element_type=F32)


def _dot_tn(a, b):
    return lax.dot_general(a.astype(BF16), b.astype(BF16), (((0,), (0,)), ((), ())),
                           preferred_element_type=F32)


def _dot_f32(a, b):
    return jnp.dot(a, b, preferred_element_type=F32, precision=lax.Precision.HIGHEST)


def _silu(x):
    return x * jax.nn.sigmoid(x)


def _const_spec(shape):
    nd = len(shape)
    return pl.BlockSpec(shape, lambda *_: (0,) * nd, pipeline_mode=pl.Buffered(1))


def _mod_kernel(sc_ref, w_ref, b_ref, o_ref):
    s = _silu(sc_ref[...])
    o_ref[0] = _dot_f32(s, w_ref[0]) + b_ref[0]


def _mod_call(sc, mod_w, mod_b):
    depth, _, n = mod_w.shape
    tn = 1536
    return pl.pallas_call(
        _mod_kernel,
        grid=(depth, n // tn),
        in_specs=[pl.BlockSpec((MOD_ROWS, D_MODEL), lambda l, j: (0, 0)),
                  pl.BlockSpec((1, D_MODEL, tn), lambda l, j: (l, 0, j)),
                  pl.BlockSpec((1, 1, tn), lambda l, j: (l, 0, j))],
        out_specs=pl.BlockSpec((1, MOD_ROWS, tn), lambda l, j: (l, 0, j)),
        out_shape=jax.ShapeDtypeStruct((depth, MOD_ROWS, n), F32),
        compiler_params=_cparams(2),
        name="adaln_mod",
    )(sc, mod_w, mod_b.reshape(depth, 1, n))


def _inproj_kernel(x_ref, g_ref, mod_ref, w_ref, z_ref, xbc_ref, hy_ref, ret_ref, dt_ref, s5_ref):
    x = x_ref[0]
    xn = x * lax.rsqrt(jnp.mean(x * x, axis=-1, keepdims=True) + EPS) * g_ref[...]
    xm = (xn * (1.0 + mod_ref[0, 1:2, :]) + mod_ref[0, 0:1, :]).astype(BF16)
    o = 0
    for ref, width in ((z_ref, W_GRP), (xbc_ref, SSD_XBC), (hy_ref, HY_COLS),
                       (ret_ref, RET_COLS)):
        ref[0] = jnp.dot(xm, w_ref[:, o:o + width], preferred_element_type=F32).astype(BF16)
        o += width
    s5_ref[...] = jnp.dot(xm, w_ref[:, o:o + W_GRP], preferred_element_type=F32).astype(BF16)
    o += W_GRP
    dt_ref[0] = jnp.dot(xm, w_ref[:, o:o + 128], preferred_element_type=F32)


def _inproj_call(h, g, mods, ctx_stream, wcat):
    bsz, L, _ = h.shape
    tm = min(512, L)
    ncols = wcat.shape[1]
    mod_map = (lambda b, i: (bsz, 0, 0)) if ctx_stream else (lambda b, i: (b, 0, 0))
    tok = lambda w: pl.BlockSpec((1, tm, w), lambda b, i: (b, i, 0))
    out_shape = [jax.ShapeDtypeStruct((bsz, L, W_GRP), BF16),
                 jax.ShapeDtypeStruct((bsz, L, SSD_XBC), BF16),
                 jax.ShapeDtypeStruct((bsz, L, HY_COLS), BF16),
                 jax.ShapeDtypeStruct((bsz, L, RET_COLS), BF16),
                 jax.ShapeDtypeStruct((bsz, L, 128), F32),
                 jax.ShapeDtypeStruct((L, bsz * W_GRP), BF16)]
    out_specs = [tok(W_GRP), tok(SSD_XBC), tok(HY_COLS), tok(RET_COLS), tok(128),
                 pl.BlockSpec((tm, W_GRP), lambda b, i: (i, b))]
    return pl.pallas_call(
        _inproj_kernel,
        grid=(bsz, L // tm),
        in_specs=[tok(D_MODEL),
                  pl.BlockSpec((1, D_MODEL), lambda b, i: (0, 0)),
                  pl.BlockSpec((1, 6, D_MODEL), mod_map),
                  _const_spec((D_MODEL, ncols))],
        out_specs=out_specs,
        out_shape=out_shape,
        compiler_params=_cparams(2),
        name="in_proj",
    )(h, g, mods, wcat)


def _halo_rows(ref, s, c, nc, L):
    sp = pl.multiple_of(jnp.maximum(s - 16, 0), 16)
    prev = ref[0, pl.ds(sp, 16), :][15:16, :].astype(F32)
    prev = jnp.where(c > 0, prev, 0.0)
    sn = pl.multiple_of(jnp.minimum(s + T, L - 16), 16)
    nxt = ref[0, pl.ds(sn, 16), :][0:1, :].astype(F32)
    nxt = jnp.where(c < nc - 1, nxt, 0.0)
    return prev, nxt


def _dwconv_chunk(x, prev, nxt, w_ref, b_ref):
    n = x.shape[0]
    row = lax.broadcasted_iota(jnp.int32, x.shape, 0)
    up = jnp.where(row == 0, prev, pltpu.roll(x, 1, 0))
    dn = jnp.where(row == n - 1, nxt, pltpu.roll(x, n - 1, 0))
    return up * w_ref[0:1, :] + x * w_ref[1:2, :] + dn * w_ref[2:3, :] + b_ref[...]


def _cumsum(x, axis):
    n = x.shape[axis]
    idx = lax.broadcasted_iota(jnp.int32, x.shape, axis)
    s = 1
    while s < n:
        x = x + jnp.where(idx >= s, pltpu.roll(x, s, axis), 0.0)
        s *= 2
    return x


def _expand_heads(c, exp_ref):
    hi = c.astype(BF16)
    lo = (c - hi.astype(F32)).astype(BF16)
    return jnp.dot(jnp.concatenate([hi, lo], axis=1), exp_ref[...], preferred_element_type=F32)


def _state_recurrence(hf_ref, hb_ref, decf_ref, decb_ref, nc, h0f, h0b):
    hf_ref[0] = h0f
    hb_ref[nc] = h0b

    def fwd(c, carry):
        hf_ref[c + 1] = decf_ref[c] * hf_ref[c] + hf_ref[c + 1]
        return carry

    lax.fori_loop(0, nc, fwd, 0)

    def bwd(k, carry):
        c = nc - 1 - k
        hb_ref[c] = decb_ref[c] * hb_ref[c + 1] + hb_ref[c]
        return carry

    lax.fori_loop(0, nc, bwd, 0)
    return hf_ref[nc], hb_ref[0]


def _split3(x):
    hi = x.astype(BF16)
    r = x - hi.astype(F32)
    mid = r.astype(BF16)
    lo = (r - mid.astype(F32)).astype(BF16)
    return jnp.concatenate([hi, mid, lo], axis=1)


def _tile_heads_bd(x, bdmask):
    xb = x.astype(BF16)
    return jnp.where(bdmask, jnp.concatenate([xb] * N_HEADS, axis=0), 0)


def _rows_to_lanes(a, lo):
    return jnp.concatenate([a[lo + h:lo + h + 1, :] for h in range(N_HEADS)], axis=1)


def _ssd_sequence(L, z_ref, xbc_ref, dt_ref, y_ref, prm, scr, h0f, h0b):
    (cw_ref, cb_ref, alog_row, bias_row, dskip_ref, ng_ref, exp_ref, exp128_ref, sel_ref, tri_ref) = prm
    (xs_s, bc_s, ee_s, acs_s, dt_s, hf_s, hb_s, decf_s, decb_s) = scr
    nc = L // T
    lane = lax.broadcasted_iota(jnp.int32, (T, 128), 1)
    a_row = -jnp.exp(alog_row[...])

    grp = min(P3_GROUP, nc)

    def phase1(p, carry):
        cs = [p * grp + j for j in range(grp)]
        rows = [pl.ds(pl.multiple_of(c * T, T), T) for c in cs]
        pre = []
        for r in rows:
            dt = jax.nn.softplus(dt_ref[0, r, :] + bias_row[...])
            dt_s[r, :] = dt
            la = dt * a_row
            hi = la.astype(BF16)
            r1 = la - hi.astype(F32)
            mid = r1.astype(BF16)
            lo = (r1 - mid.astype(F32)).astype(BF16)
            acs_f = jnp.dot(tri_ref[...], jnp.concatenate([hi, mid, lo], axis=0),
                            preferred_element_type=F32)
            pre.append((dt, la, acs_f))
        st = []
        for c, r, (dt, la, acs_f) in zip(cs, rows, pre):
            prev, nxt = _halo_rows(xbc_ref, pl.multiple_of(c * T, T), c, nc, L)
            xact = _silu(_dwconv_chunk(xbc_ref[0, r, :].astype(F32), prev, nxt, cw_ref, cb_ref))
            xs_s[r, :] = xact[:, 0:W_GRP]
            bc_s[r, :] = xact[:, W_GRP:SSD_XBC]
            tot = acs_f[T - 1:T, :]
            acs = jnp.where(lane < N_HEADS, acs_f, tot - acs_f + la)
            acs_s[r, :] = acs
            st.append((xact, _expand_heads(jnp.exp(acs), exp_ref),
                       _expand_heads(dt * jnp.exp(tot - acs), exp_ref)))
        for c, r, (xact, ee, wx) in zip(cs, rows, st):
            xs = xact[:, 0:W_GRP]
            bm = xact[:, W_GRP:W_GRP + SSD_STATE]
            ee_s[r, :] = ee
            hf_s[c + 1] = _dot_tn(bm, xs * wx[:, 0:W_GRP])
            hb_s[c] = _dot_tn(bm, xs * wx[:, W_GRP:2 * W_GRP])
            decf_s[c] = ee[T - 1:T, 0:W_GRP]
            decb_s[c] = ee[0:1, W_GRP:2 * W_GRP]
        return carry

    lax.fori_loop(0, nc // grp, phase1, 0)
    hf_fin, hb_fin = _state_recurrence(hf_s, hb_s, decf_s, decb_s, nc, h0f, h0b)
    if y_ref is None:
        return hf_fin, hb_fin

    ri = lax.broadcasted_iota(jnp.int32, (T, N_HEADS * T), 0)
    ci = lax.broadcasted_iota(jnp.int32, (T, N_HEADS * T), 1) % T
    strict_lower = ci < ri
    diag = ci == ri
    r4 = lax.broadcasted_iota(jnp.int32, (N_HEADS * T, W_GRP), 0) // T
    c4 = lax.broadcasted_iota(jnp.int32, (N_HEADS * T, W_GRP), 1) // HEAD_DIM
    bdmask = r4 == c4
    nl = N_HEADS * T

    nt = (((1,), (1,)), ((), ()))

    def phase3(p, carry):
        cs = [p * grp + j for j in range(grp)]
        rows = [pl.ds(pl.multiple_of(c * T, T), T) for c in cs]
        st = []
        for c, r in zip(cs, rows):
            bc = bc_s[r, :]
            bm = bc[:, 0:SSD_STATE]
            cm = bc[:, SSD_STATE:2 * SSD_STATE]
            a3 = _split3(acs_s[r, :])
            col = jnp.dot(a3, exp128_ref[...], preferred_element_type=F32)
            acst = lax.dot_general(sel_ref[...], a3, nt, preferred_element_type=F32)
            dtt = lax.dot_general(sel_ref[...], _split3(dt_s[r, :]), nt,
                                  preferred_element_type=F32)
            inter = (_dot(cm, hf_s[c]), _dot(cm, hb_s[c + 1]))
            st.append((col, acst, dtt, _dot_nt(cm, bm), inter))
        ys = []
        for r, (col, acst, dtt, g, inter) in zip(rows, st):
            g4 = jnp.concatenate([g] * N_HEADS, axis=1)
            shifted = acst - jnp.log(dtt)
            arg = jnp.where(strict_lower, col[:, 0:nl] - _rows_to_lanes(shifted, 0),
                            col[:, nl:2 * nl] - _rows_to_lanes(shifted, N_HEADS))
            w = g4 * (jnp.exp(arg) + jnp.where(diag, _rows_to_lanes(dtt, 0), 0.0))
            ys.append(jnp.dot(w.astype(BF16), _tile_heads_bd(xs_s[r, :], bdmask),
                              preferred_element_type=F32))
        for r, y, (_, _, _, _, inter) in zip(rows, ys, st):
            ee = ee_s[r, :]
            xs = xs_s[r, :]
            y = y + ee[:, 0:W_GRP] * inter[0] + ee[:, W_GRP:2 * W_GRP] * inter[1]
            y = y + xs * dskip_ref[...]
            y = y * _silu(z_ref[0, r, :].astype(F32))
            y = y * lax.rsqrt(jnp.mean(y * y, axis=-1, keepdims=True) + EPS) * ng_ref[...]
            y_ref[0, r, :] = y.astype(y_ref.dtype)
        return carry

    lax.fori_loop(0, nc // grp, phase3, 0)
    return hf_fin, hb_fin


def _ssd_kernel(Lc, L, ctx_out, zc_ref, xbcc_ref, dtc_ref, zl_ref, xbcl_ref, dtl_ref,
                cw_ref, cb_ref, alog_row, bias_row, dskip_ref, ng_ref, exp_ref, exp128_ref, sel_ref,
                tri_ref, *rest):
    yc_ref, yl_ref, scr = (rest[0], rest[1], rest[2:]) if ctx_out else (None, rest[0], rest[1:])
    prm = (cw_ref, cb_ref, alog_row, bias_row, dskip_ref, ng_ref, exp_ref, exp128_ref, sel_ref, tri_ref)
    zero = jnp.zeros((SSD_STATE, W_GRP), F32)
    hf, hb = _ssd_sequence(Lc, zc_ref, xbcc_ref, dtc_ref, yc_ref, prm, scr, zero, zero)
    _ssd_sequence(L, zl_ref, xbcl_ref, dtl_ref, yl_ref, prm, scr, hf, hb)


@functools.lru_cache(maxsize=None)
def _ssd_tables():
    exp64 = np.zeros((256, 512), np.float32)
    exp128 = np.zeros((384, 2 * N_HEADS * T), np.float32)
    sel = np.zeros((8, 384), np.float32)
    for r in range(2):
        for h in range(N_HEADS):
            m = r * N_HEADS + h
            for part in range(2):
                exp64[128 * part + m, r * 256 + 64 * h:r * 256 + 64 * (h + 1)] = 1.0
            for part in range(3):
                exp128[128 * part + m, (r * N_HEADS + h) * T:(r * N_HEADS + h + 1) * T] = 1.0
                sel[m, 128 * part + m] = 1.0
    tri = np.tile(np.tril(np.ones((T, T), np.float32)), (1, 3))
    return exp64, exp128, sel, tri


def _pad_row(v, n=128):
    v = v.reshape(1, -1)
    return jnp.pad(v, ((0, 0), (0, n - v.shape[1])))


def _ssd_call(uc, ul, conv_w, conv_b, a_log, dt_bias, d_skip, norm_g, ctx_out):
    zc, xbcc, dtc = uc
    zl, xbcl, dtl = ul
    bsz, Lc, _ = zc.shape
    L = zl.shape[1]
    nc = L // T
    alog_row = _pad_row(a_log)
    bias_row = _pad_row(dt_bias)
    dskip = jnp.repeat(d_skip, HEAD_DIM).reshape(1, W_GRP)
    exp64, exp128, sel, tri = (jnp.asarray(t).astype(BF16) for t in _ssd_tables())

    def seq(Lx, w):
        return pl.BlockSpec((1, Lx, w), lambda b: (b, 0, 0))

    def small(shape):
        return pl.BlockSpec(shape, lambda b: (0,) * len(shape))

    in_specs = [seq(Lc, W_GRP), seq(Lc, SSD_XBC), seq(Lc, 128),
                seq(L, W_GRP), seq(L, SSD_XBC), seq(L, 128),
                small((3, SSD_XBC)), small((1, SSD_XBC)), small((1, 128)),
                small((1, 128)), small((1, W_GRP)), small((1, W_GRP)),
                small(exp64.shape), small(exp128.shape), small(sel.shape), small(tri.shape)]
    scratch = [pltpu.VMEM((L, W_GRP), F32), pltpu.VMEM((L, 128), F32), pltpu.VMEM((L, 512), F32),
               pltpu.VMEM((L, 128), F32), pltpu.VMEM((L, 128), F32),
               pltpu.VMEM((nc + 1, SSD_STATE, W_GRP), F32), pltpu.VMEM((nc + 1, SSD_STATE, W_GRP), F32),
               pltpu.VMEM((nc, 1, W_GRP), F32), pltpu.VMEM((nc, 1, W_GRP), F32)]
    return pl.pallas_call(
        functools.partial(_ssd_kernel, Lc, L, ctx_out),
        grid=(bsz,),
        in_specs=in_specs,
        out_specs=([seq(Lc, W_GRP)] if ctx_out else []) + [seq(L, W_GRP)],
        out_shape=([jax.ShapeDtypeStruct((bsz, Lc, W_GRP), BF16)] if ctx_out else [])
                  + [jax.ShapeDtypeStruct((bsz, L, W_GRP), BF16)],
        scratch_shapes=scratch,
        compiler_params=_cparams(1),
        name="ssd_mixer",
    )(zc, xbcc, dtc, zl, xbcl, dtl, conv_w, conv_b.reshape(1, -1),
      alog_row, bias_row, dskip, norm_g.reshape(1, -1), exp64, exp128, sel, tri)


def _ret_sequence(L, rope, q_ref, k_ref, v_ref, g_ref, y_ref, cos_ref, sin_ref, perm_ref, avg_ref,
                  tabs, scr, h0f, h0b):
    (ef, eb, wf, wb, decf, decb, bdmask) = tabs
    (qr_s, kr_s, dm_s, hf_s, hb_s) = scr
    nc = L // T
    scale = HEAD_DIM ** -0.5
    r4 = lax.broadcasted_iota(jnp.int32, (N_HEADS * T, W_GRP), 0) // T
    c4 = lax.broadcasted_iota(jnp.int32, (N_HEADS * T, W_GRP), 1) // HEAD_DIM
    stackmask = r4 == c4

    def rot(x, rows):
        partner = jnp.dot(x, perm_ref[...], preferred_element_type=F32)
        return x.astype(F32) * cos_ref[rows, :] + partner * sin_ref[rows, :]

    grp = min(P3_GROUP, nc)

    def phase1(p, carry):
        cs = [p * grp + j for j in range(grp)]
        rows = [pl.ds(pl.multiple_of(c * T, T), T) for c in cs]
        qk = [(q_ref[0, r, :], k_ref[0, r, :]) for r in rows]
        if rope:
            qk = [(rot(q, r), rot(k, r)) for (q, k), r in zip(qk, rows)]
        for c, r, (q, k) in zip(cs, rows, qk):
            k = k.astype(F32) * scale
            qr_s[r, :] = q.astype(F32)
            kr_s[r, :] = k
            v = v_ref[0, r, :]
            hf_s[c + 1] = _dot_tn(k * wf, v) * bdmask
            hb_s[c] = _dot_tn(k * wb, v) * bdmask
        return carry

    lax.fori_loop(0, nc // grp, phase1, 0)

    hf_s[0] = h0f
    hb_s[nc] = h0b

    def fwd(c, carry):
        hf_s[c + 1] = decf * hf_s[c] + hf_s[c + 1]
        return carry

    lax.fori_loop(0, nc, fwd, 0)

    def bwd(kk, carry):
        c = nc - 1 - kk
        hb_s[c] = decb * hb_s[c + 1] + hb_s[c]
        return carry

    lax.fori_loop(0, nc, bwd, 0)
    if y_ref is None:
        return hf_s[nc], hb_s[0]

    def gmean(x):
        hi = x.astype(BF16)
        lo = (x - hi.astype(F32)).astype(BF16)
        return jnp.dot(jnp.concatenate([hi, lo], axis=1), avg_ref[...], preferred_element_type=F32)

    def gmean_sq(x):
        return jnp.dot((x * x).astype(BF16), avg_ref[0:W_GRP, :], preferred_element_type=F32)

    def phase3(p, carry):
        cs = [p * grp + j for j in range(grp)]
        rows = [pl.ds(pl.multiple_of(c * T, T), T) for c in cs]
        qs = [qr_s[r, :] for r in rows]
        scs = [lax.dot_general(q.astype(BF16), _tile_heads_bd(kr_s[r, :], stackmask),
                               (((1,), (1,)), ((), ())), preferred_element_type=F32)
               for q, r in zip(qs, rows)]
        inters = [_dot(q * ef, hf_s[c]) + _dot(q * eb, hb_s[c + 1]) for q, c in zip(qs, cs)]
        ys = [jnp.dot((sc * dm_s[...]).astype(BF16), _tile_heads_bd(v_ref[0, r, :], stackmask),
                      preferred_element_type=F32) + it
              for sc, r, it in zip(scs, rows, inters)]
        ycs = [y - gmean(y) for y in ys]
        vars_ = [gmean_sq(yc) for yc in ycs]
        for r, yc, var in zip(rows, ycs, vars_):
            y_ref[0, r, :] = (_silu(g_ref[0, r, :].astype(F32))
                              * (yc * lax.rsqrt(var + EPS))).astype(y_ref.dtype)
        return carry

    lax.fori_loop(0, nc // grp, phase3, 0)
    return hf_s[nc], hb_s[0]


def _ret_kernel(Lc, L, ctx_out, qc_ref, kc_ref, vc_ref, gc_ref, ql_ref, kl_ref, vl_ref, gl_ref,
                cos_ref, sin_ref, dec_ref, perm_ref, avg_ref, *rest):
    yc_ref, rest = (rest[0], rest[1:]) if ctx_out else (None, rest)
    yl_ref, qr_s, kr_s, dm_s, hf_s, hb_s = rest
    lg = -jnp.exp(dec_ref[...])
    lgf = lg[0:1, :]
    lgb = lg[1:2, :]
    i = lax.broadcasted_iota(jnp.int32, (T, W_GRP), 0).astype(F32)
    ef = jnp.exp(lgf * (i + 1.0))
    eb = jnp.exp(lgb * (T - i))
    wf = jnp.exp(lgf * (T - 1.0 - i))
    wb = jnp.exp(lgb * i)
    decf = jnp.exp(lgf * float(T))
    decb = jnp.exp(lgb * float(T))
    r2 = lax.broadcasted_iota(jnp.int32, (W_GRP, W_GRP), 0) // HEAD_DIM
    c2 = lax.broadcasted_iota(jnp.int32, (W_GRP, W_GRP), 1) // HEAD_DIM
    bdmask = (r2 == c2).astype(F32)
    ri = lax.broadcasted_iota(jnp.int32, (T, T), 0)
    ci = lax.broadcasted_iota(jnp.int32, (T, T), 1)
    d = (ri - ci).astype(F32)
    for h in range(N_HEADS):
        lf = lgf[:, HEAD_DIM * h:HEAD_DIM * h + 1]
        lb = lgb[:, HEAD_DIM * h:HEAD_DIM * h + 1]
        dm_s[:, T * h:T * (h + 1)] = (jnp.exp(jnp.where(ci <= ri, lf * d, -jnp.inf))
                                      + jnp.exp(jnp.where(ci >= ri, -lb * d, -jnp.inf)))
    tabs = (ef, eb, wf, wb, decf, decb, bdmask)
    scr = (qr_s, kr_s, dm_s, hf_s, hb_s)
    zero = jnp.zeros((W_GRP, W_GRP), F32)
    hf, hb = _ret_sequence(Lc, False, qc_ref, kc_ref, vc_ref, gc_ref, yc_ref, cos_ref, sin_ref,
                           perm_ref, avg_ref, tabs, scr, zero, zero)
    _ret_sequence(L, True, ql_ref, kl_ref, vl_ref, gl_ref, yl_ref, cos_ref, sin_ref,
                  perm_ref, avg_ref, tabs, scr, hf, hb)


@functools.lru_cache(maxsize=None)
def _rope_tables(L):
    t = np.arange(L)
    f = 16
    inv = (ROPE_BASE ** (-np.arange(f, dtype=np.float32) / f)).astype(np.float32)
    cos = np.zeros((L, HEAD_DIM), np.float32)
    sin = np.zeros((L, HEAD_DIM), np.float32)
    for base, pos in ((0, t // GRID_W), (32, t % GRID_W)):
        ang = pos.astype(np.float32)[:, None] * inv[None, :]
        ang = ang.astype(np.float32).astype(np.float64)
        cos[:, base:base + f] = np.cos(ang)
        cos[:, base + f:base + 2 * f] = np.cos(ang)
        sin[:, base:base + f] = -np.sin(ang)
        sin[:, base + f:base + 2 * f] = np.sin(ang)
    return np.tile(cos, (1, N_HEADS)), np.tile(sin, (1, N_HEADS))


@functools.lru_cache(maxsize=None)
def _ret_tables():
    perm = np.zeros((W_GRP, W_GRP), np.float32)
    avg = np.zeros((2 * W_GRP, W_GRP), np.float32)
    for l in range(W_GRP):
        src = l + 16 if (l % 32) < 16 else l - 16
        perm[src, l] = 1.0
        g = l // HEAD_DIM
        for part in range(2):
            avg[part * W_GRP + g * HEAD_DIM:part * W_GRP + (g + 1) * HEAD_DIM, l] = 1.0 / HEAD_DIM
    return perm, avg


def _ret_call(uc, ul, decay_param, ctx_out):
    bsz, Lc, _ = uc.shape
    L = ul.shape[1]
    nc = L // T
    cos, sin = _rope_tables(L)
    perm, avg = (jnp.asarray(t).astype(BF16) for t in _ret_tables())
    dec = jnp.repeat(decay_param, HEAD_DIM, axis=1)

    def col(Lx, j):
        return pl.BlockSpec((1, Lx, W_GRP), lambda b, j=j: (b, 0, j))

    def seq(Lx):
        return pl.BlockSpec((1, Lx, W_GRP), lambda b: (b, 0, 0))

    in_specs = ([col(Lc, j) for j in range(4)] + [col(L, j) for j in range(4)]
                + [pl.BlockSpec((L, W_GRP), lambda b: (0, 0)), pl.BlockSpec((L, W_GRP), lambda b: (0, 0)),
                   pl.BlockSpec((2, W_GRP), lambda b: (0, 0)),
                   pl.BlockSpec((W_GRP, W_GRP), lambda b: (0, 0)),
                   pl.BlockSpec((2 * W_GRP, W_GRP), lambda b: (0, 0))])
    scratch = [pltpu.VMEM((L, W_GRP), F32), pltpu.VMEM((L, W_GRP), F32),
               pltpu.VMEM((T, N_HEADS * T), F32),
               pltpu.VMEM((nc + 1, W_GRP, W_GRP), F32), pltpu.VMEM((nc + 1, W_GRP, W_GRP), F32)]
    return pl.pallas_call(
        functools.partial(_ret_kernel, Lc, L, ctx_out),
        grid=(bsz,),
        in_specs=in_specs,
        out_specs=([seq(Lc)] if ctx_out else []) + [seq(L)],
        out_shape=([jax.ShapeDtypeStruct((bsz, Lc, W_GRP), BF16)] if ctx_out else [])
                  + [jax.ShapeDtypeStruct((bsz, L, W_GRP), BF16)],
        scratch_shapes=scratch,
        compiler_params=_cparams(1),
        name="retention_mixer",
    )(uc, uc, uc, uc, ul, ul, ul, ul, jnp.asarray(cos), jnp.asarray(sin), dec, perm, avg)


def _s5_prep_kernel(are_ref, aim_ref, ldt_ref, bre_ref, bim_ref, cre_ref, cim_ref,
                    bmat_ref, cmat_ref, ab_ref):
    a_re = are_ref[0]
    a_im = aim_ref[0]
    dt = jnp.exp(ldt_ref[0])
    mag = jnp.exp(a_re * dt)
    ab_re = mag * jnp.cos(a_im * dt)
    ab_im = mag * jnp.sin(a_im * dt)
    den = a_re * a_re + a_im * a_im
    z_re = ((ab_re - 1.0) * a_re + ab_im * a_im) / den
    z_im = (ab_im * a_re - (ab_re - 1.0) * a_im) / den
    b_re = bre_ref[...]
    b_im = bim_ref[...]
    bmat_ref[0, :, 0:S5_NS] = (b_re * z_re - b_im * z_im).astype(BF16)
    bmat_ref[0, :, S5_NS:2 * S5_NS] = (b_re * z_im + b_im * z_re).astype(BF16)
    cmat_ref[0, 0:S5_NS, :] = cre_ref[0].astype(BF16)
    cmat_ref[0, S5_NS:2 * S5_NS, :] = (-cim_ref[0]).astype(BF16)
    ab_ref[0, :, 0:S5_NS] = ab_re
    ab_ref[0, :, S5_NS:2 * S5_NS] = ab_im


def _s5_prep_call(a_re, a_im, log_dt, b_re, b_im, c_re, c_im):
    eye = jnp.eye(S5_GROUPS, dtype=F32)
    b_re_bd = jnp.einsum('gpc,gh->gchp', b_re, eye).reshape(W_GRP, S5_NS)
    b_im_bd = jnp.einsum('gpc,gh->gchp', b_im, eye).reshape(W_GRP, S5_NS)
    c_re_bd = jnp.einsum('rgcp,gh->rgphc', c_re, eye).reshape(2, S5_NS, W_GRP)
    c_im_bd = jnp.einsum('rgcp,gh->rgphc', c_im, eye).reshape(2, S5_NS, W_GRP)
    ldt = jnp.repeat(log_dt, S5_STATE, axis=1).reshape(2, 1, S5_NS)
    row = lambda: pl.BlockSpec((1, 1, S5_NS), lambda r: (r, 0, 0))
    return pl.pallas_call(
        _s5_prep_kernel,
        grid=(2,),
        in_specs=[row(), row(), row(),
                  pl.BlockSpec((W_GRP, S5_NS), lambda r: (0, 0)), pl.BlockSpec((W_GRP, S5_NS), lambda r: (0, 0)),
                  pl.BlockSpec((1, S5_NS, W_GRP), lambda r: (r, 0, 0)),
                  pl.BlockSpec((1, S5_NS, W_GRP), lambda r: (r, 0, 0))],
        out_specs=[pl.BlockSpec((1, W_GRP, 2 * S5_NS), lambda r: (r, 0, 0)),
                   pl.BlockSpec((1, 2 * S5_NS, W_GRP), lambda r: (r, 0, 0)),
                   pl.BlockSpec((1, 1, 2 * S5_NS), lambda r: (r, 0, 0))],
        out_shape=[jax.ShapeDtypeStruct((2, W_GRP, 2 * S5_NS), BF16),
                   jax.ShapeDtypeStruct((2, 2 * S5_NS, W_GRP), BF16),
                   jax.ShapeDtypeStruct((2, 1, 2 * S5_NS), F32)],
        compiler_params=_cparams(1),
        name="s5_prep",
    )(a_re.reshape(2, 1, S5_NS), a_im.reshape(2, 1, S5_NS), ldt, b_re_bd, b_im_bd, c_re_bd, c_im_bd)


def _s5_pipe_kernel(bsz, reverse, u_ref, h0_ref, bmat_ref, cmat_ref, ab_ref, *rest):
    if reverse:
        y_ref, hfin_ref, x0_s, x1_s, hb0_s, hb1_s, hs_s = rest
    else:
        ul_ref, yb_ref, d_ref, y_ref, hfin_ref, x0_s, x1_s, hb0_s, hb1_s, hs_s = rest
    g = pl.program_id(0)
    rows_blk = S5_TS * bsz

    @pl.when(g == 0)
    def _():
        x1_s[...] = jnp.zeros_like(x1_s)
        hb0_s[...] = jnp.zeros_like(hb0_s)
        hb1_s[...] = jnp.zeros_like(hb1_s)
        hs_s[...] = jnp.zeros_like(hs_s)

    def half(k):
        first = (1 - k) if reverse else k
        return slice(first * S5_TS, (first + 1) * S5_TS)

    def emit(k, hb_s):
        y = jnp.dot(hb_s[...], cmat_ref[0], preferred_element_type=F32)
        if not reverse:
            y = (y + yb_ref[half(k), :, :].reshape(rows_blk, W_GRP)
                 + ul_ref[half(k), :, :].reshape(rows_blk, W_GRP).astype(F32) * d_ref[...])
        y_ref[half(k), :, :] = y.reshape(S5_TS, bsz, W_GRP)

    def project(k, x_s):
        u = u_ref[half(k), :, :].reshape(rows_blk, W_GRP)
        x_s[...] = jnp.dot(u, bmat_ref[0], preferred_element_type=F32)

    def scan(x_s, hb_s):
        for q in range(S5_NS // W_GRP):
            cr = slice(W_GRP * q, W_GRP * (q + 1))
            cim = slice(S5_NS + W_GRP * q, S5_NS + W_GRP * (q + 1))
            a_re = jnp.broadcast_to(ab_ref[0, :, cr], (bsz, W_GRP))
            a_im = jnp.broadcast_to(ab_ref[0, :, cim], (bsz, W_GRP))
            h_re = hs_s[:, cr]
            h_im = hs_s[:, cim]
            for kk in range(S5_TS):
                t = (S5_TS - 1 - kk) if reverse else kk
                rows = slice(t * bsz, (t + 1) * bsz)
                h_re, h_im = (a_re * h_re - a_im * h_im + x_s[rows, cr],
                              a_re * h_im + a_im * h_re + x_s[rows, cim])
                hb_s[rows, cr] = h_re.astype(BF16)
                hb_s[rows, cim] = h_im.astype(BF16)
            hs_s[:, cr] = h_re
            hs_s[:, cim] = h_im

    emit(0, hb0_s)
    project(0, x0_s)
    scan(x1_s, hb1_s)
    hfin_ref[...] = hs_s[...]
    hs_s[...] = jnp.where(g == 0, h0_ref[...], hs_s[...])
    emit(1, hb1_s)
    project(1, x1_s)
    scan(x0_s, hb0_s)


def _s5_pipe_call(r, u3, h0, bmat, cmat, ab, post=None):
    L, bsz, _ = u3.shape
    npair = L // (2 * S5_TS)
    blk = (2 * S5_TS, bsz, W_GRP)
    reverse = r == 1

    def pair_of(c):
        return npair - 1 - c if reverse else c

    cur = lambda i: (pair_of(jnp.minimum(i, npair - 1)), 0, 0)
    lag = lambda i: (pair_of(jnp.maximum(i - 1, 0)), 0, 0)
    state = pl.BlockSpec((bsz, 2 * S5_NS), lambda i: (0, 0))
    in_specs = [pl.BlockSpec(blk, cur), state,
                pl.BlockSpec((1, W_GRP, 2 * S5_NS), lambda i: (r, 0, 0)),
                pl.BlockSpec((1, 2 * S5_NS, W_GRP), lambda i: (r, 0, 0)),
                pl.BlockSpec((1, 1, 2 * S5_NS), lambda i: (r, 0, 0))]
    args = [u3, h0, bmat, cmat, ab]
    if not reverse:
        yb, d = post
        in_specs += [pl.BlockSpec(blk, lag), pl.BlockSpec(blk, lag),
                     pl.BlockSpec((1, W_GRP), lambda i: (0, 0))]
        args += [u3, yb, d.reshape(1, -1)]
    xbuf = pltpu.VMEM((S5_TS * bsz, 2 * S5_NS), F32)
    hbuf = pltpu.VMEM((S5_TS * bsz, 2 * S5_NS), BF16)
    return pl.pallas_call(
        functools.partial(_s5_pipe_kernel, bsz, reverse),
        grid=(npair + 1,),
        in_specs=in_specs,
        out_specs=[pl.BlockSpec(blk, lag), state],
        out_shape=[jax.ShapeDtypeStruct((L, bsz, W_GRP), F32),
                   jax.ShapeDtypeStruct((bsz, 2 * S5_NS), F32)],
        scratch_shapes=[xbuf, xbuf, hbuf, hbuf, pltpu.VMEM((bsz, 2 * S5_NS), F32)],
        compiler_params=_cparams(1),
        name="s5_bwd" if reverse else "s5_fwd",
    )(*args)


def _s5_mixer(u5c, u5l, bsz, p):
    bmat, cmat, ab = _s5_prep_call(p['s5_a_re'], p['s5_a_im'], p['s5_log_dt'], p['s5_b_re'],
                                   p['s5_b_im'], p['s5_c_re'], p['s5_c_im'])
    Lc, L = u5c.shape[0], u5l.shape[0]
    u3c = u5c.reshape(Lc, bsz, W_GRP)
    u3l = u5l.reshape(L, bsz, W_GRP)
    h0 = jnp.zeros((bsz, 2 * S5_NS), F32)
    ybc, hbc = _s5_pipe_call(1, u3c, h0, bmat, cmat, ab)
    ybl, _ = _s5_pipe_call(1, u3l, hbc, bmat, cmat, ab)
    s5c, hfc = _s5_pipe_call(0, u3c, h0, bmat, cmat, ab, post=(ybc, p['s5_d']))
    s5l, _ = _s5_pipe_call(0, u3l, hfc, bmat, cmat, ab, post=(ybl, p['s5_d']))
    return s5c.reshape(Lc, bsz * W_GRP), s5l.reshape(L, bsz * W_GRP)


@functools.lru_cache(maxsize=None)
def _dft_tables(L):
    k = np.arange(L, dtype=np.int64)
    ft = (k[:, None] * k[None, :]) % (2 * L)
    ang = ft.astype(np.float64) * (math.pi / L)
    return np.cos(ang).astype(np.float32), np.sin(ang).astype(np.float32)


@functools.lru_cache(maxsize=None)
def _dft_tables_split(L):
    c, s = _dft_tables(L)
    return (np.concatenate([c[0::2], c[1::2]], axis=0), np.concatenate([s[0::2], s[1::2]], axis=0))


@functools.lru_cache(maxsize=None)
def _hyena_consts(L):
    t = np.linspace(0.0, 1.0, L, dtype=np.float32)[:, None]
    w = (2.0 * math.pi * np.arange(L, dtype=np.float32)[:, None] / L).astype(np.float32)
    bands = np.linspace(1e-4, HY_BANDS - 1, HY_BANDS, dtype=np.float32)[None, :]
    bw = (bands * w).astype(np.float32).astype(np.float64)
    feats = np.zeros((L, 128), np.float32)
    feats[:, 0:1] = t
    feats[:, 1:1 + HY_BANDS] = np.cos(bw)
    feats[:, 1 + HY_BANDS:HY_EMB] = -np.sin(bw)
    max_decay = math.log(1e-2) / 0.3
    min_decay = math.log(1e-2) / 1.5
    deltas = np.abs(np.linspace(min_decay, max_decay, 4 * W_GRP, dtype=np.float32))[None, :]
    return feats, deltas.astype(np.float32)


HY_RB = 256


def _hy_filter_kernel(feats_ref, w1_ref, b1_ref, fr_ref, w2_ref, b2_ref, w3_ref, del_ref,
                      p_ref, q_ref, nrm_ref, an_ref):
    i = pl.program_id(0)
    feats = feats_ref[...]
    fr = fr_ref[...]
    h = jnp.sin(fr * (_dot_f32(feats, w1_ref[...]) + b1_ref[...]))
    h = jnp.sin(fr * (_dot_f32(h, w2_ref[...]) + b2_ref[...]))
    h = _dot_f32(h, w3_ref[...])
    h = h * jnp.exp(-feats[:, 0:1] * del_ref[...])
    half = 2 * W_GRP
    hf = h[:, 0:half]
    row = lax.broadcasted_iota(jnp.int32, (HY_RB, half), 0) + i * HY_RB
    hb = jnp.where(row == 0, 0.0, h[:, half:2 * half])
    p = hf + hb
    sign = (1 - 2 * (row % 2)).astype(F32)
    p_ref[...] = p.astype(BF16)
    q_ref[...] = (hb - hf).astype(BF16)

    @pl.when(i == 0)
    def _():
        nrm_ref[...] = jnp.full_like(nrm_ref, EPS)
        an_ref[...] = jnp.zeros_like(an_ref)

    nrm_ref[...] += (jnp.sum(jnp.abs(hf), axis=0, keepdims=True)
                     + jnp.sum(jnp.abs(hb), axis=0, keepdims=True))
    an_ref[...] += jnp.sum(p * sign, axis=0, keepdims=True)


def _hy_spectrum_kernel(L, c_ref, s_ref, p_ref, q_ref, nrm_ref, ans_ref, a_ref, bc_ref, an_ref):
    i = pl.program_id(0)
    n = 2.0 * L
    inv = 1.0 / nrm_ref[...]
    row = lax.broadcasted_iota(jnp.int32, a_ref.shape, 0) + i * HY_RB
    wv = jnp.where(row == 0, 1.0 / n, 2.0 / n) * inv
    a_ref[...] = wv * jnp.dot(c_ref[...], p_ref[...], preferred_element_type=F32)
    bc_ref[...] = wv * jnp.dot(s_ref[...], q_ref[...], preferred_element_type=F32)
    an_ref[...] = ans_ref[...] * inv / n


def _hy_filter_call(L, w1, b1, freq, w2, b2, w3, cmat, smat):
    feats, deltas = _hyena_consts(L)
    w1p = jnp.pad(w1, ((0, 128 - HY_EMB), (0, 0)))
    half = 2 * W_GRP
    nb = L // HY_RB
    full = lambda a: pl.BlockSpec(a.shape, lambda i, nd=a.ndim: (0,) * nd)
    small = (w1p, b1.reshape(1, -1), freq.reshape(1, -1), w2, b2.reshape(1, -1), w3, jnp.asarray(deltas))
    rowblk = lambda w: pl.BlockSpec((HY_RB, w), lambda i: (i, 0))
    vec = pl.BlockSpec((1, half), lambda i: (0, 0))
    p, q, nrm, ans = pl.pallas_call(
        _hy_filter_kernel,
        grid=(nb,),
        in_specs=[rowblk(128)] + [full(a) for a in small],
        out_specs=[rowblk(half), rowblk(half), vec, vec],
        out_shape=[jax.ShapeDtypeStruct((L, half), BF16), jax.ShapeDtypeStruct((L, half), BF16),
                   jax.ShapeDtypeStruct((1, half), F32), jax.ShapeDtypeStruct((1, half), F32)],
        compiler_params=_cparams(1),
        name="hyena_filter",
    )(jnp.asarray(feats), *small)
    return pl.pallas_call(
        functools.partial(_hy_spectrum_kernel, L),
        grid=(nb,),
        in_specs=[rowblk(L), rowblk(L), full(p), full(q), vec, vec],
        out_specs=[rowblk(half), rowblk(half), vec],
        out_shape=[jax.ShapeDtypeStruct((L, half), F32), jax.ShapeDtypeStruct((L, half), F32),
                   jax.ShapeDtypeStruct((1, half), F32)],
        compiler_params=_cparams(1),
        name="hyena_spectrum",
    )(cmat, smat, p, q, nrm, ans)


def _alt_sign(shape):
    return (1 - 2 * (lax.broadcasted_iota(jnp.int32, shape, 0) % 2)).astype(F32)


def _reverse_shift(x, j_ref):
    hi = x.astype(BF16)
    lo = (x - hi.astype(F32)).astype(BF16)
    j = j_ref[...]
    return (jnp.dot(j, hi, preferred_element_type=F32) + jnp.dot(j, lo, preferred_element_type=F32))


def _hy_conv_kernel(L, u_ref, cw_ref, cb_ref, ce_ref, se_ref, co_ref, so_ref, cot_ref, sot_ref, j_ref,
                    a_ref, bc_ref, an_ref, bias_ref, o_ref,
                    x1_s, x2_s, z_s, zs_s, zd_s, xe_s, ye_s, xo_s, yo_s, d_s, acc_s):
    nc = L // T
    H = L // 2
    fb = min(256, H)
    nfb = H // fb
    rb = min(1024, H)
    nrb = H // rb

    def conv(c, carry):
        s = pl.multiple_of(c * T, T)
        rows = pl.ds(s, T)
        prev, nxt = _halo_rows(u_ref, s, c, nc, L)
        y = _dwconv_chunk(u_ref[0, rows, :].astype(F32), prev, nxt, cw_ref, cb_ref)
        x1_s[rows, :] = y[:, 0:W_GRP]
        x2_s[rows, :] = y[:, W_GRP:2 * W_GRP]
        z_s[rows, :] = y[:, 2 * W_GRP:3 * W_GRP]
        return carry

    lax.fori_loop(0, nc, conv, 0)
    row0 = lax.broadcasted_iota(jnp.int32, (fb, W_GRP), 0) == 0

    for o, gate_s in enumerate((x1_s, x2_s)):
        cols = slice(W_GRP * o, W_GRP * (o + 1))
        acc_s[...] = jnp.zeros_like(acc_s)

        ws = [z_s[H + (nfb - 1 - b) * fb:H + (nfb - b) * fb, :] for b in range(nfb)]
        revs = [_reverse_shift(w, j_ref) for w in ws]
        qn = jnp.zeros((1, W_GRP), F32)
        for b in range(nfb):
            top = z_s[b * fb:(b + 1) * fb, :]
            edge = z_s[H + (nfb - b) * fb:H + (nfb - b) * fb + 1, :] if b > 0 else 0.0
            zr = jnp.where(row0, edge, revs[b])
            zs_s[b * fb:(b + 1) * fb, :] = (top + zr).astype(BF16)
            zd_s[b * fb:(b + 1) * fb, :] = (top - zr).astype(BF16)
            qn = qn + jnp.sum((top + ws[b]) * _alt_sign(top.shape), axis=0, keepdims=True)
        acc_s[0:1, :] = qn
        z_mid = z_s[H:H + 1, :]

        def fwd(j, carry, cols=cols, z_mid=z_mid):
            rows = pl.ds(pl.multiple_of(j * rb, rb), rb)
            odd_rows = pl.ds(pl.multiple_of(H + j * rb, rb), rb)
            zs = zs_s[...]
            zd = zd_s[...]
            mid = _alt_sign((rb, W_GRP)) * z_mid
            pe = jnp.dot(ce_ref[rows, :], zs, preferred_element_type=F32) + mid
            qo = jnp.dot(so_ref[rows, :], zs, preferred_element_type=F32) + mid
            po = jnp.dot(co_ref[rows, :], zd, preferred_element_type=F32)
            qe = jnp.dot(se_ref[rows, :], zd, preferred_element_type=F32)
            ae = a_ref[rows, cols]
            bce = bc_ref[rows, cols]
            ao = a_ref[odd_rows, cols]
            bco = bc_ref[odd_rows, cols]
            xe = pe * ae + qe * bce
            yo = qo * ao - po * bco
            xe_s[rows, :] = xe.astype(BF16)
            ye_s[rows, :] = (qe * ae - pe * bce).astype(BF16)
            xo_s[rows, :] = (po * ao + qo * bco).astype(BF16)
            yo_s[rows, :] = yo.astype(BF16)
            acc_s[1:2, :] += jnp.sum((xe + yo) * _alt_sign(xe.shape), axis=0, keepdims=True)
            return carry

        lax.fori_loop(0, nrb, fwd, 0)
        nyq = acc_s[0:1, :] * an_ref[:, cols]

        def finish(rows, y, cols=cols, gate_s=gate_s, nyq=nyq, o=o):
            res = gate_s[rows, :] * (y + _alt_sign(y.shape) * nyq + z_s[rows, :] * bias_ref[:, cols])
            if o == 0:
                z_s[rows, :] = res
            else:
                o_ref[0, rows, :] = res.astype(o_ref.dtype)

        def inv(j, carry, finish=finish):
            rows = pl.ds(pl.multiple_of(j * rb, rb), rb)
            u1 = (jnp.dot(ce_ref[rows, :], xe_s[...], preferred_element_type=F32)
                  + jnp.dot(sot_ref[rows, :], yo_s[...], preferred_element_type=F32))
            u2 = (jnp.dot(cot_ref[rows, :], xo_s[...], preferred_element_type=F32)
                  + jnp.dot(se_ref[rows, :], ye_s[...], preferred_element_type=F32))
            d_s[rows, :] = u1 - u2
            finish(rows, u1 + u2)
            return carry

        lax.fori_loop(0, nrb, inv, 0)
        y_mid = acc_s[1:2, :]

        drevs = [_reverse_shift(d_s[(nfb - 1 - b) * fb:(nfb - b) * fb, :], j_ref) for b in range(nfb)]
        for b in range(nfb):
            edge = d_s[(nfb - b) * fb:(nfb - b) * fb + 1, :] if b > 0 else y_mid
            finish(pl.ds(H + b * fb, fb), jnp.where(row0, edge, drevs[b]))


@functools.lru_cache(maxsize=None)
def _dft_half_tables(L):
    c, s = _dft_tables(L)
    H = L // 2
    fb = min(256, H)
    ce, co = c[0::2, :H], c[1::2, :H]
    se, so = s[0::2, :H], s[1::2, :H]
    j = np.zeros((fb, fb), np.float32)
    for r in range(1, fb):
        j[r, fb - r] = 1.0
    return tuple(np.ascontiguousarray(m) for m in (ce, se, co, so, co.T, so.T, j))


def _hy_conv_call(u, conv_w, conv_b, tables, a, bc, an, bias):
    bsz, L, _ = u.shape
    H = L // 2
    half = 2 * W_GRP
    mats = [jnp.asarray(m).astype(BF16) for m in tables]
    return pl.pallas_call(
        functools.partial(_hy_conv_kernel, L),
        grid=(bsz,),
        in_specs=[pl.BlockSpec((1, L, HY_COLS), lambda b: (b, 0, 0)),
                  pl.BlockSpec((3, HY_COLS), lambda b: (0, 0)),
                  pl.BlockSpec((1, HY_COLS), lambda b: (0, 0))]
                 + [_const_spec(m.shape) for m in mats]
                 + [_const_spec((L, half)), _const_spec((L, half)),
                    pl.BlockSpec((1, half), lambda b: (0, 0)),
                    pl.BlockSpec((1, half), lambda b: (0, 0))],
        out_specs=pl.BlockSpec((1, L, W_GRP), lambda b: (b, 0, 0)),
        out_shape=jax.ShapeDtypeStruct((bsz, L, W_GRP), BF16),
        scratch_shapes=[pltpu.VMEM((L, W_GRP), F32), pltpu.VMEM((L, W_GRP), F32),
                        pltpu.VMEM((L, W_GRP), F32),
                        pltpu.VMEM((H, W_GRP), BF16), pltpu.VMEM((H, W_GRP), BF16),
                        pltpu.VMEM((H, W_GRP), BF16), pltpu.VMEM((H, W_GRP), BF16),
                        pltpu.VMEM((H, W_GRP), BF16), pltpu.VMEM((H, W_GRP), BF16),
                        pltpu.VMEM((H, W_GRP), F32), pltpu.VMEM((8, W_GRP), F32)],
        compiler_params=_cparams(1),
        name="hyena_conv",
    )(u, conv_w, conv_b.reshape(1, -1), *mats, a, bc, an, bias.reshape(1, -1))


def _hyena_mixer(u, p):
    L = u.shape[1]
    cnp, snp = _dft_tables_split(L)
    cmat = jnp.asarray(cnp).astype(BF16)
    smat = jnp.asarray(snp).astype(BF16)
    a, bc, an = _hy_filter_call(L, p['hy_w1'], p['hy_b1'], p['hy_freq'], p['hy_w2'], p['hy_b2'],
                                p['hy_w3'], cmat, smat)
    return _hy_conv_call(u, p['hy_conv_w'], p['hy_conv_b'], _dft_half_tables(L), a, bc, an, p['hy_bias'])


def _out_ffn_kernel(final, h_ref, ssd_ref, hy_ref, ret_ref, s5_ref, gw_ref, gb_ref, mod_ref, g2_ref,
                    wo_ref, wup_ref, wdn_ref, fg_ref, o_ref, acc_s):
    s5 = jax.nn.gelu(s5_ref[...])
    s5 = s5 * jax.nn.sigmoid(jnp.dot(s5.astype(BF16), gw_ref[...], preferred_element_type=F32)
                             + gb_ref[...])
    y = jnp.zeros(h_ref.shape[1:], F32)
    for j, blk in enumerate((ssd_ref[0], hy_ref[0], ret_ref[0], s5)):
        y = y + jnp.dot(blk.astype(BF16), wo_ref[W_GRP * j:W_GRP * (j + 1), :],
                        preferred_element_type=F32)
    h1 = h_ref[0] + mod_ref[0, 2:3, :] * y
    xn = h1 * lax.rsqrt(jnp.mean(h1 * h1, axis=-1, keepdims=True) + EPS) * g2_ref[...]
    xm = (xn * (1.0 + mod_ref[0, 4:5, :]) + mod_ref[0, 3:4, :]).astype(BF16)
    fc = 256
    for j in range(D_FF // fc):
        gg = jnp.dot(xm, wup_ref[:, fc * j:fc * (j + 1)], preferred_element_type=F32)
        uu = jnp.dot(xm, wup_ref[:, D_FF + fc * j:D_FF + fc * (j + 1)], preferred_element_type=F32)
        part = jnp.dot((_silu(gg) * uu).astype(BF16), wdn_ref[fc * j:fc * (j + 1), :],
                       preferred_element_type=F32)
        if j == 0:
            acc_s[...] = part
        else:
            acc_s[...] += part
    h2 = h1 + mod_ref[0, 5:6, :] * acc_s[...]
    if final:
        h2 = h2 * lax.rsqrt(jnp.mean(h2 * h2, axis=-1, keepdims=True) + EPS) * fg_ref[...]
    o_ref[0] = h2


def _out_ffn_call(h, mix, glu, mods, ctx_stream, g2, wo, wup, wdn, final_g, final):
    bsz, L, _ = h.shape
    tm = min(512, L)
    mod_map = (lambda b, i: (bsz, 0, 0)) if ctx_stream else (lambda b, i: (b, 0, 0))
    tok = lambda w: pl.BlockSpec((1, tm, w), lambda b, i: (b, i, 0))
    glu_w, glu_b = glu
    return pl.pallas_call(
        functools.partial(_out_ffn_kernel, final),
        grid=(bsz, L // tm),
        in_specs=[tok(D_MODEL), tok(W_GRP), tok(W_GRP), tok(W_GRP),
                  pl.BlockSpec((tm, W_GRP), lambda b, i: (i, b)),
                  pl.BlockSpec((W_GRP, W_GRP), lambda b, i: (0, 0)),
                  pl.BlockSpec((1, W_GRP), lambda b, i: (0, 0)),
                  pl.BlockSpec((1, 6, D_MODEL), mod_map),
                  pl.BlockSpec((1, D_MODEL), lambda b, i: (0, 0)),
                  _const_spec((D_MODEL, D_MODEL)),
                  _const_spec((D_MODEL, 2 * D_FF)),
                  _const_spec((D_FF, D_MODEL)),
                  pl.BlockSpec((1, D_MODEL), lambda b, i: (0, 0))],
        out_specs=tok(D_MODEL),
        out_shape=jax.ShapeDtypeStruct((bsz, L, D_MODEL), F32),
        scratch_shapes=[pltpu.VMEM((tm, D_MODEL), F32)],
        compiler_params=_cparams(2),
        name="out_ffn",
    )(h, *mix, glu_w.astype(BF16), glu_b.reshape(1, -1), mods, g2, wo, wup, wdn, final_g)


def kernel(x, c, ctx, c_ctx, mod_w, mod_b, norm1_g, norm2_g, w_in, w_out, ssd_conv_w, ssd_conv_b, ssd_a_log, ssd_dt_bias, ssd_d, ssd_norm_g, hy_conv_w, hy_conv_b, hy_w1, hy_b1, hy_freq, hy_w2, hy_b2, hy_w3, hy_bias, ret_decay, s5_a_re, s5_a_im, s5_log_dt, s5_b_re, s5_b_im, s5_c_re, s5_c_im, s5_d, s5_glu_w, s5_glu_b, ffn_w_up, ffn_w_down, final_norm_g):
    bsz = x.shape[0]
    depth = mod_w.shape[0]
    sc = jnp.concatenate([c, c_ctx[None, :], jnp.zeros((MOD_ROWS - bsz - 1, D_MODEL), F32)], axis=0)
    mods_all = _mod_call(sc, mod_w, mod_b).reshape(depth, MOD_ROWS, 6, D_MODEL)
    fg = final_norm_g.reshape(1, -1)
    h_l, h_c = x, ctx
    o_xbc, o_dt, o_hy = W_GRP, W_GRP + SSD_XBC, W_GRP + SSD_XBC + 2 * N_HEADS
    o_ret = o_hy + HY_COLS
    o_s5 = o_ret + RET_COLS
    for i in range(depth):
        last = i == depth - 1
        wi = w_in[i]
        wdt = wi[:, o_dt:o_hy]
        wcat = jnp.concatenate([wi[:, 0:o_dt], wi[:, o_hy:o_s5 + W_GRP], wdt,
                                jnp.zeros((D_MODEL, 128 - 2 * N_HEADS), F32)], axis=1).astype(BF16)
        mods = mods_all[i]
        g1 = norm1_g[i].reshape(1, -1)
        zl, xbcl, hyl, retl, dtl, s5l = _inproj_call(h_l, g1, mods, False, wcat)
        zc, xbcc, hyc, retc, dtc, s5c = _inproj_call(h_c, g1, mods, True, wcat)
        ssd_out = _ssd_call((zc, xbcc, dtc), (zl, xbcl, dtl), ssd_conv_w[i], ssd_conv_b[i],
                            ssd_a_log[i], ssd_dt_bias[i], ssd_d[i], ssd_norm_g[i], not last)
        ret_out = _ret_call(retc, retl, ret_decay[i], not last)
        ssd_c, ssd_l = ssd_out if not last else (None, ssd_out[0])
        ret_c, ret_l = ret_out if not last else (None, ret_out[0])
        p = dict(s5_a_re=s5_a_re[i], s5_a_im=s5_a_im[i], s5_log_dt=s5_log_dt[i], s5_b_re=s5_b_re[i],
                 s5_b_im=s5_b_im[i], s5_c_re=s5_c_re[i], s5_c_im=s5_c_im[i], s5_d=s5_d[i],
                 s5_glu_w=s5_glu_w[i], s5_glu_b=s5_glu_b[i],
                 hy_conv_w=hy_conv_w[i], hy_conv_b=hy_conv_b[i], hy_w1=hy_w1[i], hy_b1=hy_b1[i],
                 hy_freq=hy_freq[i], hy_w2=hy_w2[i], hy_b2=hy_b2[i], hy_w3=hy_w3[i], hy_bias=hy_bias[i])
        s5_c, s5_l = _s5_mixer(s5c, s5l, bsz, p)
        hy_l = _hyena_mixer(hyl, p)
        g2 = norm2_g[i].reshape(1, -1)
        wo = w_out[i].astype(BF16)
        wup = ffn_w_up[i].astype(BF16)
        wdn = ffn_w_down[i].astype(BF16)
        glu = (s5_glu_w[i], s5_glu_b[i])
        h_l = _out_ffn_call(h_l, (ssd_l, hy_l, ret_l, s5_l), glu, mods, False, g2, wo, wup, wdn, fg, last)
        if not last:
            hy_c = _hyena_mixer(hyc, p)
            h_c = _out_ffn_call(h_c, (ssd_c, hy_c, ret_c, s5_c), glu, mods, True, g2, wo, wup, wdn, fg, False)
    return h_l
```

```python
import functools
import math

import numpy as np
import jax
import jax.numpy as jnp
from jax import lax
from jax.experimental import pallas as pl
from jax.experimental.pallas import tpu as pltpu

F32 = jnp.float32
BF16 = jnp.bfloat16
EPS = 1e-6

D_MODEL = 1024
W_GRP = 256
T = 128
N_HEADS = 4
HEAD_DIM = 64
SSD_STATE = 64
SSD_XBC = W_GRP + 2 * SSD_STATE
HY_COLS = 3 * W_GRP
RET_COLS = 4 * W_GRP
GRID_W = 64
ROPE_BASE = 10000.0
HY_EMB = 33
HY_BANDS = 16
HY_FILT = 64
S5_GROUPS = 16
S5_CH = 16
S5_STATE = 64
S5_NS = S5_GROUPS * S5_STATE
D_FF = 2816
P3_GROUP = 4
RET_GROUP = 8
S5_TS = 64
MOD_ROWS = 24

VMEM_LIMIT = 56 * 1024 * 1024


def _cparams(n_grid):
    return pltpu.CompilerParams(dimension_semantics=("arbitrary",) * n_grid,
                                vmem_limit_bytes=VMEM_LIMIT)


def _dot(a, b):
    return jnp.dot(a.astype(BF16), b.astype(BF16), preferred_element_type=F32)


def _dot_nt(a, b):
    return lax.dot_general(a.astype(BF16), b.astype(BF16), (((1,), (1,)), ((), ())),
                           preferred_element_type=F32)


def _dot_tn(a, b):
    return lax.dot_general(a.astype(BF16), b.astype(BF16), (((0,), (0,)), ((), ())),
                           preferred_element_type=F32)


def _dot_f32(a, b):
    return jnp.dot(a, b, preferred_element_type=F32, precision=lax.Precision.HIGHEST)


def _silu(x):
    return x * jax.nn.sigmoid(x)


def _const_spec(shape):
    nd = len(shape)
    return pl.BlockSpec(shape, lambda *_: (0,) * nd, pipeline_mode=pl.Buffered(1))


def _mod_kernel(sc_ref, w_ref, b_ref, o_ref):
    s = _silu(sc_ref[...])
    o_ref[0] = _dot_f32(s, w_ref[0]) + b_ref[0]


def _mod_call(sc, mod_w, mod_b):
    depth, _, n = mod_w.shape
    tn = 1536
    return pl.pallas_call(
        _mod_kernel,
        grid=(depth, n // tn),
        in_specs=[pl.BlockSpec((MOD_ROWS, D_MODEL), lambda l, j: (0, 0)),
                  pl.BlockSpec((1, D_MODEL, tn), lambda l, j: (l, 0, j)),
                  pl.BlockSpec((1, 1, tn), lambda l, j: (l, 0, j))],
        out_specs=pl.BlockSpec((1, MOD_ROWS, tn), lambda l, j: (l, 0, j)),
        out_shape=jax.ShapeDtypeStruct((depth, MOD_ROWS, n), F32),
        compiler_params=_cparams(2),
        name="adaln_mod",
    )(sc, mod_w, mod_b.reshape(depth, 1, n))


def _inproj_kernel(x_ref, g_ref, mod_ref, w_ref, z_ref, xbc_ref, hy_ref, ret_ref, dt_ref, s5_ref):
    x = x_ref[0]
    xn = x * lax.rsqrt(jnp.mean(x * x, axis=-1, keepdims=True) + EPS) * g_ref[...]
    xm = (xn * (1.0 + mod_ref[0, 1:2, :]) + mod_ref[0, 0:1, :]).astype(BF16)
    o = 0
    for ref, width in ((z_ref, W_GRP), (xbc_ref, SSD_XBC), (hy_ref, HY_COLS),
                       (ret_ref, RET_COLS)):
        ref[0] = jnp.dot(xm, w_ref[:, o:o + width], preferred_element_type=F32).astype(BF16)
        o += width
    s5_ref[...] = jnp.dot(xm, w_ref[:, o:o + W_GRP], preferred_element_type=F32).astype(BF16)
    o += W_GRP
    dt_ref[0] = jnp.dot(xm, w_ref[:, o:o + 128], preferred_element_type=F32)


def _inproj_call(h, g, mods, ctx_stream, wcat):
    bsz, L, _ = h.shape
    tm = min(1024, L)
    ncols = wcat.shape[1]
    mod_map = (lambda b, i: (bsz, 0, 0)) if ctx_stream else (lambda b, i: (b, 0, 0))
    tok = lambda w: pl.BlockSpec((1, tm, w), lambda b, i: (b, i, 0))
    out_shape = [jax.ShapeDtypeStruct((bsz, L, W_GRP), BF16),
                 jax.ShapeDtypeStruct((bsz, L, SSD_XBC), BF16),
                 jax.ShapeDtypeStruct((bsz, L, HY_COLS), BF16),
                 jax.ShapeDtypeStruct((bsz, L, RET_COLS), BF16),
                 jax.ShapeDtypeStruct((bsz, L, 128), F32),
                 jax.ShapeDtypeStruct((L, bsz * W_GRP), BF16)]
    out_specs = [tok(W_GRP), tok(SSD_XBC), tok(HY_COLS), tok(RET_COLS), tok(128),
                 pl.BlockSpec((tm, W_GRP), lambda b, i: (i, b))]
    return pl.pallas_call(
        _inproj_kernel,
        grid=(bsz, L // tm),
        in_specs=[tok(D_MODEL),
                  pl.BlockSpec((1, D_MODEL), lambda b, i: (0, 0)),
                  pl.BlockSpec((1, 6, D_MODEL), mod_map),
                  _const_spec((D_MODEL, ncols))],
        out_specs=out_specs,
        out_shape=out_shape,
        compiler_params=_cparams(2),
        name="in_proj",
    )(h, g, mods, wcat)


def _halo_rows(ref, s, c, nc, L):
    sp = pl.multiple_of(jnp.maximum(s - 16, 0), 16)
    prev = ref[0, pl.ds(sp, 16), :][15:16, :].astype(F32)
    prev = jnp.where(c > 0, prev, 0.0)
    sn = pl.multiple_of(jnp.minimum(s + T, L - 16), 16)
    nxt = ref[0, pl.ds(sn, 16), :][0:1, :].astype(F32)
    nxt = jnp.where(c < nc - 1, nxt, 0.0)
    return prev, nxt


def _dwconv_chunk(x, prev, nxt, w_ref, b_ref):
    n = x.shape[0]
    row = lax.broadcasted_iota(jnp.int32, x.shape, 0)
    up = jnp.where(row == 0, prev, pltpu.roll(x, 1, 0))
    dn = jnp.where(row == n - 1, nxt, pltpu.roll(x, n - 1, 0))
    return up * w_ref[0:1, :] + x * w_ref[1:2, :] + dn * w_ref[2:3, :] + b_ref[...]


def _expand_heads(c, exp_ref):
    hi = c.astype(BF16)
    lo = (c - hi.astype(F32)).astype(BF16)
    return jnp.dot(jnp.concatenate([hi, lo], axis=1), exp_ref[...], preferred_element_type=F32)


def _state_recurrence(hf_ref, hb_ref, decf_ref, decb_ref, nc, h0f, h0b):
    hf_ref[0] = h0f
    hb_ref[nc] = h0b

    def fwd(c, carry):
        hf_ref[c + 1] = decf_ref[c] * hf_ref[c] + hf_ref[c + 1]
        return carry

    lax.fori_loop(0, nc, fwd, 0)

    def bwd(k, carry):
        c = nc - 1 - k
        hb_ref[c] = decb_ref[c] * hb_ref[c + 1] + hb_ref[c]
        return carry

    lax.fori_loop(0, nc, bwd, 0)
    return hf_ref[nc], hb_ref[0]


def _split3(x):
    hi = x.astype(BF16)
    r = x - hi.astype(F32)
    mid = r.astype(BF16)
    lo = (r - mid.astype(F32)).astype(BF16)
    return jnp.concatenate([hi, mid, lo], axis=1)


def _tile_heads_bd(x, bdmask):
    xb = x.astype(BF16)
    return jnp.where(bdmask, jnp.concatenate([xb] * N_HEADS, axis=0), 0)


def _rows_to_lanes(a, lo):
    return jnp.concatenate([a[lo + h:lo + h + 1, :] for h in range(N_HEADS)], axis=1)


def _ssd_sequence(L, z_ref, xbc_ref, dt_ref, y_ref, prm, scr, h0f, h0b):
    (cw_ref, cb_ref, alog_row, bias_row, dskip_ref, ng_ref, exp_ref, exp128_ref, sel_ref, tri_ref) = prm
    (xs_s, bc_s, ee_s, acs_s, dt_s, hf_s, hb_s, decf_s, decb_s) = scr
    nc = L // T
    lane = lax.broadcasted_iota(jnp.int32, (T, 128), 1)
    a_row = -jnp.exp(alog_row[...])

    grp = min(P3_GROUP, nc)

    def phase1(p, carry):
        cs = [p * grp + j for j in range(grp)]
        rows = [pl.ds(pl.multiple_of(c * T, T), T) for c in cs]
        pre = []
        for r in rows:
            dt = jax.nn.softplus(dt_ref[0, r, :] + bias_row[...])
            dt_s[r, :] = dt
            la = dt * a_row
            hi = la.astype(BF16)
            r1 = la - hi.astype(F32)
            mid = r1.astype(BF16)
            lo = (r1 - mid.astype(F32)).astype(BF16)
            acs_f = jnp.dot(tri_ref[...], jnp.concatenate([hi, mid, lo], axis=0),
                            preferred_element_type=F32)
            pre.append((dt, la, acs_f))
        st = []
        for c, r, (dt, la, acs_f) in zip(cs, rows, pre):
            prev, nxt = _halo_rows(xbc_ref, pl.multiple_of(c * T, T), c, nc, L)
            xact = _silu(_dwconv_chunk(xbc_ref[0, r, :].astype(F32), prev, nxt, cw_ref, cb_ref))
            xs_s[r, :] = xact[:, 0:W_GRP]
            bc_s[r, :] = xact[:, W_GRP:SSD_XBC]
            tot = acs_f[T - 1:T, :]
            acs = jnp.where(lane < N_HEADS, acs_f, tot - acs_f + la)
            acs_s[r, :] = acs
            st.append((xact, _expand_heads(jnp.exp(acs), exp_ref),
                       _expand_heads(dt * jnp.exp(tot - acs), exp_ref)))
        for c, r, (xact, ee, wx) in zip(cs, rows, st):
            xs = xact[:, 0:W_GRP]
            bm = xact[:, W_GRP:W_GRP + SSD_STATE]
            ee_s[r, :] = ee
            hf_s[c + 1] = _dot_tn(bm, xs * wx[:, 0:W_GRP])
            hb_s[c] = _dot_tn(bm, xs * wx[:, W_GRP:2 * W_GRP])
            decf_s[c] = ee[T - 1:T, 0:W_GRP]
            decb_s[c] = ee[0:1, W_GRP:2 * W_GRP]
        return carry

    lax.fori_loop(0, nc // grp, phase1, 0)
    hf_fin, hb_fin = _state_recurrence(hf_s, hb_s, decf_s, decb_s, nc, h0f, h0b)
    if y_ref is None:
        return hf_fin, hb_fin

    ri = lax.broadcasted_iota(jnp.int32, (T, N_HEADS * T), 0)
    ci = lax.broadcasted_iota(jnp.int32, (T, N_HEADS * T), 1) % T
    strict_lower = ci < ri
    diag = ci == ri
    r4 = lax.broadcasted_iota(jnp.int32, (N_HEADS * T, W_GRP), 0) // T
    c4 = lax.broadcasted_iota(jnp.int32, (N_HEADS * T, W_GRP), 1) // HEAD_DIM
    bdmask = r4 == c4
    nl = N_HEADS * T

    nt = (((1,), (1,)), ((), ()))

    def phase3(p, carry):
        cs = [p * grp + j for j in range(grp)]
        rows = [pl.ds(pl.multiple_of(c * T, T), T) for c in cs]
        st = []
        for c, r in zip(cs, rows):
            bc = bc_s[r, :]
            bm = bc[:, 0:SSD_STATE]
            cm = bc[:, SSD_STATE:2 * SSD_STATE]
            a3 = _split3(acs_s[r, :])
            col = jnp.dot(a3, exp128_ref[...], preferred_element_type=F32)
            acst = lax.dot_general(sel_ref[...], a3, nt, preferred_element_type=F32)
            dtt = lax.dot_general(sel_ref[...], _split3(dt_s[r, :]), nt,
                                  preferred_element_type=F32)
            inter = (_dot(cm, hf_s[c]), _dot(cm, hb_s[c + 1]))
            st.append((col, acst, dtt, _dot_nt(cm, bm), inter))
        ys = []
        for r, (col, acst, dtt, g, inter) in zip(rows, st):
            g4 = jnp.concatenate([g] * N_HEADS, axis=1)
            shifted = acst - jnp.log(dtt)
            arg = jnp.where(strict_lower, col[:, 0:nl] - _rows_to_lanes(shifted, 0),
                            col[:, nl:2 * nl] - _rows_to_lanes(shifted, N_HEADS))
            w = g4 * (jnp.exp(arg) + jnp.where(diag, _rows_to_lanes(dtt, 0), 0.0))
            ys.append(jnp.dot(w.astype(BF16), _tile_heads_bd(xs_s[r, :], bdmask),
                              preferred_element_type=F32))
        for r, y, (_, _, _, _, inter) in zip(rows, ys, st):
            ee = ee_s[r, :]
            xs = xs_s[r, :]
            y = y + ee[:, 0:W_GRP] * inter[0] + ee[:, W_GRP:2 * W_GRP] * inter[1]
            y = y + xs * dskip_ref[...]
            y = y * _silu(z_ref[0, r, :].astype(F32))
            y = y * lax.rsqrt(jnp.mean(y * y, axis=-1, keepdims=True) + EPS) * ng_ref[...]
            y_ref[0, r, :] = y.astype(y_ref.dtype)
        return carry

    lax.fori_loop(0, nc // grp, phase3, 0)
    return hf_fin, hb_fin


def _ssd_kernel(Lc, L, ctx_out, zc_ref, xbcc_ref, dtc_ref, zl_ref, xbcl_ref, dtl_ref,
                cw_ref, cb_ref, alog_row, bias_row, dskip_ref, ng_ref, exp_ref, exp128_ref, sel_ref,
                tri_ref, *rest):
    yc_ref, yl_ref, scr = (rest[0], rest[1], rest[2:]) if ctx_out else (None, rest[0], rest[1:])
    prm = (cw_ref, cb_ref, alog_row, bias_row, dskip_ref, ng_ref, exp_ref, exp128_ref, sel_ref, tri_ref)
    zero = jnp.zeros((SSD_STATE, W_GRP), F32)
    hf, hb = _ssd_sequence(Lc, zc_ref, xbcc_ref, dtc_ref, yc_ref, prm, scr, zero, zero)
    _ssd_sequence(L, zl_ref, xbcl_ref, dtl_ref, yl_ref, prm, scr, hf, hb)


@functools.lru_cache(maxsize=None)
def _ssd_tables():
    exp64 = np.zeros((256, 512), np.float32)
    exp128 = np.zeros((384, 2 * N_HEADS * T), np.float32)
    sel = np.zeros((8, 384), np.float32)
    for r in range(2):
        for h in range(N_HEADS):
            m = r * N_HEADS + h
            for part in range(2):
                exp64[128 * part + m, r * 256 + 64 * h:r * 256 + 64 * (h + 1)] = 1.0
            for part in range(3):
                exp128[128 * part + m, (r * N_HEADS + h) * T:(r * N_HEADS + h + 1) * T] = 1.0
                sel[m, 128 * part + m] = 1.0
    tri = np.tile(np.tril(np.ones((T, T), np.float32)), (1, 3))
    return exp64, exp128, sel, tri


def _pad_row(v, n=128):
    v = v.reshape(1, -1)
    return jnp.pad(v, ((0, 0), (0, n - v.shape[1])))


def _ssd_call(uc, ul, conv_w, conv_b, a_log, dt_bias, d_skip, norm_g, ctx_out):
    zc, xbcc, dtc = uc
    zl, xbcl, dtl = ul
    bsz, Lc, _ = zc.shape
    L = zl.shape[1]
    nc = L // T
    alog_row = _pad_row(a_log)
    bias_row = _pad_row(dt_bias)
    dskip = jnp.repeat(d_skip, HEAD_DIM).reshape(1, W_GRP)
    exp64, exp128, sel, tri = (jnp.asarray(t).astype(BF16) for t in _ssd_tables())

    def seq(Lx, w):
        return pl.BlockSpec((1, Lx, w), lambda b: (b, 0, 0))

    def small(shape):
        return pl.BlockSpec(shape, lambda b: (0,) * len(shape))

    in_specs = [seq(Lc, W_GRP), seq(Lc, SSD_XBC), seq(Lc, 128),
                seq(L, W_GRP), seq(L, SSD_XBC), seq(L, 128),
                small((3, SSD_XBC)), small((1, SSD_XBC)), small((1, 128)),
                small((1, 128)), small((1, W_GRP)), small((1, W_GRP)),
                small(exp64.shape), small(exp128.shape), small(sel.shape), small(tri.shape)]
    scratch = [pltpu.VMEM((L, W_GRP), F32), pltpu.VMEM((L, 128), F32), pltpu.VMEM((L, 512), F32),
               pltpu.VMEM((L, 128), F32), pltpu.VMEM((L, 128), F32),
               pltpu.VMEM((nc + 1, SSD_STATE, W_GRP), F32), pltpu.VMEM((nc + 1, SSD_STATE, W_GRP), F32),
               pltpu.VMEM((nc, 1, W_GRP), F32), pltpu.VMEM((nc, 1, W_GRP), F32)]
    return pl.pallas_call(
        functools.partial(_ssd_kernel, Lc, L, ctx_out),
        grid=(bsz,),
        in_specs=in_specs,
        out_specs=([seq(Lc, W_GRP)] if ctx_out else []) + [seq(L, W_GRP)],
        out_shape=([jax.ShapeDtypeStruct((bsz, Lc, W_GRP), BF16)] if ctx_out else [])
                  + [jax.ShapeDtypeStruct((bsz, L, W_GRP), BF16)],
        scratch_shapes=scratch,
        compiler_params=_cparams(1),
        name="ssd_mixer",
    )(zc, xbcc, dtc, zl, xbcl, dtl, conv_w, conv_b.reshape(1, -1),
      alog_row, bias_row, dskip, norm_g.reshape(1, -1), exp64, exp128, sel, tri)


def _ret_sequence(L, rope, q_ref, k_ref, v_ref, g_ref, y_ref, cos_ref, sin_ref, perm_ref, avg_ref,
                  tabs, scr, h0f, h0b):
    (ef, eb, wf, wb, decf, decb, bdmask) = tabs
    (qr_s, kr_s, dm_s, hf_s, hb_s) = scr
    nc = L // T
    scale = HEAD_DIM ** -0.5
    r4 = lax.broadcasted_iota(jnp.int32, (N_HEADS * T, W_GRP), 0) // T
    c4 = lax.broadcasted_iota(jnp.int32, (N_HEADS * T, W_GRP), 1) // HEAD_DIM
    stackmask = r4 == c4

    def rot(x, rows):
        partner = jnp.dot(x, perm_ref[...], preferred_element_type=F32)
        return x.astype(F32) * cos_ref[rows, :] + partner * sin_ref[rows, :]

    grp = min(RET_GROUP, nc)

    def phase1(p, carry):
        cs = [p * grp + j for j in range(grp)]
        rows = [pl.ds(pl.multiple_of(c * T, T), T) for c in cs]
        qk = [(q_ref[0, r, :], k_ref[0, r, :]) for r in rows]
        if rope:
            qk = [(rot(q, r), rot(k, r)) for (q, k), r in zip(qk, rows)]
        for c, r, (q, k) in zip(cs, rows, qk):
            k = k.astype(F32) * scale
            qr_s[r, :] = q.astype(F32)
            kr_s[r, :] = k
            v = v_ref[0, r, :]
            hf_s[c + 1] = _dot_tn(k * wf, v) * bdmask
            hb_s[c] = _dot_tn(k * wb, v) * bdmask
        return carry

    lax.fori_loop(0, nc // grp, phase1, 0)

    hf_s[0] = h0f
    hb_s[nc] = h0b

    def fwd(c, carry):
        hf_s[c + 1] = decf * hf_s[c] + hf_s[c + 1]
        return carry

    lax.fori_loop(0, nc, fwd, 0)

    def bwd(kk, carry):
        c = nc - 1 - kk
        hb_s[c] = decb * hb_s[c + 1] + hb_s[c]
        return carry

    lax.fori_loop(0, nc, bwd, 0)
    if y_ref is None:
        return hf_s[nc], hb_s[0]

    def gmean(x):
        hi = x.astype(BF16)
        lo = (x - hi.astype(F32)).astype(BF16)
        return jnp.dot(jnp.concatenate([hi, lo], axis=1), avg_ref[...], preferred_element_type=F32)

    def gmean_sq(x):
        return jnp.dot((x * x).astype(BF16), avg_ref[0:W_GRP, :], preferred_element_type=F32)

    def phase3(p, carry):
        cs = [p * grp + j for j in range(grp)]
        rows = [pl.ds(pl.multiple_of(c * T, T), T) for c in cs]
        qs = [qr_s[r, :] for r in rows]
        scs = [lax.dot_general(q.astype(BF16), _tile_heads_bd(kr_s[r, :], stackmask),
                               (((1,), (1,)), ((), ())), preferred_element_type=F32)
               for q, r in zip(qs, rows)]
        inters = [_dot(q * ef, hf_s[c]) + _dot(q * eb, hb_s[c + 1]) for q, c in zip(qs, cs)]
        ys = [jnp.dot((sc * dm_s[...]).astype(BF16), _tile_heads_bd(v_ref[0, r, :], stackmask),
                      preferred_element_type=F32) + it
              for sc, r, it in zip(scs, rows, inters)]
        ycs = [y - gmean(y) for y in ys]
        vars_ = [gmean_sq(yc) for yc in ycs]
        for r, yc, var in zip(rows, ycs, vars_):
            y_ref[0, r, :] = (_silu(g_ref[0, r, :].astype(F32))
                              * (yc * lax.rsqrt(var + EPS))).astype(y_ref.dtype)
        return carry

    lax.fori_loop(0, nc // grp, phase3, 0)
    return hf_s[nc], hb_s[0]


def _ret_kernel(Lc, L, ctx_out, qc_ref, kc_ref, vc_ref, gc_ref, ql_ref, kl_ref, vl_ref, gl_ref,
                cos_ref, sin_ref, dec_ref, perm_ref, avg_ref, *rest):
    yc_ref, rest = (rest[0], rest[1:]) if ctx_out else (None, rest)
    yl_ref, qr_s, kr_s, dm_s, hf_s, hb_s = rest
    lg = -jnp.exp(dec_ref[...])
    lgf = lg[0:1, :]
    lgb = lg[1:2, :]
    i = lax.broadcasted_iota(jnp.int32, (T, W_GRP), 0).astype(F32)
    ef = jnp.exp(lgf * (i + 1.0))
    eb = jnp.exp(lgb * (T - i))
    wf = jnp.exp(lgf * (T - 1.0 - i))
    wb = jnp.exp(lgb * i)
    decf = jnp.exp(lgf * float(T))
    decb = jnp.exp(lgb * float(T))
    r2 = lax.broadcasted_iota(jnp.int32, (W_GRP, W_GRP), 0) // HEAD_DIM
    c2 = lax.broadcasted_iota(jnp.int32, (W_GRP, W_GRP), 1) // HEAD_DIM
    bdmask = (r2 == c2).astype(F32)
    ri = lax.broadcasted_iota(jnp.int32, (T, T), 0)
    ci = lax.broadcasted_iota(jnp.int32, (T, T), 1)
    d = (ri - ci).astype(F32)
    for h in range(N_HEADS):
        lf = lgf[:, HEAD_DIM * h:HEAD_DIM * h + 1]
        lb = lgb[:, HEAD_DIM * h:HEAD_DIM * h + 1]
        dm_s[:, T * h:T * (h + 1)] = (jnp.exp(jnp.where(ci <= ri, lf * d, -jnp.inf))
                                      + jnp.exp(jnp.where(ci >= ri, -lb * d, -jnp.inf)))
    tabs = (ef, eb, wf, wb, decf, decb, bdmask)
    scr = (qr_s, kr_s, dm_s, hf_s, hb_s)
    zero = jnp.zeros((W_GRP, W_GRP), F32)
    hf, hb = _ret_sequence(Lc, False, qc_ref, kc_ref, vc_ref, gc_ref, yc_ref, cos_ref, sin_ref,
                           perm_ref, avg_ref, tabs, scr, zero, zero)
    _ret_sequence(L, True, ql_ref, kl_ref, vl_ref, gl_ref, yl_ref, cos_ref, sin_ref,
                  perm_ref, avg_ref, tabs, scr, hf, hb)


@functools.lru_cache(maxsize=None)
def _rope_tables(L):
    t = np.arange(L)
    f = 16
    inv = (ROPE_BASE ** (-np.arange(f, dtype=np.float32) / f)).astype(np.float32)
    cos = np.zeros((L, HEAD_DIM), np.float32)
    sin = np.zeros((L, HEAD_DIM), np.float32)
    for base, pos in ((0, t // GRID_W), (32, t % GRID_W)):
        ang = pos.astype(np.float32)[:, None] * inv[None, :]
        ang = ang.astype(np.float32).astype(np.float64)
        cos[:, base:base + f] = np.cos(ang)
        cos[:, base + f:base + 2 * f] = np.cos(ang)
        sin[:, base:base + f] = -np.sin(ang)
        sin[:, base + f:base + 2 * f] = np.sin(ang)
    return np.tile(cos, (1, N_HEADS)), np.tile(sin, (1, N_HEADS))


@functools.lru_cache(maxsize=None)
def _ret_tables():
    perm = np.zeros((W_GRP, W_GRP), np.float32)
    avg = np.zeros((2 * W_GRP, W_GRP), np.float32)
    for l in range(W_GRP):
        src = l + 16 if (l % 32) < 16 else l - 16
        perm[src, l] = 1.0
        g = l // HEAD_DIM
        for part in range(2):
            avg[part * W_GRP + g * HEAD_DIM:part * W_GRP + (g + 1) * HEAD_DIM, l] = 1.0 / HEAD_DIM
    return perm, avg


def _ret_call(uc, ul, decay_param, ctx_out):
    bsz, Lc, _ = uc.shape
    L = ul.shape[1]
    nc = L // T
    cos, sin = _rope_tables(L)
    perm, avg = (jnp.asarray(t).astype(BF16) for t in _ret_tables())
    dec = jnp.repeat(decay_param, HEAD_DIM, axis=1)

    def col(Lx, j):
        return pl.BlockSpec((1, Lx, W_GRP), lambda b, j=j: (b, 0, j))

    def seq(Lx):
        return pl.BlockSpec((1, Lx, W_GRP), lambda b: (b, 0, 0))

    in_specs = ([col(Lc, j) for j in range(4)] + [col(L, j) for j in range(4)]
                + [pl.BlockSpec((L, W_GRP), lambda b: (0, 0)), pl.BlockSpec((L, W_GRP), lambda b: (0, 0)),
                   pl.BlockSpec((2, W_GRP), lambda b: (0, 0)),
                   pl.BlockSpec((W_GRP, W_GRP), lambda b: (0, 0)),
                   pl.BlockSpec((2 * W_GRP, W_GRP), lambda b: (0, 0))])
    scratch = [pltpu.VMEM((L, W_GRP), F32), pltpu.VMEM((L, W_GRP), F32),
               pltpu.VMEM((T, N_HEADS * T), F32),
               pltpu.VMEM((nc + 1, W_GRP, W_GRP), F32), pltpu.VMEM((nc + 1, W_GRP, W_GRP), F32)]
    return pl.pallas_call(
        functools.partial(_ret_kernel, Lc, L, ctx_out),
        grid=(bsz,),
        in_specs=in_specs,
        out_specs=([seq(Lc)] if ctx_out else []) + [seq(L)],
        out_shape=([jax.ShapeDtypeStruct((bsz, Lc, W_GRP), BF16)] if ctx_out else [])
                  + [jax.ShapeDtypeStruct((bsz, L, W_GRP), BF16)],
        scratch_shapes=scratch,
        compiler_params=_cparams(1),
        name="retention_mixer",
    )(uc, uc, uc, uc, ul, ul, ul, ul, jnp.asarray(cos), jnp.asarray(sin), dec, perm, avg)


def _s5_prep_kernel(are_ref, aim_ref, ldt_ref, bre_ref, bim_ref, cre_ref, cim_ref,
                    bmat_ref, cmat_ref, ab_ref):
    a_re = are_ref[0]
    a_im = aim_ref[0]
    dt = jnp.exp(ldt_ref[0])
    mag = jnp.exp(a_re * dt)
    ab_re = mag * jnp.cos(a_im * dt)
    ab_im = mag * jnp.sin(a_im * dt)
    den = a_re * a_re + a_im * a_im
    z_re = ((ab_re - 1.0) * a_re + ab_im * a_im) / den
    z_im = (ab_im * a_re - (ab_re - 1.0) * a_im) / den
    b_re = bre_ref[...]
    b_im = bim_ref[...]
    bmat_ref[0, :, 0:S5_NS] = (b_re * z_re - b_im * z_im).astype(BF16)
    bmat_ref[0, :, S5_NS:2 * S5_NS] = (b_re * z_im + b_im * z_re).astype(BF16)
    cmat_ref[0, 0:S5_NS, :] = cre_ref[0].astype(BF16)
    cmat_ref[0, S5_NS:2 * S5_NS, :] = (-cim_ref[0]).astype(BF16)
    ab_ref[0, :, 0:S5_NS] = ab_re
    ab_ref[0, :, S5_NS:2 * S5_NS] = ab_im


def _s5_prep_call(a_re, a_im, log_dt, b_re, b_im, c_re, c_im):
    eye = jnp.eye(S5_GROUPS, dtype=F32)
    b_re_bd = jnp.einsum('gpc,gh->gchp', b_re, eye).reshape(W_GRP, S5_NS)
    b_im_bd = jnp.einsum('gpc,gh->gchp', b_im, eye).reshape(W_GRP, S5_NS)
    c_re_bd = jnp.einsum('rgcp,gh->rgphc', c_re, eye).reshape(2, S5_NS, W_GRP)
    c_im_bd = jnp.einsum('rgcp,gh->rgphc', c_im, eye).reshape(2, S5_NS, W_GRP)
    ldt = jnp.repeat(log_dt, S5_STATE, axis=1).reshape(2, 1, S5_NS)
    row = lambda: pl.BlockSpec((1, 1, S5_NS), lambda r: (r, 0, 0))
    return pl.pallas_call(
        _s5_prep_kernel,
        grid=(2,),
        in_specs=[row(), row(), row(),
                  pl.BlockSpec((W_GRP, S5_NS), lambda r: (0, 0)), pl.BlockSpec((W_GRP, S5_NS), lambda r: (0, 0)),
                  pl.BlockSpec((1, S5_NS, W_GRP), lambda r: (r, 0, 0)),
                  pl.BlockSpec((1, S5_NS, W_GRP), lambda r: (r, 0, 0))],
        out_specs=[pl.BlockSpec((1, W_GRP, 2 * S5_NS), lambda r: (r, 0, 0)),
                   pl.BlockSpec((1, 2 * S5_NS, W_GRP), lambda r: (r, 0, 0)),
                   pl.BlockSpec((1, 1, 2 * S5_NS), lambda r: (r, 0, 0))],
        out_shape=[jax.ShapeDtypeStruct((2, W_GRP, 2 * S5_NS), BF16),
                   jax.ShapeDtypeStruct((2, 2 * S5_NS, W_GRP), BF16),
                   jax.ShapeDtypeStruct((2, 1, 2 * S5_NS), F32)],
        compiler_params=_cparams(1),
        name="s5_prep",
    )(a_re.reshape(2, 1, S5_NS), a_im.reshape(2, 1, S5_NS), ldt, b_re_bd, b_im_bd, c_re_bd, c_im_bd)


def _s5_pipe_kernel(bsz, reverse, u_ref, h0_ref, bmat_ref, cmat_ref, ab_ref, *rest):
    if reverse:
        y_ref, hfin_ref, x0_s, x1_s, hb0_s, hb1_s, hs_s = rest
    else:
        ul_ref, yb_ref, d_ref, y_ref, hfin_ref, x0_s, x1_s, hb0_s, hb1_s, hs_s = rest
    g = pl.program_id(0)
    rows_blk = S5_TS * bsz

    @pl.when(g == 0)
    def _():
        x1_s[...] = jnp.zeros_like(x1_s)
        hb0_s[...] = jnp.zeros_like(hb0_s)
        hb1_s[...] = jnp.zeros_like(hb1_s)
        hs_s[...] = jnp.zeros_like(hs_s)

    def half(k):
        first = (1 - k) if reverse else k
        return slice(first * S5_TS, (first + 1) * S5_TS)

    def emit(k, hb_s):
        y = jnp.dot(hb_s[...], cmat_ref[0], preferred_element_type=F32)
        if not reverse:
            y = (y + yb_ref[half(k), :, :].reshape(rows_blk, W_GRP)
                 + ul_ref[half(k), :, :].reshape(rows_blk, W_GRP).astype(F32) * d_ref[...])
        y_ref[half(k), :, :] = y.reshape(S5_TS, bsz, W_GRP)

    def project(k, x_s):
        u = u_ref[half(k), :, :].reshape(rows_blk, W_GRP)
        x_s[...] = jnp.dot(u, bmat_ref[0], preferred_element_type=F32)

    def scan(x_s, hb_s):
        for q in range(S5_NS // W_GRP):
            cr = slice(W_GRP * q, W_GRP * (q + 1))
            cim = slice(S5_NS + W_GRP * q, S5_NS + W_GRP * (q + 1))
            a_re = jnp.broadcast_to(ab_ref[0, :, cr], (bsz, W_GRP))
            a_im = jnp.broadcast_to(ab_ref[0, :, cim], (bsz, W_GRP))
            h_re = hs_s[:, cr]
            h_im = hs_s[:, cim]
            for kk in range(S5_TS):
                t = (S5_TS - 1 - kk) if reverse else kk
                rows = slice(t * bsz, (t + 1) * bsz)
                h_re, h_im = (a_re * h_re - a_im * h_im + x_s[rows, cr],
                              a_re * h_im + a_im * h_re + x_s[rows, cim])
                hb_s[rows, cr] = h_re.astype(BF16)
                hb_s[rows, cim] = h_im.astype(BF16)
            hs_s[:, cr] = h_re
            hs_s[:, cim] = h_im

    emit(0, hb0_s)
    project(0, x0_s)
    scan(x1_s, hb1_s)
    hfin_ref[...] = hs_s[...]
    hs_s[...] = jnp.where(g == 0, h0_ref[...], hs_s[...])
    emit(1, hb1_s)
    project(1, x1_s)
    scan(x0_s, hb0_s)


def _s5_pipe_call(r, u3, h0, bmat, cmat, ab, post=None):
    L, bsz, _ = u3.shape
    npair = L // (2 * S5_TS)
    blk = (2 * S5_TS, bsz, W_GRP)
    reverse = r == 1

    def pair_of(c):
        return npair - 1 - c if reverse else c

    cur = lambda i: (pair_of(jnp.minimum(i, npair - 1)), 0, 0)
    lag = lambda i: (pair_of(jnp.maximum(i - 1, 0)), 0, 0)
    state = pl.BlockSpec((bsz, 2 * S5_NS), lambda i: (0, 0))
    in_specs = [pl.BlockSpec(blk, cur), state,
                pl.BlockSpec((1, W_GRP, 2 * S5_NS), lambda i: (r, 0, 0)),
                pl.BlockSpec((1, 2 * S5_NS, W_GRP), lambda i: (r, 0, 0)),
                pl.BlockSpec((1, 1, 2 * S5_NS), lambda i: (r, 0, 0))]
    args = [u3, h0, bmat, cmat, ab]
    if not reverse:
        yb, d = post
        in_specs += [pl.BlockSpec(blk, lag), pl.BlockSpec(blk, lag),
                     pl.BlockSpec((1, W_GRP), lambda i: (0, 0))]
        args += [u3, yb, d.reshape(1, -1)]
    xbuf = pltpu.VMEM((S5_TS * bsz, 2 * S5_NS), F32)
    hbuf = pltpu.VMEM((S5_TS * bsz, 2 * S5_NS), BF16)
    return pl.pallas_call(
        functools.partial(_s5_pipe_kernel, bsz, reverse),
        grid=(npair + 1,),
        in_specs=in_specs,
        out_specs=[pl.BlockSpec(blk, lag), state],
        out_shape=[jax.ShapeDtypeStruct((L, bsz, W_GRP), F32),
                   jax.ShapeDtypeStruct((bsz, 2 * S5_NS), F32)],
        scratch_shapes=[xbuf, xbuf, hbuf, hbuf, pltpu.VMEM((bsz, 2 * S5_NS), F32)],
        compiler_params=_cparams(1),
        name="s5_bwd" if reverse else "s5_fwd",
    )(*args)


def _s5_mixer(u5c, u5l, bsz, p):
    bmat, cmat, ab = _s5_prep_call(p['s5_a_re'], p['s5_a_im'], p['s5_log_dt'], p['s5_b_re'],
                                   p['s5_b_im'], p['s5_c_re'], p['s5_c_im'])
    Lc, L = u5c.shape[0], u5l.shape[0]
    u3c = u5c.reshape(Lc, bsz, W_GRP)
    u3l = u5l.reshape(L, bsz, W_GRP)
    h0 = jnp.zeros((bsz, 2 * S5_NS), F32)
    ybc, hbc = _s5_pipe_call(1, u3c, h0, bmat, cmat, ab)
    ybl, _ = _s5_pipe_call(1, u3l, hbc, bmat, cmat, ab)
    s5c, hfc = _s5_pipe_call(0, u3c, h0, bmat, cmat, ab, post=(ybc, p['s5_d']))
    s5l, _ = _s5_pipe_call(0, u3l, hfc, bmat, cmat, ab, post=(ybl, p['s5_d']))
    return s5c.reshape(Lc, bsz * W_GRP), s5l.reshape(L, bsz * W_GRP)


@functools.lru_cache(maxsize=None)
def _dft_tables(L):
    k = np.arange(L, dtype=np.int64)
    ft = (k[:, None] * k[None, :]) % (2 * L)
    ang = ft.astype(np.float64) * (math.pi / L)
    return np.cos(ang).astype(np.float32), np.sin(ang).astype(np.float32)


@functools.lru_cache(maxsize=None)
def _dft_tables_split(L):
    c, s = _dft_tables(L)
    return (np.concatenate([c[0::2], c[1::2]], axis=0), np.concatenate([s[0::2], s[1::2]], axis=0))


@functools.lru_cache(maxsize=None)
def _hyena_consts(L):
    t = np.linspace(0.0, 1.0, L, dtype=np.float32)[:, None]
    w = (2.0 * math.pi * np.arange(L, dtype=np.float32)[:, None] / L).astype(np.float32)
    bands = np.linspace(1e-4, HY_BANDS - 1, HY_BANDS, dtype=np.float32)[None, :]
    bw = (bands * w).astype(np.float32).astype(np.float64)
    feats = np.zeros((L, 128), np.float32)
    feats[:, 0:1] = t
    feats[:, 1:1 + HY_BANDS] = np.cos(bw)
    feats[:, 1 + HY_BANDS:HY_EMB] = -np.sin(bw)
    max_decay = math.log(1e-2) / 0.3
    min_decay = math.log(1e-2) / 1.5
    deltas = np.abs(np.linspace(min_decay, max_decay, 4 * W_GRP, dtype=np.float32))[None, :]
    return feats, deltas.astype(np.float32)


HY_RB = 256


def _hy_filter_kernel(feats_ref, w1_ref, b1_ref, fr_ref, w2_ref, b2_ref, w3_ref, del_ref,
                      p_ref, q_ref, nrm_ref, an_ref):
    i = pl.program_id(0)
    feats = feats_ref[...]
    fr = fr_ref[...]
    h = jnp.sin(fr * (_dot_f32(feats, w1_ref[...]) + b1_ref[...]))
    h = jnp.sin(fr * (_dot_f32(h, w2_ref[...]) + b2_ref[...]))
    h = _dot_f32(h, w3_ref[...])
    h = h * jnp.exp(-feats[:, 0:1] * del_ref[...])
    half = 2 * W_GRP
    hf = h[:, 0:half]
    row = lax.broadcasted_iota(jnp.int32, (HY_RB, half), 0) + i * HY_RB
    hb = jnp.where(row == 0, 0.0, h[:, half:2 * half])
    p = hf + hb
    sign = (1 - 2 * (row % 2)).astype(F32)
    p_ref[...] = p.astype(BF16)
    q_ref[...] = (hb - hf).astype(BF16)

    @pl.when(i == 0)
    def _():
        nrm_ref[...] = jnp.full_like(nrm_ref, EPS)
        an_ref[...] = jnp.zeros_like(an_ref)

    nrm_ref[...] += (jnp.sum(jnp.abs(hf), axis=0, keepdims=True)
                     + jnp.sum(jnp.abs(hb), axis=0, keepdims=True))
    an_ref[...] += jnp.sum(p * sign, axis=0, keepdims=True)


def _hy_spectrum_kernel(L, c_ref, s_ref, p_ref, q_ref, nrm_ref, ans_ref, a_ref, bc_ref, an_ref):
    i = pl.program_id(0)
    n = 2.0 * L
    inv = 1.0 / nrm_ref[...]
    row = lax.broadcasted_iota(jnp.int32, a_ref.shape, 0) + i * HY_RB
    wv = jnp.where(row == 0, 1.0 / n, 2.0 / n) * inv
    a_ref[...] = wv * jnp.dot(c_ref[...], p_ref[...], preferred_element_type=F32)
    bc_ref[...] = wv * jnp.dot(s_ref[...], q_ref[...], preferred_element_type=F32)
    an_ref[...] = ans_ref[...] * inv / n


def _hy_filter_call(L, w1, b1, freq, w2, b2, w3, cmat, smat):
    feats, deltas = _hyena_consts(L)
    w1p = jnp.pad(w1, ((0, 128 - HY_EMB), (0, 0)))
    half = 2 * W_GRP
    nb = L // HY_RB
    full = lambda a: pl.BlockSpec(a.shape, lambda i, nd=a.ndim: (0,) * nd)
    small = (w1p, b1.reshape(1, -1), freq.reshape(1, -1), w2, b2.reshape(1, -1), w3, jnp.asarray(deltas))
    rowblk = lambda w: pl.BlockSpec((HY_RB, w), lambda i: (i, 0))
    vec = pl.BlockSpec((1, half), lambda i: (0, 0))
    p, q, nrm, ans = pl.pallas_call(
        _hy_filter_kernel,
        grid=(nb,),
        in_specs=[rowblk(128)] + [full(a) for a in small],
        out_specs=[rowblk(half), rowblk(half), vec, vec],
        out_shape=[jax.ShapeDtypeStruct((L, half), BF16), jax.ShapeDtypeStruct((L, half), BF16),
                   jax.ShapeDtypeStruct((1, half), F32), jax.ShapeDtypeStruct((1, half), F32)],
        compiler_params=_cparams(1),
        name="hyena_filter",
    )(jnp.asarray(feats), *small)
    return pl.pallas_call(
        functools.partial(_hy_spectrum_kernel, L),
        grid=(nb,),
        in_specs=[rowblk(L), rowblk(L), full(p), full(q), vec, vec],
        out_specs=[rowblk(half), rowblk(half), vec],
        out_shape=[jax.ShapeDtypeStruct((L, half), F32), jax.ShapeDtypeStruct((L, half), F32),
                   jax.ShapeDtypeStruct((1, half), F32)],
        compiler_params=_cparams(1),
        name="hyena_spectrum",
    )(cmat, smat, p, q, nrm, ans)


def _alt_sign(shape):
    return (1 - 2 * (lax.broadcasted_iota(jnp.int32, shape, 0) % 2)).astype(F32)


def _reverse_shift(x, j_ref):
    hi = x.astype(BF16)
    lo = (x - hi.astype(F32)).astype(BF16)
    j = j_ref[...]
    return (jnp.dot(j, hi, preferred_element_type=F32) + jnp.dot(j, lo, preferred_element_type=F32))


def _hy_conv_kernel(L, u_ref, cw_ref, cb_ref, ce_ref, se_ref, co_ref, so_ref, cot_ref, sot_ref, j_ref,
                    a_ref, bc_ref, an_ref, bias_ref, o_ref,
                    x1_s, x2_s, z_s, zs_s, zd_s, xe_s, ye_s, xo_s, yo_s, d_s, acc_s):
    nc = L // T
    H = L // 2
    fb = min(256, H)
    nfb = H // fb
    rb = min(1024, H)
    nrb = H // rb

    def conv(c, carry):
        s = pl.multiple_of(c * T, T)
        rows = pl.ds(s, T)
        prev, nxt = _halo_rows(u_ref, s, c, nc, L)
        y = _dwconv_chunk(u_ref[0, rows, :].astype(F32), prev, nxt, cw_ref, cb_ref)
        x1_s[rows, :] = y[:, 0:W_GRP]
        x2_s[rows, :] = y[:, W_GRP:2 * W_GRP]
        z_s[rows, :] = y[:, 2 * W_GRP:3 * W_GRP]
        return carry

    lax.fori_loop(0, nc, conv, 0)
    row0 = lax.broadcasted_iota(jnp.int32, (fb, W_GRP), 0) == 0

    for o, gate_s in enumerate((x1_s, x2_s)):
        cols = slice(W_GRP * o, W_GRP * (o + 1))
        acc_s[...] = jnp.zeros_like(acc_s)

        ws = [z_s[H + (nfb - 1 - b) * fb:H + (nfb - b) * fb, :] for b in range(nfb)]
        revs = [_reverse_shift(w, j_ref) for w in ws]
        qn = jnp.zeros((1, W_GRP), F32)
        for b in range(nfb):
            top = z_s[b * fb:(b + 1) * fb, :]
            edge = z_s[H + (nfb - b) * fb:H + (nfb - b) * fb + 1, :] if b > 0 else 0.0
            zr = jnp.where(row0, edge, revs[b])
            zs_s[b * fb:(b + 1) * fb, :] = (top + zr).astype(BF16)
            zd_s[b * fb:(b + 1) * fb, :] = (top - zr).astype(BF16)
            qn = qn + jnp.sum((top + ws[b]) * _alt_sign(top.shape), axis=0, keepdims=True)
        acc_s[0:1, :] = qn
        z_mid = z_s[H:H + 1, :]

        def fwd(j, carry, cols=cols, z_mid=z_mid):
            rows = pl.ds(pl.multiple_of(j * rb, rb), rb)
            odd_rows = pl.ds(pl.multiple_of(H + j * rb, rb), rb)
            zs = zs_s[...]
            zd = zd_s[...]
            mid = _alt_sign((rb, W_GRP)) * z_mid
            pe = jnp.dot(ce_ref[rows, :], zs, preferred_element_type=F32) + mid
            qo = jnp.dot(so_ref[rows, :], zs, preferred_element_type=F32) + mid
            po = jnp.dot(co_ref[rows, :], zd, preferred_element_type=F32)
            qe = jnp.dot(se_ref[rows, :], zd, preferred_element_type=F32)
            ae = a_ref[rows, cols]
            bce = bc_ref[rows, cols]
            ao = a_ref[odd_rows, cols]
            bco = bc_ref[odd_rows, cols]
            xe = pe * ae + qe * bce
            yo = qo * ao - po * bco
            xe_s[rows, :] = xe.astype(BF16)
            ye_s[rows, :] = (qe * ae - pe * bce).astype(BF16)
            xo_s[rows, :] = (po * ao + qo * bco).astype(BF16)
            yo_s[rows, :] = yo.astype(BF16)
            acc_s[1:2, :] += jnp.sum((xe + yo) * _alt_sign(xe.shape), axis=0, keepdims=True)
            return carry

        lax.fori_loop(0, nrb, fwd, 0)
        nyq = acc_s[0:1, :] * an_ref[:, cols]

        def finish(rows, y, cols=cols, gate_s=gate_s, nyq=nyq, o=o):
            res = gate_s[rows, :] * (y + _alt_sign(y.shape) * nyq + z_s[rows, :] * bias_ref[:, cols])
            if o == 0:
                z_s[rows, :] = res
            else:
                o_ref[0, rows, :] = res.astype(o_ref.dtype)

        def inv(j, carry, finish=finish):
            rows = pl.ds(pl.multiple_of(j * rb, rb), rb)
            u1 = (jnp.dot(ce_ref[rows, :], xe_s[...], preferred_element_type=F32)
                  + jnp.dot(sot_ref[rows, :], yo_s[...], preferred_element_type=F32))
            u2 = (jnp.dot(cot_ref[rows, :], xo_s[...], preferred_element_type=F32)
                  + jnp.dot(se_ref[rows, :], ye_s[...], preferred_element_type=F32))
            d_s[rows, :] = u1 - u2
            finish(rows, u1 + u2)
            return carry

        lax.fori_loop(0, nrb, inv, 0)
        y_mid = acc_s[1:2, :]

        drevs = [_reverse_shift(d_s[(nfb - 1 - b) * fb:(nfb - b) * fb, :], j_ref) for b in range(nfb)]
        for b in range(nfb):
            edge = d_s[(nfb - b) * fb:(nfb - b) * fb + 1, :] if b > 0 else y_mid
            finish(pl.ds(H + b * fb, fb), jnp.where(row0, edge, drevs[b]))


@functools.lru_cache(maxsize=None)
def _dft_half_tables(L):
    c, s = _dft_tables(L)
    H = L // 2
    fb = min(256, H)
    ce, co = c[0::2, :H], c[1::2, :H]
    se, so = s[0::2, :H], s[1::2, :H]
    j = np.zeros((fb, fb), np.float32)
    for r in range(1, fb):
        j[r, fb - r] = 1.0
    return tuple(np.ascontiguousarray(m) for m in (ce, se, co, so, co.T, so.T, j))


def _hy_conv_call(u, conv_w, conv_b, tables, a, bc, an, bias):
    bsz, L, _ = u.shape
    H = L // 2
    half = 2 * W_GRP
    mats = [jnp.asarray(m).astype(BF16) for m in tables]
    return pl.pallas_call(
        functools.partial(_hy_conv_kernel, L),
        grid=(bsz,),
        in_specs=[pl.BlockSpec((1, L, HY_COLS), lambda b: (b, 0, 0)),
                  pl.BlockSpec((3, HY_COLS), lambda b: (0, 0)),
                  pl.BlockSpec((1, HY_COLS), lambda b: (0, 0))]
                 + [_const_spec(m.shape) for m in mats]
                 + [_const_spec((L, half)), _const_spec((L, half)),
                    pl.BlockSpec((1, half), lambda b: (0, 0)),
                    pl.BlockSpec((1, half), lambda b: (0, 0))],
        out_specs=pl.BlockSpec((1, L, W_GRP), lambda b: (b, 0, 0)),
        out_shape=jax.ShapeDtypeStruct((bsz, L, W_GRP), BF16),
        scratch_shapes=[pltpu.VMEM((L, W_GRP), F32), pltpu.VMEM((L, W_GRP), F32),
                        pltpu.VMEM((L, W_GRP), F32),
                        pltpu.VMEM((H, W_GRP), BF16), pltpu.VMEM((H, W_GRP), BF16),
                        pltpu.VMEM((H, W_GRP), BF16), pltpu.VMEM((H, W_GRP), BF16),
                        pltpu.VMEM((H, W_GRP), BF16), pltpu.VMEM((H, W_GRP), BF16),
                        pltpu.VMEM((H, W_GRP), F32), pltpu.VMEM((8, W_GRP), F32)],
        compiler_params=_cparams(1),
        name="hyena_conv",
    )(u, conv_w, conv_b.reshape(1, -1), *mats, a, bc, an, bias.reshape(1, -1))


def _hyena_mixer(u, p):
    L = u.shape[1]
    cnp, snp = _dft_tables_split(L)
    cmat = jnp.asarray(cnp).astype(BF16)
    smat = jnp.asarray(snp).astype(BF16)
    a, bc, an = _hy_filter_call(L, p['hy_w1'], p['hy_b1'], p['hy_freq'], p['hy_w2'], p['hy_b2'],
                                p['hy_w3'], cmat, smat)
    return _hy_conv_call(u, p['hy_conv_w'], p['hy_conv_b'], _dft_half_tables(L), a, bc, an, p['hy_bias'])


def _out_ffn_kernel(final, h_ref, ssd_ref, hy_ref, ret_ref, s5_ref, gw_ref, gb_ref, mod_ref, g2_ref,
                    wo_ref, wup_ref, wdn_ref, fg_ref, o_ref, acc_s):
    s5 = jax.nn.gelu(s5_ref[...])
    s5 = s5 * jax.nn.sigmoid(jnp.dot(s5.astype(BF16), gw_ref[...], preferred_element_type=F32)
                             + gb_ref[...])
    y = jnp.zeros(h_ref.shape[1:], F32)
    for j, blk in enumerate((ssd_ref[0], hy_ref[0], ret_ref[0], s5)):
        y = y + jnp.dot(blk.astype(BF16), wo_ref[W_GRP * j:W_GRP * (j + 1), :],
                        preferred_element_type=F32)
    h1 = h_ref[0] + mod_ref[0, 2:3, :] * y
    xn = h1 * lax.rsqrt(jnp.mean(h1 * h1, axis=-1, keepdims=True) + EPS) * g2_ref[...]
    xm = (xn * (1.0 + mod_ref[0, 4:5, :]) + mod_ref[0, 3:4, :]).astype(BF16)
    fc = 256
    for j in range(D_FF // fc):
        gg = jnp.dot(xm, wup_ref[:, fc * j:fc * (j + 1)], preferred_element_type=F32)
        uu = jnp.dot(xm, wup_ref[:, D_FF + fc * j:D_FF + fc * (j + 1)], preferred_element_type=F32)
        part = jnp.dot((_silu(gg) * uu).astype(BF16), wdn_ref[fc * j:fc * (j + 1), :],
                       preferred_element_type=F32)
        if j == 0:
            acc_s[...] = part
        else:
            acc_s[...] += part
    h2 = h1 + mod_ref[0, 5:6, :] * acc_s[...]
    if final:
        h2 = h2 * lax.rsqrt(jnp.mean(h2 * h2, axis=-1, keepdims=True) + EPS) * fg_ref[...]
    o_ref[0] = h2


def _out_ffn_call(h, mix, glu, mods, ctx_stream, g2, wo, wup, wdn, final_g, final):
    bsz, L, _ = h.shape
    tm = min(512, L)
    mod_map = (lambda b, i: (bsz, 0, 0)) if ctx_stream else (lambda b, i: (b, 0, 0))
    tok = lambda w: pl.BlockSpec((1, tm, w), lambda b, i: (b, i, 0))
    glu_w, glu_b = glu
    return pl.pallas_call(
        functools.partial(_out_ffn_kernel, final),
        grid=(bsz, L // tm),
        in_specs=[tok(D_MODEL), tok(W_GRP), tok(W_GRP), tok(W_GRP),
                  pl.BlockSpec((tm, W_GRP), lambda b, i: (i, b)),
                  pl.BlockSpec((W_GRP, W_GRP), lambda b, i: (0, 0)),
                  pl.BlockSpec((1, W_GRP), lambda b, i: (0, 0)),
                  pl.BlockSpec((1, 6, D_MODEL), mod_map),
                  pl.BlockSpec((1, D_MODEL), lambda b, i: (0, 0)),
                  _const_spec((D_MODEL, D_MODEL)),
                  _const_spec((D_MODEL, 2 * D_FF)),
                  _const_spec((D_FF, D_MODEL)),
                  pl.BlockSpec((1, D_MODEL), lambda b, i: (0, 0))],
        out_specs=tok(D_MODEL),
        out_shape=jax.ShapeDtypeStruct((bsz, L, D_MODEL), F32),
        scratch_shapes=[pltpu.VMEM((tm, D_MODEL), F32)],
        compiler_params=_cparams(2),
        name="out_ffn",
    )(h, *mix, glu_w.astype(BF16), glu_b.reshape(1, -1), mods, g2, wo, wup, wdn, final_g)


def kernel(x, c, ctx, c_ctx, mod_w, mod_b, norm1_g, norm2_g, w_in, w_out, ssd_conv_w, ssd_conv_b, ssd_a_log, ssd_dt_bias, ssd_d, ssd_norm_g, hy_conv_w, hy_conv_b, hy_w1, hy_b1, hy_freq, hy_w2, hy_b2, hy_w3, hy_bias, ret_decay, s5_a_re, s5_a_im, s5_log_dt, s5_b_re, s5_b_im, s5_c_re, s5_c_im, s5_d, s5_glu_w, s5_glu_b, ffn_w_up, ffn_w_down, final_norm_g):
    bsz = x.shape[0]
    depth = mod_w.shape[0]
    sc = jnp.concatenate([c, c_ctx[None, :], jnp.zeros((MOD_ROWS - bsz - 1, D_MODEL), F32)], axis=0)
    mods_all = _mod_call(sc, mod_w, mod_b).reshape(depth, MOD_ROWS, 6, D_MODEL)
    fg = final_norm_g.reshape(1, -1)
    h_l, h_c = x, ctx
    o_xbc, o_dt, o_hy = W_GRP, W_GRP + SSD_XBC, W_GRP + SSD_XBC + 2 * N_HEADS
    o_ret = o_hy + HY_COLS
    o_s5 = o_ret + RET_COLS
    for i in range(depth):
        last = i == depth - 1
        wi = w_in[i]
        wdt = wi[:, o_dt:o_hy]
        wcat = jnp.concatenate([wi[:, 0:o_dt], wi[:, o_hy:o_s5 + W_GRP], wdt,
                                jnp.zeros((D_MODEL, 128 - 2 * N_HEADS), F32)], axis=1).astype(BF16)
        mods = mods_all[i]
        g1 = norm1_g[i].reshape(1, -1)
        zl, xbcl, hyl, retl, dtl, s5l = _inproj_call(h_l, g1, mods, False, wcat)
        zc, xbcc, hyc, retc, dtc, s5c = _inproj_call(h_c, g1, mods, True, wcat)
        ssd_out = _ssd_call((zc, xbcc, dtc), (zl, xbcl, dtl), ssd_conv_w[i], ssd_conv_b[i],
                            ssd_a_log[i], ssd_dt_bias[i], ssd_d[i], ssd_norm_g[i], not last)
        ret_out = _ret_call(retc, retl, ret_decay[i], not last)
        ssd_c, ssd_l = ssd_out if not last else (None, ssd_out[0])
        ret_c, ret_l = ret_out if not last else (None, ret_out[0])
        p = dict(s5_a_re=s5_a_re[i], s5_a_im=s5_a_im[i], s5_log_dt=s5_log_dt[i], s5_b_re=s5_b_re[i],
                 s5_b_im=s5_b_im[i], s5_c_re=s5_c_re[i], s5_c_im=s5_c_im[i], s5_d=s5_d[i],
                 s5_glu_w=s5_glu_w[i], s5_glu_b=s5_glu_b[i],
                 hy_conv_w=hy_conv_w[i], hy_conv_b=hy_conv_b[i], hy_w1=hy_w1[i], hy_b1=hy_b1[i],
                 hy_freq=hy_freq[i], hy_w2=hy_w2[i], hy_b2=hy_b2[i], hy_w3=hy_w3[i], hy_bias=hy_bias[i])
        s5_c, s5_l = _s5_mixer(s5c, s5l, bsz, p)
        hy_l = _hyena_mixer(hyl, p)
        g2 = norm2_g[i].reshape(1, -1)
        wo = w_out[i].astype(BF16)
        wup = ffn_w_up[i].astype(BF16)
        wdn = ffn_w_down[i].astype(BF16)
        glu = (s5_glu_w[i], s5_glu_b[i])
        h_l = _out_ffn_call(h_l, (ssd_l, hy_l, ret_l, s5_l), glu, mods, False, g2, wo, wup, wdn, fg, last)
        if not last:
            hy_c = _hyena_mixer(hyc, p)
            h_c = _out_ffn_call(h_c, (ssd_c, hy_c, ret_c, s5_c), glu, mods, True, g2, wo, wup, wdn, fg, False)
    return h_l
```

```python
import functools
import math

import numpy as np
import jax
import jax.numpy as jnp
from jax import lax
from jax.experimental import pallas as pl
from jax.experimental.pallas import tpu as pltpu

F32 = jnp.float32
BF16 = jnp.bfloat16
EPS = 1e-6

D_MODEL = 1024
W_GRP = 256
T = 128
N_HEADS = 4
HEAD_DIM = 64
SSD_STATE = 64
SSD_XBC = W_GRP + 2 * SSD_STATE
HY_COLS = 3 * W_GRP
RET_COLS = 4 * W_GRP
GRID_W = 64
ROPE_BASE = 10000.0
HY_EMB = 33
HY_BANDS = 16
HY_FILT = 64
S5_GROUPS = 16
S5_CH = 16
S5_STATE = 64
S5_NS = S5_GROUPS * S5_STATE
D_FF = 2816
P3_GROUP = 4
RET_GROUP = 8
S5_TS = 64
MOD_ROWS = 24

VMEM_LIMIT = 56 * 1024 * 1024


def _cparams(n_grid):
    return pltpu.CompilerParams(dimension_semantics=("arbitrary",) * n_grid,
                                vmem_limit_bytes=VMEM_LIMIT)


def _dot(a, b):
    return jnp.dot(a.astype(BF16), b.astype(BF16), preferred_element_type=F32)


def _dot_nt(a, b):
    return lax.dot_general(a.astype(BF16), b.astype(BF16), (((1,), (1,)), ((), ())),
                           preferred_element_type=F32)


def _dot_tn(a, b):
    return lax.dot_general(a.astype(BF16), b.astype(BF16), (((0,), (0,)), ((), ())),
                           preferred_element_type=F32)


def _dot_f32(a, b):
    return jnp.dot(a, b, preferred_element_type=F32, precision=lax.Precision.HIGHEST)


def _silu(x):
    return x * jax.nn.sigmoid(x)


def _const_spec(shape):
    nd = len(shape)
    return pl.BlockSpec(shape, lambda *_: (0,) * nd, pipeline_mode=pl.Buffered(1))


def _mod_kernel(sc_ref, w_ref, b_ref, o_ref):
    s = _silu(sc_ref[...])
    o_ref[0] = _dot_f32(s, w_ref[0]) + b_ref[0]


def _mod_call(sc, mod_w, mod_b):
    depth, _, n = mod_w.shape
    tn = 1536
    return pl.pallas_call(
        _mod_kernel,
        grid=(depth, n // tn),
        in_specs=[pl.BlockSpec((MOD_ROWS, D_MODEL), lambda l, j: (0, 0)),
                  pl.BlockSpec((1, D_MODEL, tn), lambda l, j: (l, 0, j)),
                  pl.BlockSpec((1, 1, tn), lambda l, j: (l, 0, j))],
        out_specs=pl.BlockSpec((1, MOD_ROWS, tn), lambda l, j: (l, 0, j)),
        out_shape=jax.ShapeDtypeStruct((depth, MOD_ROWS, n), F32),
        compiler_params=_cparams(2),
        name="adaln_mod",
    )(sc, mod_w, mod_b.reshape(depth, 1, n))


def _fold_batches(L, rows):
    return max(1, rows // L)


def _inproj_kernel(nb, x_ref, g_ref, mod_ref, w_ref, z_ref, xbc_ref, hy_ref, ret_ref, dt_ref, s5_ref):
    tm = x_ref.shape[1]
    x = x_ref[...].reshape(nb * tm, D_MODEL)
    xn = x * lax.rsqrt(jnp.mean(x * x, axis=-1, keepdims=True) + EPS) * g_ref[...]
    xm = (xn * (1.0 + mod_ref[0, 1:2, :]) + mod_ref[0, 0:1, :]).astype(BF16)
    o = 0
    for ref, width in ((z_ref, W_GRP), (xbc_ref, SSD_XBC), (hy_ref, HY_COLS),
                       (ret_ref, RET_COLS)):
        ref[...] = jnp.dot(xm, w_ref[:, o:o + width],
                           preferred_element_type=F32).astype(BF16).reshape(nb, tm, width)
        o += width
    s5 = jnp.dot(xm, w_ref[:, o:o + W_GRP], preferred_element_type=F32).astype(BF16)
    s5_ref[...] = jnp.concatenate([s5[j * tm:(j + 1) * tm, :] for j in range(nb)], axis=1)
    o += W_GRP
    dt_ref[...] = jnp.dot(xm, w_ref[:, o:o + 128], preferred_element_type=F32).reshape(nb, tm, 128)


def _inproj_call(h, g, mods, ctx_stream, wcat):
    bsz, L, _ = h.shape
    tm = min(1024, L)
    nb = _fold_batches(L, 1024) if ctx_stream else 1
    ncols = wcat.shape[1]
    mod_map = (lambda b, i: (bsz, 0, 0)) if ctx_stream else (lambda b, i: (b, 0, 0))
    tok = lambda w: pl.BlockSpec((nb, tm, w), lambda b, i: (b, i, 0))
    out_shape = [jax.ShapeDtypeStruct((bsz, L, W_GRP), BF16),
                 jax.ShapeDtypeStruct((bsz, L, SSD_XBC), BF16),
                 jax.ShapeDtypeStruct((bsz, L, HY_COLS), BF16),
                 jax.ShapeDtypeStruct((bsz, L, RET_COLS), BF16),
                 jax.ShapeDtypeStruct((bsz, L, 128), F32),
                 jax.ShapeDtypeStruct((L, bsz * W_GRP), BF16)]
    out_specs = [tok(W_GRP), tok(SSD_XBC), tok(HY_COLS), tok(RET_COLS), tok(128),
                 pl.BlockSpec((tm, nb * W_GRP), lambda b, i: (i, b))]
    return pl.pallas_call(
        functools.partial(_inproj_kernel, nb),
        grid=(bsz // nb, L // tm),
        in_specs=[tok(D_MODEL),
                  pl.BlockSpec((1, D_MODEL), lambda b, i: (0, 0)),
                  pl.BlockSpec((1, 6, D_MODEL), mod_map),
                  _const_spec((D_MODEL, ncols))],
        out_specs=out_specs,
        out_shape=out_shape,
        compiler_params=_cparams(2),
        name="in_proj",
    )(h, g, mods, wcat)


def _halo_rows(ref, s, c, nc, L):
    sp = pl.multiple_of(jnp.maximum(s - 16, 0), 16)
    prev = ref[0, pl.ds(sp, 16), :][15:16, :].astype(F32)
    prev = jnp.where(c > 0, prev, 0.0)
    sn = pl.multiple_of(jnp.minimum(s + T, L - 16), 16)
    nxt = ref[0, pl.ds(sn, 16), :][0:1, :].astype(F32)
    nxt = jnp.where(c < nc - 1, nxt, 0.0)
    return prev, nxt


def _dwconv_chunk(x, prev, nxt, w_ref, b_ref):
    n = x.shape[0]
    row = lax.broadcasted_iota(jnp.int32, x.shape, 0)
    up = jnp.where(row == 0, prev, pltpu.roll(x, 1, 0))
    dn = jnp.where(row == n - 1, nxt, pltpu.roll(x, n - 1, 0))
    return up * w_ref[0:1, :] + x * w_ref[1:2, :] + dn * w_ref[2:3, :] + b_ref[...]


def _expand_heads(c, exp_ref):
    hi = c.astype(BF16)
    lo = (c - hi.astype(F32)).astype(BF16)
    return jnp.dot(jnp.concatenate([hi, lo], axis=1), exp_ref[...], preferred_element_type=F32)


def _state_recurrence(hf_ref, hb_ref, decf_ref, decb_ref, nc, h0f, h0b):
    hf_ref[0] = h0f
    hb_ref[nc] = h0b

    def fwd(c, carry):
        hf_ref[c + 1] = decf_ref[c] * hf_ref[c] + hf_ref[c + 1]
        return carry

    lax.fori_loop(0, nc, fwd, 0)

    def bwd(k, carry):
        c = nc - 1 - k
        hb_ref[c] = decb_ref[c] * hb_ref[c + 1] + hb_ref[c]
        return carry

    lax.fori_loop(0, nc, bwd, 0)
    return hf_ref[nc], hb_ref[0]


def _split3(x):
    hi = x.astype(BF16)
    r = x - hi.astype(F32)
    mid = r.astype(BF16)
    lo = (r - mid.astype(F32)).astype(BF16)
    return jnp.concatenate([hi, mid, lo], axis=1)


def _tile_heads_bd(x, bdmask):
    xb = x.astype(BF16)
    return jnp.where(bdmask, jnp.concatenate([xb] * N_HEADS, axis=0), 0)


def _rows_to_lanes(a, lo):
    return jnp.concatenate([a[lo + h:lo + h + 1, :] for h in range(N_HEADS)], axis=1)


def _ssd_sequence(L, z_ref, xbc_ref, dt_ref, y_ref, prm, scr, h0f, h0b):
    (cw_ref, cb_ref, alog_row, bias_row, dskip_ref, ng_ref, exp_ref, exp128_ref, sel_ref, tri_ref) = prm
    (xs_s, bc_s, ee_s, acs_s, dt_s, hf_s, hb_s, decf_s, decb_s) = scr
    nc = L // T
    lane = lax.broadcasted_iota(jnp.int32, (T, 128), 1)
    a_row = -jnp.exp(alog_row[...])

    grp = min(P3_GROUP, nc)

    def phase1(p, carry):
        cs = [p * grp + j for j in range(grp)]
        rows = [pl.ds(pl.multiple_of(c * T, T), T) for c in cs]
        pre = []
        for r in rows:
            dt = jax.nn.softplus(dt_ref[0, r, :] + bias_row[...])
            dt_s[r, :] = dt
            la = dt * a_row
            hi = la.astype(BF16)
            r1 = la - hi.astype(F32)
            mid = r1.astype(BF16)
            lo = (r1 - mid.astype(F32)).astype(BF16)
            acs_f = jnp.dot(tri_ref[...], jnp.concatenate([hi, mid, lo], axis=0),
                            preferred_element_type=F32)
            pre.append((dt, la, acs_f))
        st = []
        for c, r, (dt, la, acs_f) in zip(cs, rows, pre):
            prev, nxt = _halo_rows(xbc_ref, pl.multiple_of(c * T, T), c, nc, L)
            xact = _silu(_dwconv_chunk(xbc_ref[0, r, :].astype(F32), prev, nxt, cw_ref, cb_ref))
            xs_s[r, :] = xact[:, 0:W_GRP]
            bc_s[r, :] = xact[:, W_GRP:SSD_XBC]
            tot = acs_f[T - 1:T, :]
            acs = jnp.where(lane < N_HEADS, acs_f, tot - acs_f + la)
            acs_s[r, :] = acs
            st.append((xact, _expand_heads(jnp.exp(acs), exp_ref),
                       _expand_heads(dt * jnp.exp(tot - acs), exp_ref)))
        for c, r, (xact, ee, wx) in zip(cs, rows, st):
            xs = xact[:, 0:W_GRP]
            bm = xact[:, W_GRP:W_GRP + SSD_STATE]
            ee_s[r, :] = ee
            hf_s[c + 1] = _dot_tn(bm, xs * wx[:, 0:W_GRP])
            hb_s[c] = _dot_tn(bm, xs * wx[:, W_GRP:2 * W_GRP])
            decf_s[c] = ee[T - 1:T, 0:W_GRP]
            decb_s[c] = ee[0:1, W_GRP:2 * W_GRP]
        return carry

    lax.fori_loop(0, nc // grp, phase1, 0)
    hf_fin, hb_fin = _state_recurrence(hf_s, hb_s, decf_s, decb_s, nc, h0f, h0b)
    if y_ref is None:
        return hf_fin, hb_fin

    ri = lax.broadcasted_iota(jnp.int32, (T, N_HEADS * T), 0)
    ci = lax.broadcasted_iota(jnp.int32, (T, N_HEADS * T), 1) % T
    strict_lower = ci < ri
    diag = ci == ri
    r4 = lax.broadcasted_iota(jnp.int32, (N_HEADS * T, W_GRP), 0) // T
    c4 = lax.broadcasted_iota(jnp.int32, (N_HEADS * T, W_GRP), 1) // HEAD_DIM
    bdmask = r4 == c4
    nl = N_HEADS * T

    nt = (((1,), (1,)), ((), ()))

    def phase3(p, carry):
        cs = [p * grp + j for j in range(grp)]
        rows = [pl.ds(pl.multiple_of(c * T, T), T) for c in cs]
        st = []
        for c, r in zip(cs, rows):
            bc = bc_s[r, :]
            bm = bc[:, 0:SSD_STATE]
            cm = bc[:, SSD_STATE:2 * SSD_STATE]
            a3 = _split3(acs_s[r, :])
            col = jnp.dot(a3, exp128_ref[...], preferred_element_type=F32)
            acst = lax.dot_general(sel_ref[...], a3, nt, preferred_element_type=F32)
            dtt = lax.dot_general(sel_ref[...], _split3(dt_s[r, :]), nt,
                                  preferred_element_type=F32)
            inter = (_dot(cm, hf_s[c]), _dot(cm, hb_s[c + 1]))
            st.append((col, acst, dtt, _dot_nt(cm, bm), inter))
        ys = []
        for r, (col, acst, dtt, g, inter) in zip(rows, st):
            g4 = jnp.concatenate([g] * N_HEADS, axis=1)
            shifted = acst - jnp.log(dtt)
            arg = jnp.where(strict_lower, col[:, 0:nl] - _rows_to_lanes(shifted, 0),
                            col[:, nl:2 * nl] - _rows_to_lanes(shifted, N_HEADS))
            w = g4 * (jnp.exp(arg) + jnp.where(diag, _rows_to_lanes(dtt, 0), 0.0))
            ys.append(jnp.dot(w.astype(BF16), _tile_heads_bd(xs_s[r, :], bdmask),
                              preferred_element_type=F32))
        for r, y, (_, _, _, _, inter) in zip(rows, ys, st):
            ee = ee_s[r, :]
            xs = xs_s[r, :]
            y = y + ee[:, 0:W_GRP] * inter[0] + ee[:, W_GRP:2 * W_GRP] * inter[1]
            y = y + xs * dskip_ref[...]
            y = y * _silu(z_ref[0, r, :].astype(F32))
            y = y * lax.rsqrt(jnp.mean(y * y, axis=-1, keepdims=True) + EPS) * ng_ref[...]
            y_ref[0, r, :] = y.astype(y_ref.dtype)
        return carry

    lax.fori_loop(0, nc // grp, phase3, 0)
    return hf_fin, hb_fin


def _ssd_kernel(Lc, L, ctx_out, zc_ref, xbcc_ref, dtc_ref, zl_ref, xbcl_ref, dtl_ref,
                cw_ref, cb_ref, alog_row, bias_row, dskip_ref, ng_ref, exp_ref, exp128_ref, sel_ref,
                tri_ref, *rest):
    yc_ref, yl_ref, scr = (rest[0], rest[1], rest[2:]) if ctx_out else (None, rest[0], rest[1:])
    prm = (cw_ref, cb_ref, alog_row, bias_row, dskip_ref, ng_ref, exp_ref, exp128_ref, sel_ref, tri_ref)
    zero = jnp.zeros((SSD_STATE, W_GRP), F32)
    hf, hb = _ssd_sequence(Lc, zc_ref, xbcc_ref, dtc_ref, yc_ref, prm, scr, zero, zero)
    _ssd_sequence(L, zl_ref, xbcl_ref, dtl_ref, yl_ref, prm, scr, hf, hb)


@functools.lru_cache(maxsize=None)
def _ssd_tables():
    exp64 = np.zeros((256, 512), np.float32)
    exp128 = np.zeros((384, 2 * N_HEADS * T), np.float32)
    sel = np.zeros((8, 384), np.float32)
    for r in range(2):
        for h in range(N_HEADS):
            m = r * N_HEADS + h
            for part in range(2):
                exp64[128 * part + m, r * 256 + 64 * h:r * 256 + 64 * (h + 1)] = 1.0
            for part in range(3):
                exp128[128 * part + m, (r * N_HEADS + h) * T:(r * N_HEADS + h + 1) * T] = 1.0
                sel[m, 128 * part + m] = 1.0
    tri = np.tile(np.tril(np.ones((T, T), np.float32)), (1, 3))
    return exp64, exp128, sel, tri


def _pad_row(v, n=128):
    v = v.reshape(1, -1)
    return jnp.pad(v, ((0, 0), (0, n - v.shape[1])))


def _ssd_call(uc, ul, conv_w, conv_b, a_log, dt_bias, d_skip, norm_g, ctx_out):
    zc, xbcc, dtc = uc
    zl, xbcl, dtl = ul
    bsz, Lc, _ = zc.shape
    L = zl.shape[1]
    nc = L // T
    alog_row = _pad_row(a_log)
    bias_row = _pad_row(dt_bias)
    dskip = jnp.repeat(d_skip, HEAD_DIM).reshape(1, W_GRP)
    exp64, exp128, sel, tri = (jnp.asarray(t).astype(BF16) for t in _ssd_tables())

    def seq(Lx, w):
        return pl.BlockSpec((1, Lx, w), lambda b: (b, 0, 0))

    def small(shape):
        return pl.BlockSpec(shape, lambda b: (0,) * len(shape))

    in_specs = [seq(Lc, W_GRP), seq(Lc, SSD_XBC), seq(Lc, 128),
                seq(L, W_GRP), seq(L, SSD_XBC), seq(L, 128),
                small((3, SSD_XBC)), small((1, SSD_XBC)), small((1, 128)),
                small((1, 128)), small((1, W_GRP)), small((1, W_GRP)),
                small(exp64.shape), small(exp128.shape), small(sel.shape), small(tri.shape)]
    scratch = [pltpu.VMEM((L, W_GRP), F32), pltpu.VMEM((L, 128), F32), pltpu.VMEM((L, 512), F32),
               pltpu.VMEM((L, 128), F32), pltpu.VMEM((L, 128), F32),
               pltpu.VMEM((nc + 1, SSD_STATE, W_GRP), F32), pltpu.VMEM((nc + 1, SSD_STATE, W_GRP), F32),
               pltpu.VMEM((nc, 1, W_GRP), F32), pltpu.VMEM((nc, 1, W_GRP), F32)]
    return pl.pallas_call(
        functools.partial(_ssd_kernel, Lc, L, ctx_out),
        grid=(bsz,),
        in_specs=in_specs,
        out_specs=([seq(Lc, W_GRP)] if ctx_out else []) + [seq(L, W_GRP)],
        out_shape=([jax.ShapeDtypeStruct((bsz, Lc, W_GRP), BF16)] if ctx_out else [])
                  + [jax.ShapeDtypeStruct((bsz, L, W_GRP), BF16)],
        scratch_shapes=scratch,
        compiler_params=_cparams(1),
        name="ssd_mixer",
    )(zc, xbcc, dtc, zl, xbcl, dtl, conv_w, conv_b.reshape(1, -1),
      alog_row, bias_row, dskip, norm_g.reshape(1, -1), exp64, exp128, sel, tri)


def _ret_sequence(L, rope, q_ref, k_ref, v_ref, g_ref, y_ref, cos_ref, sin_ref, perm_ref, avg_ref,
                  tabs, scr, h0f, h0b):
    (ef, eb, wf, wb, decf, decb, bdmask) = tabs
    (qr_s, kr_s, dm_s, hf_s, hb_s) = scr
    nc = L // T
    scale = HEAD_DIM ** -0.5
    r4 = lax.broadcasted_iota(jnp.int32, (N_HEADS * T, W_GRP), 0) // T
    c4 = lax.broadcasted_iota(jnp.int32, (N_HEADS * T, W_GRP), 1) // HEAD_DIM
    stackmask = r4 == c4

    def rot(x, rows):
        partner = jnp.dot(x, perm_ref[...], preferred_element_type=F32)
        return x.astype(F32) * cos_ref[rows, :] + partner * sin_ref[rows, :]

    grp = min(RET_GROUP, nc)

    def phase1(p, carry):
        cs = [p * grp + j for j in range(grp)]
        rows = [pl.ds(pl.multiple_of(c * T, T), T) for c in cs]
        qk = [(q_ref[0, r, :], k_ref[0, r, :]) for r in rows]
        if rope:
            qk = [(rot(q, r), rot(k, r)) for (q, k), r in zip(qk, rows)]
        for c, r, (q, k) in zip(cs, rows, qk):
            k = k.astype(F32) * scale
            qr_s[r, :] = q.astype(F32)
            kr_s[r, :] = k
            v = v_ref[0, r, :]
            hf_s[c + 1] = _dot_tn(k * wf, v) * bdmask
            hb_s[c] = _dot_tn(k * wb, v) * bdmask
        return carry

    lax.fori_loop(0, nc // grp, phase1, 0)

    hf_s[0] = h0f
    hb_s[nc] = h0b

    def fwd(c, carry):
        hf_s[c + 1] = decf * hf_s[c] + hf_s[c + 1]
        return carry

    lax.fori_loop(0, nc, fwd, 0)

    def bwd(kk, carry):
        c = nc - 1 - kk
        hb_s[c] = decb * hb_s[c + 1] + hb_s[c]
        return carry

    lax.fori_loop(0, nc, bwd, 0)
    if y_ref is None:
        return hf_s[nc], hb_s[0]

    def gmean(x):
        hi = x.astype(BF16)
        lo = (x - hi.astype(F32)).astype(BF16)
        return jnp.dot(jnp.concatenate([hi, lo], axis=1), avg_ref[...], preferred_element_type=F32)

    def gmean_sq(x):
        return jnp.dot((x * x).astype(BF16), avg_ref[0:W_GRP, :], preferred_element_type=F32)

    def phase3(p, carry):
        cs = [p * grp + j for j in range(grp)]
        rows = [pl.ds(pl.multiple_of(c * T, T), T) for c in cs]
        qs = [qr_s[r, :] for r in rows]
        scs = [lax.dot_general(q.astype(BF16), _tile_heads_bd(kr_s[r, :], stackmask),
                               (((1,), (1,)), ((), ())), preferred_element_type=F32)
               for q, r in zip(qs, rows)]
        inters = [_dot(q * ef, hf_s[c]) + _dot(q * eb, hb_s[c + 1]) for q, c in zip(qs, cs)]
        ys = [jnp.dot((sc * dm_s[...]).astype(BF16), _tile_heads_bd(v_ref[0, r, :], stackmask),
                      preferred_element_type=F32) + it
              for sc, r, it in zip(scs, rows, inters)]
        ycs = [y - gmean(y) for y in ys]
        vars_ = [gmean_sq(yc) for yc in ycs]
        for r, yc, var in zip(rows, ycs, vars_):
            y_ref[0, r, :] = (_silu(g_ref[0, r, :].astype(F32))
                              * (yc * lax.rsqrt(var + EPS))).astype(y_ref.dtype)
        return carry

    lax.fori_loop(0, nc // grp, phase3, 0)
    return hf_s[nc], hb_s[0]


def _ret_kernel(Lc, L, ctx_out, qc_ref, kc_ref, vc_ref, gc_ref, ql_ref, kl_ref, vl_ref, gl_ref,
                cos_ref, sin_ref, dec_ref, perm_ref, avg_ref, *rest):
    yc_ref, rest = (rest[0], rest[1:]) if ctx_out else (None, rest)
    yl_ref, qr_s, kr_s, dm_s, hf_s, hb_s = rest
    lg = -jnp.exp(dec_ref[...])
    lgf = lg[0:1, :]
    lgb = lg[1:2, :]
    i = lax.broadcasted_iota(jnp.int32, (T, W_GRP), 0).astype(F32)
    ef = jnp.exp(lgf * (i + 1.0))
    eb = jnp.exp(lgb * (T - i))
    wf = jnp.exp(lgf * (T - 1.0 - i))
    wb = jnp.exp(lgb * i)
    decf = jnp.exp(lgf * float(T))
    decb = jnp.exp(lgb * float(T))
    r2 = lax.broadcasted_iota(jnp.int32, (W_GRP, W_GRP), 0) // HEAD_DIM
    c2 = lax.broadcasted_iota(jnp.int32, (W_GRP, W_GRP), 1) // HEAD_DIM
    bdmask = (r2 == c2).astype(F32)
    ri = lax.broadcasted_iota(jnp.int32, (T, T), 0)
    ci = lax.broadcasted_iota(jnp.int32, (T, T), 1)
    d = (ri - ci).astype(F32)
    for h in range(N_HEADS):
        lf = lgf[:, HEAD_DIM * h:HEAD_DIM * h + 1]
        lb = lgb[:, HEAD_DIM * h:HEAD_DIM * h + 1]
        dm_s[:, T * h:T * (h + 1)] = (jnp.exp(jnp.where(ci <= ri, lf * d, -jnp.inf))
                                      + jnp.exp(jnp.where(ci >= ri, -lb * d, -jnp.inf)))
    tabs = (ef, eb, wf, wb, decf, decb, bdmask)
    scr = (qr_s, kr_s, dm_s, hf_s, hb_s)
    zero = jnp.zeros((W_GRP, W_GRP), F32)
    hf, hb = _ret_sequence(Lc, False, qc_ref, kc_ref, vc_ref, gc_ref, yc_ref, cos_ref, sin_ref,
                           perm_ref, avg_ref, tabs, scr, zero, zero)
    _ret_sequence(L, True, ql_ref, kl_ref, vl_ref, gl_ref, yl_ref, cos_ref, sin_ref,
                  perm_ref, avg_ref, tabs, scr, hf, hb)


@functools.lru_cache(maxsize=None)
def _rope_tables(L):
    t = np.arange(L)
    f = 16
    inv = (ROPE_BASE ** (-np.arange(f, dtype=np.float32) / f)).astype(np.float32)
    cos = np.zeros((L, HEAD_DIM), np.float32)
    sin = np.zeros((L, HEAD_DIM), np.float32)
    for base, pos in ((0, t // GRID_W), (32, t % GRID_W)):
        ang = pos.astype(np.float32)[:, None] * inv[None, :]
        ang = ang.astype(np.float32).astype(np.float64)
        cos[:, base:base + f] = np.cos(ang)
        cos[:, base + f:base + 2 * f] = np.cos(ang)
        sin[:, base:base + f] = -np.sin(ang)
        sin[:, base + f:base + 2 * f] = np.sin(ang)
    return np.tile(cos, (1, N_HEADS)), np.tile(sin, (1, N_HEADS))


@functools.lru_cache(maxsize=None)
def _ret_tables():
    perm = np.zeros((W_GRP, W_GRP), np.float32)
    avg = np.zeros((2 * W_GRP, W_GRP), np.float32)
    for l in range(W_GRP):
        src = l + 16 if (l % 32) < 16 else l - 16
        perm[src, l] = 1.0
        g = l // HEAD_DIM
        for part in range(2):
            avg[part * W_GRP + g * HEAD_DIM:part * W_GRP + (g + 1) * HEAD_DIM, l] = 1.0 / HEAD_DIM
    return perm, avg


def _ret_call(uc, ul, decay_param, ctx_out):
    bsz, Lc, _ = uc.shape
    L = ul.shape[1]
    nc = L // T
    cos, sin = _rope_tables(L)
    perm, avg = (jnp.asarray(t).astype(BF16) for t in _ret_tables())
    dec = jnp.repeat(decay_param, HEAD_DIM, axis=1)

    def col(Lx, j):
        return pl.BlockSpec((1, Lx, W_GRP), lambda b, j=j: (b, 0, j))

    def seq(Lx):
        return pl.BlockSpec((1, Lx, W_GRP), lambda b: (b, 0, 0))

    in_specs = ([col(Lc, j) for j in range(4)] + [col(L, j) for j in range(4)]
                + [pl.BlockSpec((L, W_GRP), lambda b: (0, 0)), pl.BlockSpec((L, W_GRP), lambda b: (0, 0)),
                   pl.BlockSpec((2, W_GRP), lambda b: (0, 0)),
                   pl.BlockSpec((W_GRP, W_GRP), lambda b: (0, 0)),
                   pl.BlockSpec((2 * W_GRP, W_GRP), lambda b: (0, 0))])
    scratch = [pltpu.VMEM((L, W_GRP), F32), pltpu.VMEM((L, W_GRP), F32),
               pltpu.VMEM((T, N_HEADS * T), F32),
               pltpu.VMEM((nc + 1, W_GRP, W_GRP), F32), pltpu.VMEM((nc + 1, W_GRP, W_GRP), F32)]
    return pl.pallas_call(
        functools.partial(_ret_kernel, Lc, L, ctx_out),
        grid=(bsz,),
        in_specs=in_specs,
        out_specs=([seq(Lc)] if ctx_out else []) + [seq(L)],
        out_shape=([jax.ShapeDtypeStruct((bsz, Lc, W_GRP), BF16)] if ctx_out else [])
                  + [jax.ShapeDtypeStruct((bsz, L, W_GRP), BF16)],
        scratch_shapes=scratch,
        compiler_params=_cparams(1),
        name="retention_mixer",
    )(uc, uc, uc, uc, ul, ul, ul, ul, jnp.asarray(cos), jnp.asarray(sin), dec, perm, avg)


def _s5_prep_kernel(are_ref, aim_ref, ldt_ref, bre_ref, bim_ref, cre_ref, cim_ref,
                    bmat_ref, cmat_ref, ab_ref):
    a_re = are_ref[0]
    a_im = aim_ref[0]
    dt = jnp.exp(ldt_ref[0])
    mag = jnp.exp(a_re * dt)
    ab_re = mag * jnp.cos(a_im * dt)
    ab_im = mag * jnp.sin(a_im * dt)
    den = a_re * a_re + a_im * a_im
    z_re = ((ab_re - 1.0) * a_re + ab_im * a_im) / den
    z_im = (ab_im * a_re - (ab_re - 1.0) * a_im) / den
    b_re = bre_ref[...]
    b_im = bim_ref[...]
    bmat_ref[0, :, 0:S5_NS] = (b_re * z_re - b_im * z_im).astype(BF16)
    bmat_ref[0, :, S5_NS:2 * S5_NS] = (b_re * z_im + b_im * z_re).astype(BF16)
    cmat_ref[0, 0:S5_NS, :] = cre_ref[0].astype(BF16)
    cmat_ref[0, S5_NS:2 * S5_NS, :] = (-cim_ref[0]).astype(BF16)
    ab_ref[0, :, 0:S5_NS] = ab_re
    ab_ref[0, :, S5_NS:2 * S5_NS] = ab_im


def _s5_prep_call(a_re, a_im, log_dt, b_re, b_im, c_re, c_im):
    eye = jnp.eye(S5_GROUPS, dtype=F32)
    b_re_bd = jnp.einsum('gpc,gh->gchp', b_re, eye).reshape(W_GRP, S5_NS)
    b_im_bd = jnp.einsum('gpc,gh->gchp', b_im, eye).reshape(W_GRP, S5_NS)
    c_re_bd = jnp.einsum('rgcp,gh->rgphc', c_re, eye).reshape(2, S5_NS, W_GRP)
    c_im_bd = jnp.einsum('rgcp,gh->rgphc', c_im, eye).reshape(2, S5_NS, W_GRP)
    ldt = jnp.repeat(log_dt, S5_STATE, axis=1).reshape(2, 1, S5_NS)
    row = lambda: pl.BlockSpec((1, 1, S5_NS), lambda r: (r, 0, 0))
    return pl.pallas_call(
        _s5_prep_kernel,
        grid=(2,),
        in_specs=[row(), row(), row(),
                  pl.BlockSpec((W_GRP, S5_NS), lambda r: (0, 0)), pl.BlockSpec((W_GRP, S5_NS), lambda r: (0, 0)),
                  pl.BlockSpec((1, S5_NS, W_GRP), lambda r: (r, 0, 0)),
                  pl.BlockSpec((1, S5_NS, W_GRP), lambda r: (r, 0, 0))],
        out_specs=[pl.BlockSpec((1, W_GRP, 2 * S5_NS), lambda r: (r, 0, 0)),
                   pl.BlockSpec((1, 2 * S5_NS, W_GRP), lambda r: (r, 0, 0)),
                   pl.BlockSpec((1, 1, 2 * S5_NS), lambda r: (r, 0, 0))],
        out_shape=[jax.ShapeDtypeStruct((2, W_GRP, 2 * S5_NS), BF16),
                   jax.ShapeDtypeStruct((2, 2 * S5_NS, W_GRP), BF16),
                   jax.ShapeDtypeStruct((2, 1, 2 * S5_NS), F32)],
        compiler_params=_cparams(1),
        name="s5_prep",
    )(a_re.reshape(2, 1, S5_NS), a_im.reshape(2, 1, S5_NS), ldt, b_re_bd, b_im_bd, c_re_bd, c_im_bd)


def _s5_pipe_kernel(bsz, reverse, u_ref, h0_ref, bmat_ref, cmat_ref, ab_ref, *rest):
    if reverse:
        y_ref, hfin_ref, x0_s, x1_s, hb0_s, hb1_s, hs_s = rest
    else:
        ul_ref, yb_ref, d_ref, y_ref, hfin_ref, x0_s, x1_s, hb0_s, hb1_s, hs_s = rest
    g = pl.program_id(0)
    rows_blk = S5_TS * bsz

    @pl.when(g == 0)
    def _():
        x1_s[...] = jnp.zeros_like(x1_s)
        hb0_s[...] = jnp.zeros_like(hb0_s)
        hb1_s[...] = jnp.zeros_like(hb1_s)
        hs_s[...] = jnp.zeros_like(hs_s)

    def half(k):
        first = (1 - k) if reverse else k
        return slice(first * S5_TS, (first + 1) * S5_TS)

    def emit(k, hb_s):
        y = jnp.dot(hb_s[...], cmat_ref[0], preferred_element_type=F32)
        if not reverse:
            y = (y + yb_ref[half(k), :, :].reshape(rows_blk, W_GRP)
                 + ul_ref[half(k), :, :].reshape(rows_blk, W_GRP).astype(F32) * d_ref[...])
        y_ref[half(k), :, :] = y.reshape(S5_TS, bsz, W_GRP)

    def project(k, x_s):
        u = u_ref[half(k), :, :].reshape(rows_blk, W_GRP)
        x_s[...] = jnp.dot(u, bmat_ref[0], preferred_element_type=F32)

    def scan(x_s, hb_s):
        for q in range(S5_NS // W_GRP):
            cr = slice(W_GRP * q, W_GRP * (q + 1))
            cim = slice(S5_NS + W_GRP * q, S5_NS + W_GRP * (q + 1))
            a_re = jnp.broadcast_to(ab_ref[0, :, cr], (bsz, W_GRP))
            a_im = jnp.broadcast_to(ab_ref[0, :, cim], (bsz, W_GRP))
            h_re = hs_s[:, cr]
            h_im = hs_s[:, cim]
            for kk in range(S5_TS):
                t = (S5_TS - 1 - kk) if reverse else kk
                rows = slice(t * bsz, (t + 1) * bsz)
                h_re, h_im = (a_re * h_re - a_im * h_im + x_s[rows, cr],
                              a_re * h_im + a_im * h_re + x_s[rows, cim])
                hb_s[rows, cr] = h_re.astype(BF16)
                hb_s[rows, cim] = h_im.astype(BF16)
            hs_s[:, cr] = h_re
            hs_s[:, cim] = h_im

    emit(0, hb0_s)
    project(0, x0_s)
    scan(x1_s, hb1_s)
    hfin_ref[...] = hs_s[...]
    hs_s[...] = jnp.where(g == 0, h0_ref[...], hs_s[...])
    emit(1, hb1_s)
    project(1, x1_s)
    scan(x0_s, hb0_s)


def _s5_pipe_call(r, u3, h0, bmat, cmat, ab, post=None):
    L, bsz, _ = u3.shape
    npair = L // (2 * S5_TS)
    blk = (2 * S5_TS, bsz, W_GRP)
    reverse = r == 1

    def pair_of(c):
        return npair - 1 - c if reverse else c

    cur = lambda i: (pair_of(jnp.minimum(i, npair - 1)), 0, 0)
    lag = lambda i: (pair_of(jnp.maximum(i - 1, 0)), 0, 0)
    state = pl.BlockSpec((bsz, 2 * S5_NS), lambda i: (0, 0))
    in_specs = [pl.BlockSpec(blk, cur), state,
                pl.BlockSpec((1, W_GRP, 2 * S5_NS), lambda i: (r, 0, 0)),
                pl.BlockSpec((1, 2 * S5_NS, W_GRP), lambda i: (r, 0, 0)),
                pl.BlockSpec((1, 1, 2 * S5_NS), lambda i: (r, 0, 0))]
    args = [u3, h0, bmat, cmat, ab]
    if not reverse:
        yb, d = post
        in_specs += [pl.BlockSpec(blk, lag), pl.BlockSpec(blk, lag),
                     pl.BlockSpec((1, W_GRP), lambda i: (0, 0))]
        args += [u3, yb, d.reshape(1, -1)]
    xbuf = pltpu.VMEM((S5_TS * bsz, 2 * S5_NS), F32)
    hbuf = pltpu.VMEM((S5_TS * bsz, 2 * S5_NS), BF16)
    return pl.pallas_call(
        functools.partial(_s5_pipe_kernel, bsz, reverse),
        grid=(npair + 1,),
        in_specs=in_specs,
        out_specs=[pl.BlockSpec(blk, lag), state],
        out_shape=[jax.ShapeDtypeStruct((L, bsz, W_GRP), F32),
                   jax.ShapeDtypeStruct((bsz, 2 * S5_NS), F32)],
        scratch_shapes=[xbuf, xbuf, hbuf, hbuf, pltpu.VMEM((bsz, 2 * S5_NS), F32)],
        compiler_params=_cparams(1),
        name="s5_bwd" if reverse else "s5_fwd",
    )(*args)


def _s5_mixer(u5c, u5l, bsz, p):
    bmat, cmat, ab = _s5_prep_call(p['s5_a_re'], p['s5_a_im'], p['s5_log_dt'], p['s5_b_re'],
                                   p['s5_b_im'], p['s5_c_re'], p['s5_c_im'])
    Lc, L = u5c.shape[0], u5l.shape[0]
    u3c = u5c.reshape(Lc, bsz, W_GRP)
    u3l = u5l.reshape(L, bsz, W_GRP)
    h0 = jnp.zeros((bsz, 2 * S5_NS), F32)
    ybc, hbc = _s5_pipe_call(1, u3c, h0, bmat, cmat, ab)
    ybl, _ = _s5_pipe_call(1, u3l, hbc, bmat, cmat, ab)
    s5c, hfc = _s5_pipe_call(0, u3c, h0, bmat, cmat, ab, post=(ybc, p['s5_d']))
    s5l, _ = _s5_pipe_call(0, u3l, hfc, bmat, cmat, ab, post=(ybl, p['s5_d']))
    return s5c.reshape(Lc, bsz * W_GRP), s5l.reshape(L, bsz * W_GRP)


@functools.lru_cache(maxsize=None)
def _dft_tables(L):
    k = np.arange(L, dtype=np.int64)
    ft = (k[:, None] * k[None, :]) % (2 * L)
    ang = ft.astype(np.float64) * (math.pi / L)
    return np.cos(ang).astype(np.float32), np.sin(ang).astype(np.float32)


@functools.lru_cache(maxsize=None)
def _dft_tables_split(L):
    c, s = _dft_tables(L)
    return (np.concatenate([c[0::2], c[1::2]], axis=0), np.concatenate([s[0::2], s[1::2]], axis=0))


@functools.lru_cache(maxsize=None)
def _hyena_consts(L):
    t = np.linspace(0.0, 1.0, L, dtype=np.float32)[:, None]
    w = (2.0 * math.pi * np.arange(L, dtype=np.float32)[:, None] / L).astype(np.float32)
    bands = np.linspace(1e-4, HY_BANDS - 1, HY_BANDS, dtype=np.float32)[None, :]
    bw = (bands * w).astype(np.float32).astype(np.float64)
    feats = np.zeros((L, 128), np.float32)
    feats[:, 0:1] = t
    feats[:, 1:1 + HY_BANDS] = np.cos(bw)
    feats[:, 1 + HY_BANDS:HY_EMB] = -np.sin(bw)
    max_decay = math.log(1e-2) / 0.3
    min_decay = math.log(1e-2) / 1.5
    deltas = np.abs(np.linspace(min_decay, max_decay, 4 * W_GRP, dtype=np.float32))[None, :]
    return feats, deltas.astype(np.float32)


HY_RB = 256


def _hy_filter_kernel(feats_ref, w1_ref, b1_ref, fr_ref, w2_ref, b2_ref, w3_ref, del_ref,
                      p_ref, q_ref, nrm_ref, an_ref):
    i = pl.program_id(0)
    feats = feats_ref[...]
    fr = fr_ref[...]
    h = jnp.sin(fr * (_dot_f32(feats, w1_ref[...]) + b1_ref[...]))
    h = jnp.sin(fr * (_dot_f32(h, w2_ref[...]) + b2_ref[...]))
    h = _dot_f32(h, w3_ref[...])
    h = h * jnp.exp(-feats[:, 0:1] * del_ref[...])
    half = 2 * W_GRP
    hf = h[:, 0:half]
    row = lax.broadcasted_iota(jnp.int32, (HY_RB, half), 0) + i * HY_RB
    hb = jnp.where(row == 0, 0.0, h[:, half:2 * half])
    p = hf + hb
    sign = (1 - 2 * (row % 2)).astype(F32)
    p_ref[...] = p.astype(BF16)
    q_ref[...] = (hb - hf).astype(BF16)

    @pl.when(i == 0)
    def _():
        nrm_ref[...] = jnp.full_like(nrm_ref, EPS)
        an_ref[...] = jnp.zeros_like(an_ref)

    nrm_ref[...] += (jnp.sum(jnp.abs(hf), axis=0, keepdims=True)
                     + jnp.sum(jnp.abs(hb), axis=0, keepdims=True))
    an_ref[...] += jnp.sum(p * sign, axis=0, keepdims=True)


def _hy_spectrum_kernel(L, c_ref, s_ref, p_ref, q_ref, nrm_ref, ans_ref, a_ref, bc_ref, an_ref):
    i = pl.program_id(0)
    n = 2.0 * L
    inv = 1.0 / nrm_ref[...]
    row = lax.broadcasted_iota(jnp.int32, a_ref.shape, 0) + i * HY_RB
    wv = jnp.where(row == 0, 1.0 / n, 2.0 / n) * inv
    a_ref[...] = wv * jnp.dot(c_ref[...], p_ref[...], preferred_element_type=F32)
    bc_ref[...] = wv * jnp.dot(s_ref[...], q_ref[...], preferred_element_type=F32)
    an_ref[...] = ans_ref[...] * inv / n


def _hy_filter_call(L, w1, b1, freq, w2, b2, w3, cmat, smat):
    feats, deltas = _hyena_consts(L)
    w1p = jnp.pad(w1, ((0, 128 - HY_EMB), (0, 0)))
    half = 2 * W_GRP
    nb = L // HY_RB
    full = lambda a: pl.BlockSpec(a.shape, lambda i, nd=a.ndim: (0,) * nd)
    small = (w1p, b1.reshape(1, -1), freq.reshape(1, -1), w2, b2.reshape(1, -1), w3, jnp.asarray(deltas))
    rowblk = lambda w: pl.BlockSpec((HY_RB, w), lambda i: (i, 0))
    vec = pl.BlockSpec((1, half), lambda i: (0, 0))
    p, q, nrm, ans = pl.pallas_call(
        _hy_filter_kernel,
        grid=(nb,),
        in_specs=[rowblk(128)] + [full(a) for a in small],
        out_specs=[rowblk(half), rowblk(half), vec, vec],
        out_shape=[jax.ShapeDtypeStruct((L, half), BF16), jax.ShapeDtypeStruct((L, half), BF16),
                   jax.ShapeDtypeStruct((1, half), F32), jax.ShapeDtypeStruct((1, half), F32)],
        compiler_params=_cparams(1),
        name="hyena_filter",
    )(jnp.asarray(feats), *small)
    return pl.pallas_call(
        functools.partial(_hy_spectrum_kernel, L),
        grid=(nb,),
        in_specs=[rowblk(L), rowblk(L), full(p), full(q), vec, vec],
        out_specs=[rowblk(half), rowblk(half), vec],
        out_shape=[jax.ShapeDtypeStruct((L, half), F32), jax.ShapeDtypeStruct((L, half), F32),
                   jax.ShapeDtypeStruct((1, half), F32)],
        compiler_params=_cparams(1),
        name="hyena_spectrum",
    )(cmat, smat, p, q, nrm, ans)


def _alt_sign(shape):
    return (1 - 2 * (lax.broadcasted_iota(jnp.int32, shape, 0) % 2)).astype(F32)


def _reverse_shift(x, j_ref):
    hi = x.astype(BF16)
    lo = (x - hi.astype(F32)).astype(BF16)
    j = j_ref[...]
    return (jnp.dot(j, hi, preferred_element_type=F32) + jnp.dot(j, lo, preferred_element_type=F32))


def _hy_conv_kernel(L, u_ref, cw_ref, cb_ref, ce_ref, se_ref, co_ref, so_ref, cot_ref, sot_ref, j_ref,
                    a_ref, bc_ref, an_ref, bias_ref, o_ref,
                    x1_s, x2_s, z_s, zs_s, zd_s, xe_s, ye_s, xo_s, yo_s, d_s, acc_s):
    nc = L // T
    H = L // 2
    fb = min(256, H)
    nfb = H // fb
    rb = min(1024, H)
    nrb = H // rb

    def conv(c, carry):
        s = pl.multiple_of(c * T, T)
        rows = pl.ds(s, T)
        prev, nxt = _halo_rows(u_ref, s, c, nc, L)
        y = _dwconv_chunk(u_ref[0, rows, :].astype(F32), prev, nxt, cw_ref, cb_ref)
        x1_s[rows, :] = y[:, 0:W_GRP]
        x2_s[rows, :] = y[:, W_GRP:2 * W_GRP]
        z_s[rows, :] = y[:, 2 * W_GRP:3 * W_GRP]
        return carry

    lax.fori_loop(0, nc, conv, 0)
    row0 = lax.broadcasted_iota(jnp.int32, (fb, W_GRP), 0) == 0

    for o, gate_s in enumerate((x1_s, x2_s)):
        cols = slice(W_GRP * o, W_GRP * (o + 1))
        acc_s[...] = jnp.zeros_like(acc_s)

        ws = [z_s[H + (nfb - 1 - b) * fb:H + (nfb - b) * fb, :] for b in range(nfb)]
        revs = [_reverse_shift(w, j_ref) for w in ws]
        qn = jnp.zeros((1, W_GRP), F32)
        for b in range(nfb):
            top = z_s[b * fb:(b + 1) * fb, :]
            edge = z_s[H + (nfb - b) * fb:H + (nfb - b) * fb + 1, :] if b > 0 else 0.0
            zr = jnp.where(row0, edge, revs[b])
            zs_s[b * fb:(b + 1) * fb, :] = (top + zr).astype(BF16)
            zd_s[b * fb:(b + 1) * fb, :] = (top - zr).astype(BF16)
            qn = qn + jnp.sum((top + ws[b]) * _alt_sign(top.shape), axis=0, keepdims=True)
        acc_s[0:1, :] = qn
        z_mid = z_s[H:H + 1, :]

        def fwd(j, carry, cols=cols, z_mid=z_mid):
            rows = pl.ds(pl.multiple_of(j * rb, rb), rb)
            odd_rows = pl.ds(pl.multiple_of(H + j * rb, rb), rb)
            zs = zs_s[...]
            zd = zd_s[...]
            mid = _alt_sign((rb, W_GRP)) * z_mid
            pe = jnp.dot(ce_ref[rows, :], zs, preferred_element_type=F32) + mid
            qo = jnp.dot(so_ref[rows, :], zs, preferred_element_type=F32) + mid
            po = jnp.dot(co_ref[rows, :], zd, preferred_element_type=F32)
            qe = jnp.dot(se_ref[rows, :], zd, preferred_element_type=F32)
            ae = a_ref[rows, cols]
            bce = bc_ref[rows, cols]
            ao = a_ref[odd_rows, cols]
            bco = bc_ref[odd_rows, cols]
            xe = pe * ae + qe * bce
            yo = qo * ao - po * bco
            xe_s[rows, :] = xe.astype(BF16)
            ye_s[rows, :] = (qe * ae - pe * bce).astype(BF16)
            xo_s[rows, :] = (po * ao + qo * bco).astype(BF16)
            yo_s[rows, :] = yo.astype(BF16)
            acc_s[1:2, :] += jnp.sum((xe + yo) * _alt_sign(xe.shape), axis=0, keepdims=True)
            return carry

        lax.fori_loop(0, nrb, fwd, 0)
        nyq = acc_s[0:1, :] * an_ref[:, cols]

        def finish(rows, y, cols=cols, gate_s=gate_s, nyq=nyq, o=o):
            res = gate_s[rows, :] * (y + _alt_sign(y.shape) * nyq + z_s[rows, :] * bias_ref[:, cols])
            if o == 0:
                z_s[rows, :] = res
            else:
                o_ref[0, rows, :] = res.astype(o_ref.dtype)

        def inv(j, carry, finish=finish):
            rows = pl.ds(pl.multiple_of(j * rb, rb), rb)
            u1 = (jnp.dot(ce_ref[rows, :], xe_s[...], preferred_element_type=F32)
                  + jnp.dot(sot_ref[rows, :], yo_s[...], preferred_element_type=F32))
            u2 = (jnp.dot(cot_ref[rows, :], xo_s[...], preferred_element_type=F32)
                  + jnp.dot(se_ref[rows, :], ye_s[...], preferred_element_type=F32))
            d_s[rows, :] = u1 - u2
            finish(rows, u1 + u2)
            return carry

        lax.fori_loop(0, nrb, inv, 0)
        y_mid = acc_s[1:2, :]

        drevs = [_reverse_shift(d_s[(nfb - 1 - b) * fb:(nfb - b) * fb, :], j_ref) for b in range(nfb)]
        for b in range(nfb):
            edge = d_s[(nfb - b) * fb:(nfb - b) * fb + 1, :] if b > 0 else y_mid
            finish(pl.ds(H + b * fb, fb), jnp.where(row0, edge, drevs[b]))


@functools.lru_cache(maxsize=None)
def _dft_half_tables(L):
    c, s = _dft_tables(L)
    H = L // 2
    fb = min(256, H)
    ce, co = c[0::2, :H], c[1::2, :H]
    se, so = s[0::2, :H], s[1::2, :H]
    j = np.zeros((fb, fb), np.float32)
    for r in range(1, fb):
        j[r, fb - r] = 1.0
    return tuple(np.ascontiguousarray(m) for m in (ce, se, co, so, co.T, so.T, j))


def _hy_conv_call(u, conv_w, conv_b, tables, a, bc, an, bias):
    bsz, L, _ = u.shape
    H = L // 2
    half = 2 * W_GRP
    mats = [jnp.asarray(m).astype(BF16) for m in tables]
    return pl.pallas_call(
        functools.partial(_hy_conv_kernel, L),
        grid=(bsz,),
        in_specs=[pl.BlockSpec((1, L, HY_COLS), lambda b: (b, 0, 0)),
                  pl.BlockSpec((3, HY_COLS), lambda b: (0, 0)),
                  pl.BlockSpec((1, HY_COLS), lambda b: (0, 0))]
                 + [_const_spec(m.shape) for m in mats]
                 + [_const_spec((L, half)), _const_spec((L, half)),
                    pl.BlockSpec((1, half), lambda b: (0, 0)),
                    pl.BlockSpec((1, half), lambda b: (0, 0))],
        out_specs=pl.BlockSpec((1, L, W_GRP), lambda b: (b, 0, 0)),
        out_shape=jax.ShapeDtypeStruct((bsz, L, W_GRP), BF16),
        scratch_shapes=[pltpu.VMEM((L, W_GRP), F32), pltpu.VMEM((L, W_GRP), F32),
                        pltpu.VMEM((L, W_GRP), F32),
                        pltpu.VMEM((H, W_GRP), BF16), pltpu.VMEM((H, W_GRP), BF16),
                        pltpu.VMEM((H, W_GRP), BF16), pltpu.VMEM((H, W_GRP), BF16),
                        pltpu.VMEM((H, W_GRP), BF16), pltpu.VMEM((H, W_GRP), BF16),
                        pltpu.VMEM((H, W_GRP), F32), pltpu.VMEM((8, W_GRP), F32)],
        compiler_params=_cparams(1),
        name="hyena_conv",
    )(u, conv_w, conv_b.reshape(1, -1), *mats, a, bc, an, bias.reshape(1, -1))


def _hyena_mixer(u, p):
    L = u.shape[1]
    cnp, snp = _dft_tables_split(L)
    cmat = jnp.asarray(cnp).astype(BF16)
    smat = jnp.asarray(snp).astype(BF16)
    a, bc, an = _hy_filter_call(L, p['hy_w1'], p['hy_b1'], p['hy_freq'], p['hy_w2'], p['hy_b2'],
                                p['hy_w3'], cmat, smat)
    return _hy_conv_call(u, p['hy_conv_w'], p['hy_conv_b'], _dft_half_tables(L), a, bc, an, p['hy_bias'])


def _out_ffn_kernel(final, nb, h_ref, ssd_ref, hy_ref, ret_ref, s5_ref, gw_ref, gb_ref, mod_ref, g2_ref,
                    wo_ref, wup_ref, wdn_ref, fg_ref, o_ref, acc_s):
    tm = h_ref.shape[1]
    rows = nb * tm

    def fold(ref):
        return ref[...].reshape(rows, ref.shape[2])

    s5 = s5_ref[...]
    s5 = jnp.concatenate([s5[:, W_GRP * j:W_GRP * (j + 1)] for j in range(nb)], axis=0)
    s5 = jax.nn.gelu(s5)
    s5 = s5 * jax.nn.sigmoid(jnp.dot(s5.astype(BF16), gw_ref[...], preferred_element_type=F32)
                             + gb_ref[...])
    y = jnp.zeros((rows, D_MODEL), F32)
    for j, blk in enumerate((fold(ssd_ref), fold(hy_ref), fold(ret_ref), s5)):
        y = y + jnp.dot(blk.astype(BF16), wo_ref[W_GRP * j:W_GRP * (j + 1), :],
                        preferred_element_type=F32)
    h1 = fold(h_ref) + mod_ref[0, 2:3, :] * y
    xn = h1 * lax.rsqrt(jnp.mean(h1 * h1, axis=-1, keepdims=True) + EPS) * g2_ref[...]
    xm = (xn * (1.0 + mod_ref[0, 4:5, :]) + mod_ref[0, 3:4, :]).astype(BF16)
    fc = 256
    for j in range(D_FF // fc):
        gg = jnp.dot(xm, wup_ref[:, fc * j:fc * (j + 1)], preferred_element_type=F32)
        uu = jnp.dot(xm, wup_ref[:, D_FF + fc * j:D_FF + fc * (j + 1)], preferred_element_type=F32)
        part = jnp.dot((_silu(gg) * uu).astype(BF16), wdn_ref[fc * j:fc * (j + 1), :],
                       preferred_element_type=F32)
        if j == 0:
            acc_s[...] = part
        else:
            acc_s[...] += part
    h2 = h1 + mod_ref[0, 5:6, :] * acc_s[...]
    if final:
        h2 = h2 * lax.rsqrt(jnp.mean(h2 * h2, axis=-1, keepdims=True) + EPS) * fg_ref[...]
    o_ref[...] = h2.reshape(nb, tm, D_MODEL)


def _out_ffn_call(h, mix, glu, mods, ctx_stream, g2, wo, wup, wdn, final_g, final):
    bsz, L, _ = h.shape
    tm = min(512, L)
    nb = _fold_batches(L, 512) if ctx_stream else 1
    mod_map = (lambda b, i: (bsz, 0, 0)) if ctx_stream else (lambda b, i: (b, 0, 0))
    tok = lambda w: pl.BlockSpec((nb, tm, w), lambda b, i: (b, i, 0))
    glu_w, glu_b = glu
    return pl.pallas_call(
        functools.partial(_out_ffn_kernel, final, nb),
        grid=(bsz // nb, L // tm),
        in_specs=[tok(D_MODEL), tok(W_GRP), tok(W_GRP), tok(W_GRP),
                  pl.BlockSpec((tm, nb * W_GRP), lambda b, i: (i, b)),
                  pl.BlockSpec((W_GRP, W_GRP), lambda b, i: (0, 0)),
                  pl.BlockSpec((1, W_GRP), lambda b, i: (0, 0)),
                  pl.BlockSpec((1, 6, D_MODEL), mod_map),
                  pl.BlockSpec((1, D_MODEL), lambda b, i: (0, 0)),
                  _const_spec((D_MODEL, D_MODEL)),
                  _const_spec((D_MODEL, 2 * D_FF)),
                  _const_spec((D_FF, D_MODEL)),
                  pl.BlockSpec((1, D_MODEL), lambda b, i: (0, 0))],
        out_specs=tok(D_MODEL),
        out_shape=jax.ShapeDtypeStruct((bsz, L, D_MODEL), F32),
        scratch_shapes=[pltpu.VMEM((nb * tm, D_MODEL), F32)],
        compiler_params=_cparams(2),
        name="out_ffn",
    )(h, *mix, glu_w.astype(BF16), glu_b.reshape(1, -1), mods, g2, wo, wup, wdn, final_g)


def kernel(x, c, ctx, c_ctx, mod_w, mod_b, norm1_g, norm2_g, w_in, w_out, ssd_conv_w, ssd_conv_b, ssd_a_log, ssd_dt_bias, ssd_d, ssd_norm_g, hy_conv_w, hy_conv_b, hy_w1, hy_b1, hy_freq, hy_w2, hy_b2, hy_w3, hy_bias, ret_decay, s5_a_re, s5_a_im, s5_log_dt, s5_b_re, s5_b_im, s5_c_re, s5_c_im, s5_d, s5_glu_w, s5_glu_b, ffn_w_up, ffn_w_down, final_norm_g):
    bsz = x.shape[0]
    depth = mod_w.shape[0]
    sc = jnp.concatenate([c, c_ctx[None, :], jnp.zeros((MOD_ROWS - bsz - 1, D_MODEL), F32)], axis=0)
    mods_all = _mod_call(sc, mod_w, mod_b).reshape(depth, MOD_ROWS, 6, D_MODEL)
    fg = final_norm_g.reshape(1, -1)
    h_l, h_c = x, ctx
    o_xbc, o_dt, o_hy = W_GRP, W_GRP + SSD_XBC, W_GRP + SSD_XBC + 2 * N_HEADS
    o_ret = o_hy + HY_COLS
    o_s5 = o_ret + RET_COLS
    for i in range(depth):
        last = i == depth - 1
        wi = w_in[i]
        wdt = wi[:, o_dt:o_hy]
        wcat = jnp.concatenate([wi[:, 0:o_dt], wi[:, o_hy:o_s5 + W_GRP], wdt,
                                jnp.zeros((D_MODEL, 128 - 2 * N_HEADS), F32)], axis=1).astype(BF16)
        mods = mods_all[i]
        g1 = norm1_g[i].reshape(1, -1)
        zl, xbcl, hyl, retl, dtl, s5l = _inproj_call(h_l, g1, mods, False, wcat)
        zc, xbcc, hyc, retc, dtc, s5c = _inproj_call(h_c, g1, mods, True, wcat)
        ssd_out = _ssd_call((zc, xbcc, dtc), (zl, xbcl, dtl), ssd_conv_w[i], ssd_conv_b[i],
                            ssd_a_log[i], ssd_dt_bias[i], ssd_d[i], ssd_norm_g[i], not last)
        ret_out = _ret_call(retc, retl, ret_decay[i], not last)
        ssd_c, ssd_l = ssd_out if not last else (None, ssd_out[0])
        ret_c, ret_l = ret_out if not last else (None, ret_out[0])
        p = dict(s5_a_re=s5_a_re[i], s5_a_im=s5_a_im[i], s5_log_dt=s5_log_dt[i], s5_b_re=s5_b_re[i],
                 s5_b_im=s5_b_im[i], s5_c_re=s5_c_re[i], s5_c_im=s5_c_im[i], s5_d=s5_d[i],
                 s5_glu_w=s5_glu_w[i], s5_glu_b=s5_glu_b[i],
                 hy_conv_w=hy_conv_w[i], hy_conv_b=hy_conv_b[i], hy_w1=hy_w1[i], hy_b1=hy_b1[i],
                 hy_freq=hy_freq[i], hy_w2=hy_w2[i], hy_b2=hy_b2[i], hy_w3=hy_w3[i], hy_bias=hy_bias[i])
        s5_c, s5_l = _s5_mixer(s5c, s5l, bsz, p)
        hy_l = _hyena_mixer(hyl, p)
        g2 = norm2_g[i].reshape(1, -1)
        wo = w_out[i].astype(BF16)
        wup = ffn_w_up[i].astype(BF16)
        wdn = ffn_w_down[i].astype(BF16)
        glu = (s5_glu_w[i], s5_glu_b[i])
        h_l = _out_ffn_call(h_l, (ssd_l, hy_l, ret_l, s5_l), glu, mods, False, g2, wo, wup, wdn, fg, last)
        if not last:
            hy_c = _hyena_mixer(hyc, p)
            h_c = _out_ffn_call(h_c, (ssd_c, hy_c, ret_c, s5_c), glu, mods, True, g2, wo, wup, wdn, fg, False)
    return h_l
```
